```python
import jax, jax.numpy as jnp
from jax import lax
import numpy as np

D_MODEL = 1024
BATCH = 16
SEQ = 2048
DEPTH = 2

D_MIX = D_MODEL
N_MIXERS = 4
GROUP_WIDTH = D_MIX // N_MIXERS
HEAD_DIM = 64
EPS = 1e-6
MASK_VALUE = -1e30

FOX_HEADS = GROUP_WIDTH // HEAD_DIM
FOX_BLOCK = 128

SSD_HEADS = GROUP_WIDTH // HEAD_DIM
SSD_GROUPS = 2
SSD_STATE = 64
SSD_CONV = 4
SSD_CHUNK = 128
SSD_CONV_DIM = GROUP_WIDTH + 2 * SSD_GROUPS * SSD_STATE

GLA_HEADS = 4
GLA_KEY_DIM = GROUP_WIDTH // 2
GLA_HEAD_K = GLA_KEY_DIM // GLA_HEADS
GLA_HEAD_V = GROUP_WIDTH // GLA_HEADS
GLA_GATE_RANK = 16
GLA_TAU = 16.0
GLA_CHUNK = 64

HGRN_HEADS = 4
HGRN_EXPAND = 64
HGRN_FDIM = HGRN_HEADS * HGRN_EXPAND
HGRN_HEAD_V = GROUP_WIDTH // HGRN_HEADS
HGRN_CHUNK = 16

N_MEM = 256
XA_HEADS = 4
XA_HEAD_DIM = D_MODEL // XA_HEADS

D_FF = 256 * ((8 * D_MODEL // 3 + 255) // 256)
FFN_CONV = 3

IN_SPLITS = (
    GROUP_WIDTH, GROUP_WIDTH, GROUP_WIDTH, FOX_HEADS,
    GROUP_WIDTH, SSD_CONV_DIM, SSD_HEADS,
    GLA_KEY_DIM, GLA_KEY_DIM, GROUP_WIDTH, GLA_GATE_RANK, GROUP_WIDTH,
    HGRN_FDIM, HGRN_FDIM, GROUP_WIDTH, GROUP_WIDTH,
)
D_IN = sum(IN_SPLITS)

kernel_name = 'hybrid_fox_ssd_gla_hgrn2_block'


def rms_norm(x, w):
    xf = x.astype(jnp.float32)
    y = xf * lax.rsqrt(jnp.mean(xf * xf, axis=-1, keepdims=True) + EPS)
    return (y * w.astype(jnp.float32)).astype(x.dtype)


def split_columns(t, sizes):
    cuts = np.cumsum(np.asarray(sizes))[:-1].tolist()
    return jnp.split(t, cuts, axis=-1)


def causal_depthwise_conv(x, w, bias):
    k_width, c = w.shape
    y = lax.conv_general_dilated(
        x, w[:, None, :].astype(x.dtype), window_strides=(1,), padding=((k_width - 1, 0),),
        dimension_numbers=('NWC', 'WIO', 'NWC'), feature_group_count=c)
    return y + bias.astype(y.dtype)


def scan_chunk_states(d_state, decay):
    def step(s, inp):
        ds, g = inp
        return s * g + ds, s
    s0 = jnp.zeros_like(d_state[:, 0])
    _, s_in = lax.scan(step, s0, (jnp.moveaxis(d_state, 1, 0), jnp.moveaxis(decay, 1, 0)))
    return jnp.moveaxis(s_in, 0, 1)


def chunked_gated_linear_attention(q, k, v, log_decay, chunk):
    b, t, h, dk = q.shape
    dv = v.shape[-1]
    n = t // chunk
    f32 = jnp.float32
    q = q.astype(f32).reshape(b, n, chunk, h, dk)
    k = k.astype(f32).reshape(b, n, chunk, h, dk)
    v = v.astype(f32).reshape(b, n, chunk, h, dv)
    g_cs = jnp.cumsum(log_decay.astype(f32).reshape(b, n, chunk, h, dk), axis=2)
    q_dec = q * jnp.exp(g_cs)
    k_inv = k * jnp.exp(-g_cs)
    k_end = k * jnp.exp(g_cs[:, :, -1:] - g_cs)
    causal = jnp.tril(jnp.ones((chunk, chunk), bool))
    att = jnp.where(causal, jnp.einsum('bnihd,bnjhd->bnhij', q_dec, k_inv), 0.0)
    o = jnp.einsum('bnhij,bnjhv->bnihv', att, v)
    d_state = jnp.einsum('bnjhd,bnjhv->bnhdv', k_end, v)
    s_in = scan_chunk_states(d_state, jnp.exp(g_cs[:, :, -1])[..., None])
    o = o + jnp.einsum('bnihd,bnhdv->bnihv', q_dec, s_in)
    return o.reshape(b, t, h, dv)


def forgetting_attention(q, k, v, f_logit, q_norm, k_norm, o_norm):
    b, t, _ = q.shape
    q = rms_norm(q.reshape(b, t, FOX_HEADS, HEAD_DIM), q_norm)
    k = rms_norm(k.reshape(b, t, FOX_HEADS, HEAD_DIM), k_norm)
    v = v.reshape(b, t, FOX_HEADS, HEAD_DIM)
    cum_log_f = jnp.cumsum(jax.nn.log_sigmoid(f_logit.astype(jnp.float32)), axis=1)
    cum_log_f = jnp.transpose(cum_log_f, (0, 2, 1))
    scale = HEAD_DIM ** -0.5
    pos = jnp.arange(t)
    outs = []
    for start in range(0, t, FOX_BLOCK):
        end = start + FOX_BLOCK
        logits = jnp.einsum('bqhd,bkhd->bhqk', q[:, start:end], k[:, :end]).astype(jnp.float32) * scale
        logits = logits + cum_log_f[:, :, start:end, None] - cum_log_f[:, :, None, :end]
        causal = pos[start:end, None] >= pos[None, :end]
        probs = jax.nn.softmax(jnp.where(causal, logits, MASK_VALUE), axis=-1)
        outs.append(jnp.einsum('bhqk,bkhd->bqhd', probs.astype(v.dtype), v[:, :end]))
    o = rms_norm(jnp.concatenate(outs, axis=1), o_norm)
    return o.reshape(b, t, GROUP_WIDTH)


def ssd_chunked_scan(x, dt, a, bm, cm):
    b, t, h, p = x.shape
    n = t // SSD_CHUNK
    rep = h // bm.shape[2]
    cs = (b, n, SSD_CHUNK)
    bh = jnp.repeat(bm.astype(jnp.float32), rep, axis=2).reshape(*cs, h, SSD_STATE)
    ch = jnp.repeat(cm.astype(jnp.float32), rep, axis=2).reshape(*cs, h, SSD_STATE)
    xdt = (x * dt[..., None]).reshape(*cs, h, p)
    a_cs = jnp.cumsum((dt * a).reshape(*cs, h), axis=2)
    causal = jnp.tril(jnp.ones((SSD_CHUNK, SSD_CHUNK), bool))[:, :, None]
    seg = a_cs[:, :, :, None, :] - a_cs[:, :, None, :, :]
    decay = jnp.where(causal, jnp.exp(jnp.where(causal, seg, 0.0)), 0.0)
    scores = jnp.einsum('bnihs,bnjhs->bnijh', ch, bh) * decay
    y = jnp.einsum('bnijh,bnjhp->bnihp', scores, xdt)
    to_end = jnp.exp(a_cs[:, :, -1:, :] - a_cs)
    d_state = jnp.einsum('bnjhs,bnjhp->bnhps', bh * to_end[..., None], xdt)
    s_in = scan_chunk_states(d_state, jnp.exp(a_cs[:, :, -1, :])[..., None, None])
    y = y + jnp.einsum('bnihs,bnhps->bnihp', ch * jnp.exp(a_cs)[..., None], s_in)
    return y.reshape(b, t, h, p)


def mamba2_mixer(z, xbc, dt_raw, conv_w, conv_b, dt_bias, a_log, d_skip, g_norm):
    b, t, _ = z.shape
    f32 = jnp.float32
    xbc = jax.nn.silu(causal_depthwise_conv(xbc, conv_w, conv_b))
    xs, bm, cm = split_columns(xbc, (GROUP_WIDTH, SSD_GROUPS * SSD_STATE, SSD_GROUPS * SSD_STATE))
    xs = xs.reshape(b, t, SSD_HEADS, HEAD_DIM).astype(f32)
    dt = jax.nn.softplus(dt_raw.astype(f32) + dt_bias.astype(f32))
    a = -jnp.exp(a_log.astype(f32))
    y = ssd_chunked_scan(xs, dt, a, bm.reshape(b, t, SSD_GROUPS, SSD_STATE),
                         cm.reshape(b, t, SSD_GROUPS, SSD_STATE))
    y = y + xs * d_skip.astype(f32)[:, None]
    y = y.reshape(b, t, GROUP_WIDTH) * jax.nn.silu(z.astype(f32))
    y = rms_norm(y.reshape(b, t, SSD_GROUPS, GROUP_WIDTH // SSD_GROUPS),
                 g_norm.reshape(SSD_GROUPS, GROUP_WIDTH // SSD_GROUPS))
    return y.reshape(b, t, GROUP_WIDTH).astype(z.dtype)


def gla_mixer(q, k, v, gate_lr, r, w_gate_up, b_gate, o_norm):
    b, t, _ = q.shape
    log_alpha = jax.nn.log_sigmoid((gate_lr @ w_gate_up + b_gate).astype(jnp.float32)) / GLA_TAU
    shp = (b, t, GLA_HEADS, GLA_HEAD_K)
    o = chunked_gated_linear_attention(q.reshape(shp) * GLA_HEAD_K ** -0.5, k.reshape(shp),
                                       v.reshape(b, t, GLA_HEADS, GLA_HEAD_V), log_alpha.reshape(shp), GLA_CHUNK)
    o = rms_norm(o, o_norm).reshape(b, t, GROUP_WIDTH)
    return (o * jax.nn.silu(r.astype(jnp.float32))).astype(q.dtype)


def hgrn2_mixer(q, f_logit, inp, g, lower_bound, o_norm):
    b, t, _ = q.shape
    z = f_logit.astype(jnp.float32)
    lb = lower_bound.astype(jnp.float32)
    log_f = jnp.log(lb + (1.0 - lb) * jax.nn.sigmoid(z))
    one_minus_f = (1.0 - lb) * jax.nn.sigmoid(-z)
    shp = (b, t, HGRN_HEADS, HGRN_EXPAND)
    o = chunked_gated_linear_attention(q.reshape(shp), one_minus_f.reshape(shp),
                                       inp.reshape(b, t, HGRN_HEADS, HGRN_HEAD_V), log_f.reshape(shp), HGRN_CHUNK)
    o = rms_norm(o, o_norm).reshape(b, t, GROUP_WIDTH)
    return (o * jax.nn.silu(g.astype(jnp.float32))).astype(q.dtype)


def memory_cross_attention(h, mem_n, w_q, w_kv, w_o, q_norm, k_norm):
    b, t, _ = h.shape
    m = mem_n.shape[1]
    q = rms_norm((h @ w_q).reshape(b, t, XA_HEADS, XA_HEAD_DIM), q_norm)
    k, v = jnp.split(mem_n @ w_kv, 2, axis=-1)
    k = rms_norm(k.reshape(b, m, XA_HEADS, XA_HEAD_DIM), k_norm)
    v = v.reshape(b, m, XA_HEADS, XA_HEAD_DIM)
    logits = jnp.einsum('bthd,bmhd->bhtm', q, k).astype(jnp.float32) * XA_HEAD_DIM ** -0.5
    probs = jax.nn.softmax(logits, axis=-1).astype(v.dtype)
    o = jnp.einsum('bhtm,bmhd->bthd', probs, v).reshape(b, t, D_MODEL)
    return o @ w_o


def conv_glu_ffn(h, w_up, conv_w, conv_b, w_down):
    gate, val = jnp.split(h @ w_up, 2, axis=-1)
    gate = causal_depthwise_conv(gate, conv_w, conv_b)
    return (jax.nn.silu(gate) * val) @ w_down


def setup_inputs(seed: int = 0) -> dict:
    key = jax.random.key(seed)
    ks = iter(jax.random.split(key, 48))
    f32 = jnp.float32
    L = DEPTH

    def nrm(shape, scale):
        return jax.random.normal(next(ks), shape, f32) * scale

    def gain(shape):
        return 1.0 + nrm(shape, 0.02)

    def unif(shape, lo, hi):
        return jax.random.uniform(next(ks), shape, f32, lo, hi)

    x = nrm((BATCH, SEQ, D_MODEL), 1.0)
    mem = nrm((BATCH, N_MEM, D_MODEL), 1.0)
    ln_mix = gain((L, D_MODEL))
    w_in = nrm((L, D_MODEL, D_IN), D_MODEL ** -0.5)
    w_out = nrm((L, D_MIX, D_MODEL), D_MIX ** -0.5)
    fox_f_bias = unif((L, FOX_HEADS), 1.0, 4.0)
    fox_qn = gain((L, HEAD_DIM))
    fox_kn = gain((L, HEAD_DIM))
    fox_on = gain((L, HEAD_DIM))
    ssd_conv_w = nrm((L, SSD_CONV, SSD_CONV_DIM), SSD_CONV ** -0.5)
    ssd_conv_b = nrm((L, SSD_CONV_DIM), 0.02)
    dt0 = jnp.exp(unif((L, SSD_HEADS), float(np.log(1e-3)), float(np.log(1e-1))))
    ssd_dt_bias = dt0 + jnp.log(-jnp.expm1(-dt0))
    ssd_a_log = jnp.log(unif((L, SSD_HEADS), 1.0, 16.0))
    ssd_d = gain((L, SSD_HEADS))
    ssd_norm = gain((L, GROUP_WIDTH))
    gla_w_g2 = nrm((L, GLA_GATE_RANK, GLA_KEY_DIM), GLA_GATE_RANK ** -0.5)
    gla_b_g2 = nrm((L, GLA_KEY_DIM), 0.02)
    gla_norm = gain((L, GLA_HEAD_V))
    hgrn_lb_logits = nrm((L, HGRN_FDIM), 0.5)
    hgrn_norm = gain((L, HGRN_HEAD_V))
    ln_xattn = gain((L, D_MODEL))
    mem_norm = gain((D_MODEL,))
    xa_wq = nrm((L, D_MODEL, D_MODEL), D_MODEL ** -0.5)
    xa_wkv = nrm((L, D_MODEL, 2 * D_MODEL), D_MODEL ** -0.5)
    xa_wo = nrm((L, D_MODEL, D_MODEL), D_MODEL ** -0.5)
    xa_qn = gain((L, XA_HEAD_DIM))
    xa_kn = gain((L, XA_HEAD_DIM))
    ln_ffn = gain((L, D_MODEL))
    ffn_w_up = nrm((L, D_MODEL, 2 * D_FF), D_MODEL ** -0.5)
    ffn_conv_w = nrm((L, FFN_CONV, D_FF), FFN_CONV ** -0.5)
    ffn_conv_b = nrm((L, D_FF), 0.02)
    ffn_w_down = nrm((L, D_FF, D_MODEL), D_FF ** -0.5)
    return {
        'x': x, 'mem': mem, 'ln_mix': ln_mix, 'w_in': w_in, 'w_out': w_out,
        'fox_f_bias': fox_f_bias, 'fox_qn': fox_qn, 'fox_kn': fox_kn, 'fox_on': fox_on,
        'ssd_conv_w': ssd_conv_w, 'ssd_conv_b': ssd_conv_b, 'ssd_dt_bias': ssd_dt_bias,
        'ssd_a_log': ssd_a_log, 'ssd_d': ssd_d, 'ssd_norm': ssd_norm,
        'gla_w_g2': gla_w_g2, 'gla_b_g2': gla_b_g2, 'gla_norm': gla_norm,
        'hgrn_lb_logits': hgrn_lb_logits, 'hgrn_norm': hgrn_norm,
        'ln_xattn': ln_xattn, 'mem_norm': mem_norm, 'xa_wq': xa_wq, 'xa_wkv': xa_wkv,
        'xa_wo': xa_wo, 'xa_qn': xa_qn, 'xa_kn': xa_kn,
        'ln_ffn': ln_ffn, 'ffn_w_up': ffn_w_up, 'ffn_conv_w': ffn_conv_w,
        'ffn_conv_b': ffn_conv_b, 'ffn_w_down': ffn_w_down,
    }


def reference(x, mem, ln_mix, w_in, w_out, fox_f_bias, fox_qn, fox_kn, fox_on,
              ssd_conv_w, ssd_conv_b, ssd_dt_bias, ssd_a_log, ssd_d, ssd_norm,
              gla_w_g2, gla_b_g2, gla_norm, hgrn_lb_logits, hgrn_norm,
              ln_xattn, mem_norm, xa_wq, xa_wkv, xa_wo, xa_qn, xa_kn,
              ln_ffn, ffn_w_up, ffn_conv_w, ffn_conv_b, ffn_w_down):
    mem_n = rms_norm(mem, mem_norm)
    s = jax.nn.softmax(hgrn_lb_logits.astype(jnp.float32), axis=0)
    lower_bounds = jnp.concatenate([jnp.zeros_like(s[:1]), jnp.cumsum(s[1:], axis=0)], axis=0)
    lower_bounds = jnp.clip(lower_bounds, 0.0, 1.0 - 1e-6)
    for l in range(DEPTH):
        h = rms_norm(x, ln_mix[l])
        (fq, fk, fv, ff, sz, sxbc, sdt, gq, gk, gv, ga, gr, hq, hf, hi, hg) = split_columns(h @ w_in[l], IN_SPLITS)
        y_fox = forgetting_attention(fq, fk, fv, ff + fox_f_bias[l], fox_qn[l], fox_kn[l], fox_on[l])
        y_ssd = mamba2_mixer(sz, sxbc, sdt, ssd_conv_w[l], ssd_conv_b[l], ssd_dt_bias[l],
                             ssd_a_log[l], ssd_d[l], ssd_norm[l])
        y_gla = gla_mixer(gq, gk, gv, ga, gr, gla_w_g2[l], gla_b_g2[l], gla_norm[l])
        y_hgrn = hgrn2_mixer(hq, hf, hi, hg, lower_bounds[l], hgrn_norm[l])
        mixed = jnp.concatenate([y_fox.astype(x.dtype), y_ssd.astype(x.dtype),
                                 y_gla.astype(x.dtype), y_hgrn.astype(x.dtype)], axis=-1)
        x = x + mixed @ w_out[l]
        x = x + memory_cross_attention(rms_norm(x, ln_xattn[l]), mem_n, xa_wq[l], xa_wkv[l],
                                       xa_wo[l], xa_qn[l], xa_kn[l])
        x = x + conv_glu_ffn(rms_norm(x, ln_ffn[l]), ffn_w_up[l], ffn_conv_w[l], ffn_conv_b[l], ffn_w_down[l])
    return x
```

```python
import functools

import jax
import jax.numpy as jnp
from jax import lax
from jax.experimental import pallas as pl
from jax.experimental.pallas import tpu as pltpu

F32 = jnp.float32
BF16 = jnp.bfloat16

EPS = 1e-6
MASK_VALUE = -1e30
D_MODEL = 1024
GROUP_WIDTH = 256
HEAD_DIM = 64
LANES = 128
N_HEADS = 4
SSD_STATE = 64
SSD_CONV = 4
SSD_CHUNK = 128
GLA_KEY_DIM = 128
GLA_GATE_RANK = 16
GLA_TAU = 16.0
GLA_CHUNK = 64
HGRN_FDIM = 256
HGRN_CHUNK = 64
HGRN_SUB = 16
XA_HEADS = 4
XA_HEAD_DIM = 256
D_FF = 2816
FFN_CONV = 3
FF_TILE = 256
HALO = 16

SMALL_FF = 0
SMALL_DT = 4
SMALL_GA = 8

VMEM_LIMIT = 56 * 1024 * 1024

HIGHEST = lax.Precision.HIGHEST


def _cparams(sem):
    return pltpu.CompilerParams(dimension_semantics=sem, vmem_limit_bytes=VMEM_LIMIT)


def _dot(a, b, precision=None):
    return jnp.dot(a, b, preferred_element_type=F32, precision=precision)


def _dot_nt(a, b, precision=None):
    return lax.dot_general(a, b, (((1,), (1,)), ((), ())), preferred_element_type=F32, precision=precision)


def _dot_tn(a, b, precision=None):
    return lax.dot_general(a, b, (((0,), (0,)), ((), ())), preferred_element_type=F32, precision=precision)


def _rms(x, w):
    return x * lax.rsqrt(jnp.mean(x * x, axis=-1, keepdims=True) + EPS) * w


def _rms_heads64(x, w):
    lane = lax.broadcasted_iota(jnp.int32, x.shape, 1)
    lo = lane < HEAD_DIM
    sq = x * x
    s_lo = jnp.sum(jnp.where(lo, sq, 0.0), axis=-1, keepdims=True)
    s_hi = jnp.sum(jnp.where(lo, 0.0, sq), axis=-1, keepdims=True)
    ms = jnp.where(lo, s_lo, s_hi) * (1.0 / HEAD_DIM)
    return x * lax.rsqrt(ms + EPS) * w


def _rms_heads64_wide(x, w):
    return jnp.concatenate(
        [_rms_heads64(x[:, i * LANES:(i + 1) * LANES], w[:, i * LANES:(i + 1) * LANES]) for i in range(2)], axis=-1)


def _sigmoid_pair(z):
    e = jnp.exp(-jnp.abs(z))
    big = 1.0 / (1.0 + e)
    small = e * big
    pos = z >= 0
    return jnp.where(pos, big, small), jnp.where(pos, small, big)


def _silu(z):
    return z * _sigmoid_pair(z)[0]


def _log_sigmoid(z):
    return jnp.minimum(z, 0.0) - jnp.log1p(jnp.exp(-jnp.abs(z)))


def _softplus(z):
    return jnp.maximum(z, 0.0) + jnp.log1p(jnp.exp(-jnp.abs(z)))


def _iota2(shape, axis):
    return lax.broadcasted_iota(jnp.int32, shape, axis)


def _lane_col(x, lane_index):
    return x[:, lane_index:lane_index + 1]


IN_GROUPS = (768, 768, 768, 1024, 128)
D_IN_PAD = sum(IN_GROUPS)


def _in_proj_kernel(x_ref, ln_ref, w_ref, fox_ref, ssd_ref, gla_ref, hgrn_ref, small_ref):
    h = _rms(x_ref[...], ln_ref[...]).astype(BF16)
    outs = (fox_ref, ssd_ref, gla_ref, hgrn_ref, small_ref)
    start = 0
    for width, out in zip(IN_GROUPS, outs):
        out[...] = _dot(h, w_ref[:, start:start + width]).astype(out.dtype)
        start += width


def _in_proj(x2d, ln, w_r, tm):
    m = x2d.shape[0]
    out_shape = [jax.ShapeDtypeStruct((m, n), BF16) for n in IN_GROUPS[:4]]
    out_shape.append(jax.ShapeDtypeStruct((m, IN_GROUPS[4]), F32))
    return pl.pallas_call(
        _in_proj_kernel,
        grid=(m // tm,),
        in_specs=[
            pl.BlockSpec((tm, D_MODEL), lambda i: (i, 0)),
            pl.BlockSpec((1, D_MODEL), lambda i: (0, 0)),
            pl.BlockSpec((D_MODEL, D_IN_PAD), lambda i: (0, 0)),
        ],
        out_specs=[pl.BlockSpec((tm, n), lambda i: (i, 0)) for n in IN_GROUPS],
        out_shape=out_shape,
        compiler_params=_cparams(("parallel",)),
        name="in_proj",
    )(x2d, ln, w_r)


FOX_TILE = 256
FOX_CUM_BLOCK = 256
N_SPLIT = 3


def _split3(c):
    hi = c.astype(BF16).astype(F32)
    r = c - hi
    mid = r.astype(BF16).astype(F32)
    lo = (r - mid).astype(BF16).astype(F32)
    return hi, mid, lo


def _fox_kernel(q_ref, k_ref, v_ref, small_ref, bias_ref, qn_ref, kn_ref, on_ref, o_ref,
                ka_ref, qaug_ref, *, seq):
    pair = pl.program_id(1)
    qi = pl.program_id(2)
    tq = FOX_TILE

    @pl.when(qi == 0)
    def _prepare():
        nblk = seq // FOX_CUM_BLOCK
        tri = (_iota2((FOX_CUM_BLOCK, FOX_CUM_BLOCK), 0) >= _iota2((FOX_CUM_BLOCK, FOX_CUM_BLOCK), 1)).astype(F32)
        lane = _iota2((FOX_CUM_BLOCK, LANES), 1)
        carry = jnp.zeros((1, LANES), F32)
        for blk in range(nblk):
            rows = pl.ds(blk * FOX_CUM_BLOCK, FOX_CUM_BLOCK)
            logf = _log_sigmoid(small_ref[rows, :] + bias_ref[...])
            c = _dot(tri, logf, precision=HIGHEST) + carry
            carry = c[FOX_CUM_BLOCK - 1:FOX_CUM_BLOCK, :]
            kn = _rms_heads64(k_ref[rows, :].astype(F32), kn_ref[...])
            for hh in range(2):
                head = 2 * pair + hh
                c_h = jnp.sum(jnp.where(lane == SMALL_FF + head, c, 0.0), axis=-1, keepdims=True)
                hi, mid, lo = _split3(c_h)
                base = HEAD_DIM * (1 - hh)
                k_aug = jnp.where((lane >= base) & (lane < base + N_SPLIT), 1.0, 0.0)
                q_aug = jnp.where((lane >= base + N_SPLIT) & (lane < base + 2 * N_SPLIT), 1.0, 0.0)
                for piece_index, piece in enumerate((hi, mid, lo)):
                    k_aug = jnp.where(lane == base + N_SPLIT + piece_index, -piece, k_aug)
                    q_aug = jnp.where(lane == base + piece_index, piece, q_aug)
                own = (lane >= HEAD_DIM * hh) & (lane < HEAD_DIM * (hh + 1))
                ka_ref[hh, rows, :] = jnp.where(own, kn, k_aug).astype(BF16)
                qaug_ref[hh, rows, :] = jnp.where(own, 0.0, q_aug)

    scale = HEAD_DIM ** -0.5
    qrows = pl.ds(pl.multiple_of(qi * tq, tq), tq)
    qn = _rms_heads64(q_ref[...].astype(F32), qn_ref[...]) * scale
    lane = _iota2((tq, LANES), 1)
    causal = _iota2((tq, tq), 0) >= _iota2((tq, tq), 1)
    halves = []
    for hh in range(2):
        own = (lane >= HEAD_DIM * hh) & (lane < HEAD_DIM * (hh + 1))
        qa = jnp.where(own, qn, qaug_ref[hh, qrows, :]).astype(BF16)

        def block(j, carry, masked, hh=hh, qa=qa):
            m_prev, l_prev, acc = carry
            krows = pl.ds(pl.multiple_of(j * tq, tq), tq)
            s = _dot_nt(qa, ka_ref[hh, krows, :])
            if masked:
                s = jnp.where(causal, s, MASK_VALUE)
            m_new = jnp.maximum(m_prev, jnp.max(s, axis=-1, keepdims=True))
            alpha = jnp.exp(m_prev - m_new)
            p = jnp.exp(s - m_new)
            l_new = alpha * l_prev + jnp.sum(p, axis=-1, keepdims=True)
            acc = alpha * acc + _dot(p.astype(BF16), v_ref[krows, :])
            return m_new, l_new, acc

        init = (jnp.full((tq, 1), MASK_VALUE, F32), jnp.zeros((tq, 1), F32), jnp.zeros((tq, LANES), F32))
        carry = lax.fori_loop(0, qi, functools.partial(block, masked=False), init)
        _, l_fin, acc = block(qi, carry, True)
        halves.append((acc / l_fin, own))
    o = jnp.where(halves[0][1], halves[0][0], halves[1][0])
    o_ref[...] = _rms_heads64(o, on_ref[...]).astype(o_ref.dtype)


def _fox(fox3, small3, bias, qn, kn, on):
    b, seq, _ = fox3.shape
    tq = FOX_TILE
    kern = functools.partial(_fox_kernel, seq=seq)
    vec = pl.BlockSpec((1, LANES), lambda bi, p, qi: (0, 0))
    return pl.pallas_call(
        kern,
        grid=(b, 2, seq // tq),
        in_specs=[
            pl.BlockSpec((None, tq, LANES), lambda bi, p, qi: (bi, qi, p)),
            pl.BlockSpec((None, seq, LANES), lambda bi, p, qi: (bi, 0, 2 + p)),
            pl.BlockSpec((None, seq, LANES), lambda bi, p, qi: (bi, 0, 4 + p)),
            pl.BlockSpec((None, seq, LANES), lambda bi, p, qi: (bi, 0, 0)),
            vec, vec, vec, vec,
        ],
        out_specs=pl.BlockSpec((None, tq, LANES), lambda bi, p, qi: (bi, qi, p)),
        out_shape=jax.ShapeDtypeStruct((b, seq, GROUP_WIDTH), BF16),
        scratch_shapes=[pltpu.VMEM((2, seq, LANES), BF16), pltpu.VMEM((2, seq, LANES), F32)],
        compiler_params=_cparams(("parallel", "parallel", "arbitrary")),
        name="fox",
    )(fox3, fox3, fox3, small3, bias, qn, kn, on)


def _ssd_kernel(ssd_ref, small_ref, cw_ref, cb_ref, dtb_ref, alog_ref, dskip_ref, gn_ref, o_ref,
                xc_ref, dt_ref, st_ref, *, seq):
    c_len = SSD_CHUNK
    conv_dim = GROUP_WIDTH + 2 * LANES
    blk = 256
    for r in range(seq // blk):
        if r == 0:
            ext = jnp.concatenate([jnp.zeros((HALO, conv_dim), F32),
                                   ssd_ref[0:blk, GROUP_WIDTH:].astype(F32)], axis=0)
        else:
            ext = ssd_ref[r * blk - HALO:(r + 1) * blk, GROUP_WIDTH:].astype(F32)
        acc = cb_ref[...] + cw_ref[SSD_CONV - 1:SSD_CONV, :] * ext[HALO:, :]
        for shift in range(1, SSD_CONV):
            acc = acc + cw_ref[SSD_CONV - 1 - shift:SSD_CONV - shift, :] * pltpu.roll(ext, shift, 0)[HALO:, :]
        xc_ref[r * blk:(r + 1) * blk, :] = _silu(acc)
    dt_ref[...] = _softplus(small_ref[...] + dtb_ref[...])
    st_ref[...] = jnp.zeros_like(st_ref)

    a_lane = -jnp.exp(alog_ref[...])
    tri = (_iota2((c_len, c_len), 0) >= _iota2((c_len, c_len), 1))
    tri_f = tri.astype(F32)
    sel = (_iota2((8, LANES), 0) + SMALL_DT == _iota2((8, LANES), 1)).astype(F32)
    lane = _iota2((c_len, LANES), 1)
    lo = lane < HEAD_DIM

    def chunk(n, _):
        rows = pl.ds(pl.multiple_of(n * c_len, c_len), c_len)
        xs = xc_ref[rows, 0:GROUP_WIDTH]
        bm = xc_ref[rows, GROUP_WIDTH:GROUP_WIDTH + LANES]
        cm = xc_ref[rows, GROUP_WIDTH + LANES:]
        dt = dt_ref[rows, :]
        a_cs = _dot(tri_f, dt * a_lane, precision=HIGHEST)
        a_row = _dot_nt(sel, a_cs, precision=HIGHEST)
        a_last = a_cs[c_len - 1:c_len, :]
        to_end = jnp.exp(a_last - a_cs)
        from_start = jnp.exp(a_cs)
        chunk_decay = jnp.exp(a_last)
        cm16 = cm.astype(BF16)
        y_halves = []
        for g in range(2):
            in_group = (lane >= SSD_STATE * g) & (lane < SSD_STATE * (g + 1))
            b_g = jnp.where(in_group, bm, 0.0).astype(BF16)
            scores = _dot_nt(cm16, b_g)
            heads = (2 * g, 2 * g + 1)

            def per_head(x):
                return jnp.where(lo[:x.shape[0]], _lane_col(x, SMALL_DT + heads[0]),
                                 _lane_col(x, SMALL_DT + heads[1]))

            xdt = xs[:, g * LANES:(g + 1) * LANES] * per_head(dt)
            xdt16 = xdt.astype(BF16)
            y_intra = []
            for h in heads:
                seg = _lane_col(a_cs, SMALL_DT + h) - a_row[h:h + 1, :]
                decay = jnp.where(tri, jnp.exp(jnp.where(tri, seg, 0.0)), 0.0)
                y_intra.append(_dot((scores * decay).astype(BF16), xdt16))
            y = jnp.where(lo, y_intra[0], y_intra[1])
            state = st_ref[g]
            y = y + _dot(cm16, state.astype(BF16)) * per_head(from_start)
            d_state = _dot_tn(b_g, (xdt * per_head(to_end)).astype(BF16))
            st_ref[g] = state * per_head(chunk_decay) + d_state
            y_halves.append(y)
        y = jnp.concatenate(y_halves, axis=-1) + xs * dskip_ref[...]
        y = y * _silu(ssd_ref[rows, 0:GROUP_WIDTH].astype(F32))
        out = jnp.concatenate([_rms(y[:, i * LANES:(i + 1) * LANES], gn_ref[:, i * LANES:(i + 1) * LANES])
                               for i in range(2)], axis=-1)
        o_ref[rows, :] = out.astype(o_ref.dtype)
        return 0

    lax.fori_loop(0, seq // c_len, chunk, 0)


def _ssd(ssd3, small3, cw, cb, dtb, alog, dskip, gn):
    b, seq, width = ssd3.shape
    conv_dim = GROUP_WIDTH + 2 * LANES
    kern = functools.partial(_ssd_kernel, seq=seq)

    def vec(n, rows=1):
        return pl.BlockSpec((rows, n), lambda bi: (0, 0))

    return pl.pallas_call(
        kern,
        grid=(b,),
        in_specs=[
            pl.BlockSpec((None, seq, width), lambda bi: (bi, 0, 0)),
            pl.BlockSpec((None, seq, LANES), lambda bi: (bi, 0, 0)),
            vec(conv_dim, SSD_CONV), vec(conv_dim), vec(LANES), vec(LANES), vec(GROUP_WIDTH), vec(GROUP_WIDTH),
        ],
        out_specs=pl.BlockSpec((None, seq, GROUP_WIDTH), lambda bi: (bi, 0, 0)),
        out_shape=jax.ShapeDtypeStruct((b, seq, GROUP_WIDTH), BF16),
        scratch_shapes=[pltpu.VMEM((seq, conv_dim), F32), pltpu.VMEM((seq, LANES), F32),
                        pltpu.VMEM((2, LANES, LANES), F32)],
        compiler_params=_cparams(("parallel",)),
        name="ssd",
    )(ssd3, small3, cw, cb, dtb, alog, dskip, gn)


def _gla_levels(c_len, sub):
    sizes = []
    size = sub
    while size <= c_len:
        sizes.append(size)
        size *= 2
    return sizes


def _gla_chunk(q, k, v, logf, state_t, *, dk, sub):
    c_len = q.shape[0]
    dv = v.shape[1]
    dkh = dk // N_HEADS
    dvh = dv // N_HEADS
    sizes = _gla_levels(c_len, sub)
    row = _iota2((c_len, c_len), 0)
    col = _iota2((c_len, c_len), 1)
    masks = []
    for size in sizes:
        same = (row // size) == (col // size)
        masks.append((same & (col <= row)).astype(F32))
        masks.append(same.astype(F32))
    sums = _dot(jnp.concatenate(masks, axis=0), logf, precision=HIGHEST)
    cs = [sums[(2 * i) * c_len:(2 * i + 1) * c_len] for i in range(len(sizes))]
    tot = [sums[(2 * i + 1) * c_len:(2 * i + 2) * c_len] for i in range(len(sizes))]

    klane = _iota2((c_len, dk), 1)
    vlane = _iota2((c_len, dv), 1)

    def stack_heads(x, lane, width):
        return jnp.concatenate(
            [jnp.where((lane >= h * width) & (lane < (h + 1) * width), x, 0.0) for h in range(N_HEADS)],
            axis=0).astype(BF16)

    arow = _iota2((c_len, N_HEADS * c_len), 0)
    acol = _iota2((c_len, N_HEADS * c_len), 1) % c_len
    q0 = (q * jnp.exp(cs[0])).astype(BF16)
    k0 = k * jnp.exp(-cs[0])
    keep = ((arow // sub) == (acol // sub)) & (acol <= arow)
    att = jnp.where(keep, _dot_nt(q0, stack_heads(k0, klane, dkh)), 0.0)
    for level in range(1, len(sizes)):
        size, half = sizes[level], sizes[level - 1]
        q_l = q0 if level == 1 else (q * jnp.exp(cs[level - 1])).astype(BF16)
        k_l = k * jnp.exp(tot[level - 1] - cs[level - 1])
        keep = ((arow // size) == (acol // size)) & ((arow % size) >= half) & ((acol % size) < half)
        att = jnp.where(keep, _dot_nt(q_l, stack_heads(k_l, klane, dkh)), att)
    o = _dot(att.astype(BF16), stack_heads(v, vlane, dvh))

    q_s = (q * jnp.exp(cs[-1])).astype(BF16)
    k_s = (k * jnp.exp(tot[-1] - cs[-1])).astype(BF16)
    o = o + _dot_nt(q_s, state_t.astype(BF16))
    srow = _iota2((dv, dk), 0) // dvh
    scol = _iota2((dv, dk), 1) // dkh
    d_state = jnp.where(srow == scol, _dot_tn(v.astype(BF16), k_s), 0.0)
    new_state = state_t * jnp.exp(tot[-1][0:1, :]) + d_state
    return o, new_state


def _gla_kernel(gla_ref, small_ref, wg_ref, bg_ref, on_ref, o_ref, st_ref, *, seq):
    c_len = GLA_CHUNK
    dk = GLA_KEY_DIM
    st_ref[...] = jnp.zeros_like(st_ref)
    scale = (dk // N_HEADS) ** -0.5

    def chunk(n, _):
        rows = pl.ds(pl.multiple_of(n * c_len, c_len), c_len)
        q = gla_ref[rows, 0:dk].astype(F32) * scale
        k = gla_ref[rows, dk:2 * dk].astype(F32)
        v = gla_ref[rows, 2 * dk:2 * dk + GROUP_WIDTH].astype(F32)
        r = gla_ref[rows, 2 * dk + GROUP_WIDTH:].astype(F32)
        gate = _dot(small_ref[rows, :].astype(BF16), wg_ref[...]) + bg_ref[...]
        logf = _log_sigmoid(gate) * (1.0 / GLA_TAU)
        o, new_state = _gla_chunk(q, k, v, logf, st_ref[...], dk=dk, sub=c_len)
        st_ref[...] = new_state
        o = _rms_heads64_wide(o, on_ref[...]) * _silu(r)
        o_ref[rows, :] = o.astype(o_ref.dtype)
        return 0

    lax.fori_loop(0, seq // c_len, chunk, 0)


def _gla(gla3, small3, wg, bg, on):
    b, seq, width = gla3.shape
    kern = functools.partial(_gla_kernel, seq=seq)
    return pl.pallas_call(
        kern,
        grid=(b,),
        in_specs=[
            pl.BlockSpec((None, seq, width), lambda bi: (bi, 0, 0)),
            pl.BlockSpec((None, seq, LANES), lambda bi: (bi, 0, 0)),
            pl.BlockSpec((LANES, GLA_KEY_DIM), lambda bi: (0, 0)),
            pl.BlockSpec((1, GLA_KEY_DIM), lambda bi: (0, 0)),
            pl.BlockSpec((1, GROUP_WIDTH), lambda bi: (0, 0)),
        ],
        out_specs=pl.BlockSpec((None, seq, GROUP_WIDTH), lambda bi: (bi, 0, 0)),
        out_shape=jax.ShapeDtypeStruct((b, seq, GROUP_WIDTH), BF16),
        scratch_shapes=[pltpu.VMEM((GROUP_WIDTH, GLA_KEY_DIM), F32)],
        compiler_params=_cparams(("parallel",)),
        name="gla",
    )(gla3, small3, wg, bg, on)


def _hgrn_kernel(hg_ref, lbl_ref, on_ref, o_ref, st_ref, *, seq, layer):
    c_len = HGRN_CHUNK
    dk = HGRN_FDIM
    st_ref[...] = jnp.zeros_like(st_ref)
    logits = lbl_ref[...]
    e = jnp.exp(logits - jnp.max(logits, axis=0, keepdims=True))
    soft = e / jnp.sum(e, axis=0, keepdims=True)
    lb = jnp.zeros((1, dk), F32)
    for i in range(1, layer + 1):
        lb = lb + soft[i:i + 1, :]
    lb = jnp.clip(lb, 0.0, 1.0 - 1e-6)

    def chunk(n, _):
        rows = pl.ds(pl.multiple_of(n * c_len, c_len), c_len)
        q = hg_ref[rows, 0:dk].astype(F32)
        z = hg_ref[rows, dk:2 * dk].astype(F32)
        v = hg_ref[rows, 2 * dk:2 * dk + GROUP_WIDTH].astype(F32)
        g = hg_ref[rows, 2 * dk + GROUP_WIDTH:].astype(F32)
        sig, sig_neg = _sigmoid_pair(z)
        logf = jnp.log(lb + (1.0 - lb) * sig)
        k = (1.0 - lb) * sig_neg
        o, new_state = _gla_chunk(q, k, v, logf, st_ref[...], dk=dk, sub=HGRN_SUB)
        st_ref[...] = new_state
        o = _rms_heads64_wide(o, on_ref[...]) * _silu(g)
        o_ref[rows, :] = o.astype(o_ref.dtype)
        return 0

    lax.fori_loop(0, seq // c_len, chunk, 0)


def _hgrn(hg3, lb_logits, on, layer):
    b, seq, width = hg3.shape
    depth = lb_logits.shape[0]
    kern = functools.partial(_hgrn_kernel, seq=seq, layer=layer)
    return pl.pallas_call(
        kern,
        grid=(b,),
        in_specs=[
            pl.BlockSpec((None, seq, width), lambda bi: (bi, 0, 0)),
            pl.BlockSpec((depth, HGRN_FDIM), lambda bi: (0, 0)),
            pl.BlockSpec((1, GROUP_WIDTH), lambda bi: (0, 0)),
        ],
        out_specs=pl.BlockSpec((None, seq, GROUP_WIDTH), lambda bi: (bi, 0, 0)),
        out_shape=jax.ShapeDtypeStruct((b, seq, GROUP_WIDTH), BF16),
        scratch_shapes=[pltpu.VMEM((GROUP_WIDTH, HGRN_FDIM), F32)],
        compiler_params=_cparams(("parallel",)),
        name="hgrn2",
    )(hg3, lb_logits, on)


def _kv_kernel(mem_ref, mn_ref, w_ref, kn_ref, k_ref, v_ref):
    mem_n = _rms(mem_ref[...], mn_ref[...]).astype(BF16)
    kv = _dot(mem_n, w_ref[...])
    for h in range(XA_HEADS):
        cols = slice(h * XA_HEAD_DIM, (h + 1) * XA_HEAD_DIM)
        k_ref[:, cols] = _rms(kv[:, cols], kn_ref[...]).astype(k_ref.dtype)
    v_ref[...] = kv[:, D_MODEL:].astype(v_ref.dtype)


def _kv_proj(mem2d, mem_norm, w_kv, kn, tm):
    m = mem2d.shape[0]
    depth = w_kv.shape[0]
    out = jax.ShapeDtypeStruct((depth, m, D_MODEL), BF16)
    return pl.pallas_call(
        _kv_kernel,
        grid=(depth, m // tm),
        in_specs=[
            pl.BlockSpec((tm, D_MODEL), lambda l, i: (i, 0)),
            pl.BlockSpec((1, D_MODEL), lambda l, i: (0, 0)),
            pl.BlockSpec((None, D_MODEL, 2 * D_MODEL), lambda l, i: (l, 0, 0)),
            pl.BlockSpec((None, 1, XA_HEAD_DIM), lambda l, i: (l, 0, 0)),
        ],
        out_specs=[pl.BlockSpec((None, tm, D_MODEL), lambda l, i: (l, i, 0))] * 2,
        out_shape=[out, out],
        compiler_params=_cparams(("parallel", "parallel")),
        name="mem_kv",
    )(mem2d, mem_norm, w_kv, kn)


def _mix_xattn_kernel(x_ref, yf_ref, ys_ref, yg_ref, yh_ref, wout_ref, ln_ref, wq_ref, qn_ref,
                      k_ref, v_ref, wo_ref, o_ref):
    x = x_ref[...]
    for i, y_ref in enumerate((yf_ref, ys_ref, yg_ref, yh_ref)):
        x = x + _dot(y_ref[...], wout_ref[i * GROUP_WIDTH:(i + 1) * GROUP_WIDTH, :])
    h = _rms(x, ln_ref[...]).astype(BF16)
    q = _dot(h, wq_ref[...])
    scale = XA_HEAD_DIM ** -0.5
    outs = []
    for hd in range(XA_HEADS):
        cols = slice(hd * XA_HEAD_DIM, (hd + 1) * XA_HEAD_DIM)
        qh = (_rms(q[:, cols], qn_ref[...]) * scale).astype(BF16)
        logits = _dot_nt(qh, k_ref[:, cols])
        p = jnp.exp(logits - jnp.max(logits, axis=-1, keepdims=True))
        p = p / jnp.sum(p, axis=-1, keepdims=True)
        outs.append(_dot(p.astype(BF16), v_ref[:, cols]).astype(BF16))
    o = jnp.concatenate(outs, axis=-1)
    o_ref[...] = x + _dot(o, wo_ref[...])


def _mix_xattn(x2d, ys, w_out, ln, wq, qn, k3, v3, wo, tm, seq):
    m = x2d.shape[0]
    n_mem = k3.shape[1]
    tiles_per_seq = seq // tm
    row = lambda i: (i, 0)
    const = lambda i: (0, 0)
    mem_spec = pl.BlockSpec((None, n_mem, D_MODEL), lambda i: (i // tiles_per_seq, 0, 0))
    return pl.pallas_call(
        _mix_xattn_kernel,
        grid=(m // tm,),
        in_specs=[
            pl.BlockSpec((tm, D_MODEL), row),
            *[pl.BlockSpec((tm, GROUP_WIDTH), row)] * 4,
            pl.BlockSpec((D_MODEL, D_MODEL), const),
            pl.BlockSpec((1, D_MODEL), const),
            pl.BlockSpec((D_MODEL, D_MODEL), const),
            pl.BlockSpec((1, XA_HEAD_DIM), const),
            mem_spec, mem_spec,
            pl.BlockSpec((D_MODEL, D_MODEL), const),
        ],
        out_specs=pl.BlockSpec((tm, D_MODEL), row),
        out_shape=jax.ShapeDtypeStruct((m, D_MODEL), F32),
        compiler_params=_cparams(("parallel",)),
        name="mix_xattn",
    )(x2d, *ys, w_out, ln, wq, qn, k3, v3, wo)


def _ffn_kernel(x_ref, xp_ref, ln_ref, wup_ref, cw_ref, cb_ref, wdown_ref, o_ref, *, tiles_per_seq):
    i = pl.program_id(0)
    tm = x_ref.shape[0]
    x = x_ref[...]
    first = (i % tiles_per_seq) == 0
    h = _rms(x, ln_ref[...]).astype(BF16)
    hp = jnp.where(first, 0.0, _rms(xp_ref[...], ln_ref[...])).astype(BF16)
    h_ext = jnp.concatenate([hp, h], axis=0)
    acc = x
    for c in range(D_FF // FF_TILE):
        cols = slice(c * FF_TILE, (c + 1) * FF_TILE)
        vcols = slice(D_FF + c * FF_TILE, D_FF + (c + 1) * FF_TILE)
        gate = _dot(h_ext, wup_ref[:, cols])
        conv = cb_ref[:, cols] + cw_ref[FFN_CONV - 1:FFN_CONV, cols] * gate[HALO:, :]
        for shift in range(1, FFN_CONV):
            conv = conv + (cw_ref[FFN_CONV - 1 - shift:FFN_CONV - shift, cols]
                           * pltpu.roll(gate, shift, 0)[HALO:, :])
        val = _dot(h, wup_ref[:, vcols])
        act = (_silu(conv) * val).astype(BF16)
        acc = acc + _dot(act, wdown_ref[cols, :])
    o_ref[...] = acc


def _ffn(x2d, ln, w_up, cw, cb, w_down, tm, seq):
    m = x2d.shape[0]
    kern = functools.partial(_ffn_kernel, tiles_per_seq=seq // tm)
    const = lambda i: (0, 0)
    halo_blocks = tm // HALO
    return pl.pallas_call(
        kern,
        grid=(m // tm,),
        in_specs=[
            pl.BlockSpec((tm, D_MODEL), lambda i: (i, 0)),
            pl.BlockSpec((HALO, D_MODEL), lambda i: (jnp.maximum(i * halo_blocks - 1, 0), 0)),
            pl.BlockSpec((1, D_MODEL), const),
            pl.BlockSpec((D_MODEL, 2 * D_FF), const),
            pl.BlockSpec((FFN_CONV, D_FF), const),
            pl.BlockSpec((1, D_FF), const),
            pl.BlockSpec((D_FF, D_MODEL), const),
        ],
        out_specs=pl.BlockSpec((tm, D_MODEL), lambda i: (i, 0)),
        out_shape=jax.ShapeDtypeStruct((m, D_MODEL), F32),
        compiler_params=_cparams(("parallel",)),
        name="conv_glu_ffn",
    )(x2d, x2d, ln, w_up, cw, cb, w_down)


def _pad_lanes(vec, offset, total=LANES):
    vec = vec.astype(F32).reshape(1, -1)
    return jnp.pad(vec, ((0, 0), (offset, total - offset - vec.shape[1])))


def _arrange_w_in(w):
    fq, fk, fv, ff, sz, sxbc, sdt, gq, gk, gv, ga, gr, hq, hf, hi, hg = jnp.split(
        w, [256, 512, 768, 772, 1028, 1540, 1544, 1672, 1800, 2056, 2072, 2328, 2584, 2840, 3096], axis=1)
    small = jnp.concatenate([ff, sdt, ga], axis=1)
    small = jnp.pad(small, ((0, 0), (0, LANES - small.shape[1])))
    return jnp.concatenate([fq, fk, fv, sz, sxbc, gq, gk, gv, gr, hq, hf, hi, hg, small], axis=1).astype(BF16)


def kernel(x, mem, ln_mix, w_in, w_out, fox_f_bias, fox_qn, fox_kn, fox_on, ssd_conv_w, ssd_conv_b, ssd_dt_bias,
           ssd_a_log, ssd_d, ssd_norm, gla_w_g2, gla_b_g2, gla_norm, hgrn_lb_logits, hgrn_norm, ln_xattn, mem_norm,
           xa_wq, xa_wkv, xa_wo, xa_qn, xa_kn, ln_ffn, ffn_w_up, ffn_conv_w, ffn_conv_b, ffn_w_down):
    b, seq, d = x.shape
    depth = w_in.shape[0]
    n_mem = mem.shape[1]
    m = b * seq
    tm = min(512, seq)
    x2d = x.reshape(m, d)

    k_all, v_all = _kv_proj(mem.reshape(b * n_mem, d), mem_norm.reshape(1, d), xa_wkv.astype(BF16),
                            xa_kn.reshape(depth, 1, XA_HEAD_DIM), min(512, b * n_mem))
    for l in range(depth):
        fox, ssd, gla, hgrn, small = _in_proj(x2d, ln_mix[l].reshape(1, d), _arrange_w_in(w_in[l]), tm)
        small3 = small.reshape(b, seq, LANES)
        tile2 = lambda v: jnp.tile(v.astype(F32).reshape(1, -1), (1, LANES // HEAD_DIM))
        tile4 = lambda v: jnp.tile(v.astype(F32).reshape(1, -1), (1, GROUP_WIDTH // HEAD_DIM))
        y_fox = _fox(fox.reshape(b, seq, -1), small3, _pad_lanes(fox_f_bias[l], SMALL_FF),
                     tile2(fox_qn[l]), tile2(fox_kn[l]), tile2(fox_on[l]))
        y_ssd = _ssd(ssd.reshape(b, seq, -1), small3, ssd_conv_w[l], ssd_conv_b[l].reshape(1, -1),
                     _pad_lanes(ssd_dt_bias[l], SMALL_DT), _pad_lanes(ssd_a_log[l], SMALL_DT),
                     jnp.repeat(ssd_d[l].astype(F32), HEAD_DIM).reshape(1, -1), ssd_norm[l].reshape(1, -1))
        wg = jnp.pad(gla_w_g2[l], ((SMALL_GA, LANES - SMALL_GA - GLA_GATE_RANK), (0, 0))).astype(BF16)
        y_gla = _gla(gla.reshape(b, seq, -1), small3, wg, gla_b_g2[l].reshape(1, -1), tile4(gla_norm[l]))
        y_hgrn = _hgrn(hgrn.reshape(b, seq, -1), hgrn_lb_logits, tile4(hgrn_norm[l]), l)
        ys = [y.reshape(m, GROUP_WIDTH) for y in (y_fox, y_ssd, y_gla, y_hgrn)]
        x2d = _mix_xattn(x2d, ys, w_out[l].astype(BF16), ln_xattn[l].reshape(1, d), xa_wq[l].astype(BF16),
                         xa_qn[l].reshape(1, -1), k_all[l].reshape(b, n_mem, d), v_all[l].reshape(b, n_mem, d),
                         xa_wo[l].astype(BF16), tm, seq)
        x2d = _ffn(x2d, ln_ffn[l].reshape(1, d), ffn_w_up[l].astype(BF16), ffn_conv_w[l],
                   ffn_conv_b[l].reshape(1, -1), ffn_w_down[l].astype(BF16), tm, seq)
    return x2d.reshape(b, seq, d)
```

```python
import functools

import jax
import jax.numpy as jnp
from jax import lax
from jax.experimental import pallas as pl
from jax.experimental.pallas import tpu as pltpu

F32 = jnp.float32
BF16 = jnp.bfloat16

EPS = 1e-6
MASK_VALUE = -1e30
D_MODEL = 1024
GROUP_WIDTH = 256
HEAD_DIM = 64
LANES = 128
N_HEADS = 4
SSD_STATE = 64
SSD_CONV = 4
SSD_CHUNK = 128
GLA_KEY_DIM = 128
GLA_GATE_RANK = 16
GLA_TAU = 16.0
GLA_CHUNK = 64
HGRN_FDIM = 256
HGRN_CHUNK = 64
HGRN_SUB = 16
XA_HEADS = 4
XA_HEAD_DIM = 256
D_FF = 2816
FFN_CONV = 3
FF_TILE = 256
HALO = 16
MIX_BATCH = 2

SMALL_FF = 0
SMALL_DT = 4
SMALL_GA = 8

VMEM_LIMIT = 56 * 1024 * 1024

HIGHEST = lax.Precision.HIGHEST


def _cparams(sem):
    return pltpu.CompilerParams(dimension_semantics=sem, vmem_limit_bytes=VMEM_LIMIT)


def _dot(a, b, precision=None):
    return jnp.dot(a, b, preferred_element_type=F32, precision=precision)


def _dot_nt(a, b, precision=None):
    return lax.dot_general(a, b, (((1,), (1,)), ((), ())), preferred_element_type=F32, precision=precision)


def _dot_tn(a, b, precision=None):
    return lax.dot_general(a, b, (((0,), (0,)), ((), ())), preferred_element_type=F32, precision=precision)


def _rms(x, w):
    return x * lax.rsqrt(jnp.mean(x * x, axis=-1, keepdims=True) + EPS) * w


def _rms_heads64(x, w):
    lane = lax.broadcasted_iota(jnp.int32, x.shape, 1)
    lo = lane < HEAD_DIM
    sq = x * x
    s_lo = jnp.sum(jnp.where(lo, sq, 0.0), axis=-1, keepdims=True)
    s_hi = jnp.sum(jnp.where(lo, 0.0, sq), axis=-1, keepdims=True)
    ms = jnp.where(lo, s_lo, s_hi) * (1.0 / HEAD_DIM)
    return x * lax.rsqrt(ms + EPS) * w


def _rms_heads64_wide(x, w):
    return jnp.concatenate(
        [_rms_heads64(x[:, i * LANES:(i + 1) * LANES], w[:, i * LANES:(i + 1) * LANES]) for i in range(2)], axis=-1)


def _sigmoid_pair(z):
    e = jnp.exp(-jnp.abs(z))
    big = 1.0 / (1.0 + e)
    small = e * big
    pos = z >= 0
    return jnp.where(pos, big, small), jnp.where(pos, small, big)


def _silu(z):
    return z * _sigmoid_pair(z)[0]


def _log_sigmoid(z):
    return jnp.minimum(z, 0.0) - jnp.log1p(jnp.exp(-jnp.abs(z)))


def _softplus(z):
    return jnp.maximum(z, 0.0) + jnp.log1p(jnp.exp(-jnp.abs(z)))


def _iota2(shape, axis):
    return lax.broadcasted_iota(jnp.int32, shape, axis)


def _lane_col(x, lane_index):
    return x[:, lane_index:lane_index + 1]


def _masked_sums(mask16, x, pieces, transpose_rhs=False):
    mm = _dot_nt if transpose_rhs else _dot
    total = None
    rest = x
    for index in range(pieces):
        part = rest.astype(BF16)
        term = mm(mask16, part)
        total = term if total is None else total + term
        if index + 1 < pieces:
            rest = rest - part.astype(F32)
    return total


IN_GROUPS = (768, 768, 768, 1024, 128)
D_IN_PAD = sum(IN_GROUPS)


def _in_proj_kernel(x_ref, ln_ref, w_ref, fox_ref, ssd_ref, gla_ref, hgrn_ref, small_ref):
    h = _rms(x_ref[...], ln_ref[...]).astype(BF16)
    outs = (fox_ref, ssd_ref, gla_ref, hgrn_ref, small_ref)
    start = 0
    for width, out in zip(IN_GROUPS, outs):
        out[...] = _dot(h, w_ref[:, start:start + width]).astype(out.dtype)
        start += width


def _in_proj(x2d, ln, w_r, tm):
    m = x2d.shape[0]
    out_shape = [jax.ShapeDtypeStruct((m, n), BF16) for n in IN_GROUPS[:4]]
    out_shape.append(jax.ShapeDtypeStruct((m, IN_GROUPS[4]), F32))
    return pl.pallas_call(
        _in_proj_kernel,
        grid=(m // tm,),
        in_specs=[
            pl.BlockSpec((tm, D_MODEL), lambda i: (i, 0)),
            pl.BlockSpec((1, D_MODEL), lambda i: (0, 0)),
            pl.BlockSpec((D_MODEL, D_IN_PAD), lambda i: (0, 0)),
        ],
        out_specs=[pl.BlockSpec((tm, n), lambda i: (i, 0)) for n in IN_GROUPS],
        out_shape=out_shape,
        compiler_params=_cparams(("parallel",)),
        name="in_proj",
    )(x2d, ln, w_r)


FOX_TILE = 256
FOX_CUM_BLOCK = 256
N_SPLIT = 3


def _split3(c):
    hi = c.astype(BF16).astype(F32)
    r = c - hi
    mid = r.astype(BF16).astype(F32)
    lo = (r - mid).astype(BF16).astype(F32)
    return hi, mid, lo


def _fox_kernel(q_ref, k_ref, v_ref, small_ref, bias_ref, qn_ref, kn_ref, on_ref, o_ref,
                ka_ref, qaug_ref, *, seq):
    qi = pl.program_id(1)
    tq = FOX_TILE
    halves = GROUP_WIDTH // LANES

    def own_lanes(lane, hh):
        return (lane >= HEAD_DIM * hh) & (lane < HEAD_DIM * (hh + 1))

    @pl.when(qi == 0)
    def _prepare():
        nblk = seq // FOX_CUM_BLOCK
        tri = (_iota2((FOX_CUM_BLOCK, FOX_CUM_BLOCK), 0) >= _iota2((FOX_CUM_BLOCK, FOX_CUM_BLOCK), 1)).astype(F32)
        lane = _iota2((FOX_CUM_BLOCK, LANES), 1)
        carry = jnp.zeros((1, LANES), F32)
        for blk in range(nblk):
            rows = pl.ds(blk * FOX_CUM_BLOCK, FOX_CUM_BLOCK)
            logf = _log_sigmoid(small_ref[rows, :] + bias_ref[...])
            c = _dot(tri, logf, precision=HIGHEST) + carry
            carry = c[FOX_CUM_BLOCK - 1:FOX_CUM_BLOCK, :]
            for half in range(halves):
                lanes = slice(half * LANES, (half + 1) * LANES)
                kn = _rms_heads64(k_ref[rows, lanes].astype(F32), kn_ref[...])
                for hh in range(2):
                    head = 2 * half + hh
                    hi, mid, lo = _split3(_lane_col(c, SMALL_FF + head))
                    base = HEAD_DIM * (1 - hh)
                    k_aug = jnp.where((lane >= base) & (lane < base + N_SPLIT), 1.0, 0.0)
                    q_aug = jnp.where((lane >= base + N_SPLIT) & (lane < base + 2 * N_SPLIT), 1.0, 0.0)
                    for piece_index, piece in enumerate((hi, mid, lo)):
                        k_aug = jnp.where(lane == base + N_SPLIT + piece_index, -piece, k_aug)
                        q_aug = jnp.where(lane == base + piece_index, piece, q_aug)
                    own = own_lanes(lane, hh)
                    ka_ref[head, rows, :] = jnp.where(own, kn, k_aug).astype(BF16)
                    qaug_ref[head, rows, :] = jnp.where(own, 0.0, q_aug)

    scale = HEAD_DIM ** -0.5
    qrows = pl.ds(pl.multiple_of(qi * tq, tq), tq)
    lane = _iota2((tq, LANES), 1)
    lo_half = lane < HEAD_DIM
    causal = _iota2((tq, tq), 0) >= _iota2((tq, tq), 1)
    qas = []
    for half in range(halves):
        lanes = slice(half * LANES, (half + 1) * LANES)
        qn = _rms_heads64(q_ref[:, lanes].astype(F32), qn_ref[...]) * scale
        for hh in range(2):
            qas.append(jnp.where(own_lanes(lane, hh), qn, qaug_ref[2 * half + hh, qrows, :]).astype(BF16))

    def block(j, carry, masked):
        krows = pl.ds(pl.multiple_of(j * tq, tq), tq)
        new_carry = []
        for half in range(halves):
            v = v_ref[krows, half * LANES:(half + 1) * LANES]
            m_prev, l_prev, acc = carry[half]
            m_new, l_new, alpha, pv = [], [], [], []
            for hh in range(2):
                head = 2 * half + hh
                s = _dot_nt(qas[head], ka_ref[head, krows, :])
                if masked:
                    s = jnp.where(causal, s, MASK_VALUE)
                m_h = jnp.maximum(m_prev[hh], jnp.max(s, axis=-1, keepdims=True))
                a_h = jnp.exp(m_prev[hh] - m_h)
                p = jnp.exp(s - m_h)
                m_new.append(m_h)
                alpha.append(a_h)
                l_new.append(a_h * l_prev[hh] + jnp.sum(p, axis=-1, keepdims=True))
                pv.append(_dot(p.astype(BF16), v))
            acc = jnp.where(lo_half, alpha[0], alpha[1]) * acc + jnp.where(lo_half, pv[0], pv[1])
            new_carry.append((tuple(m_new), tuple(l_new), acc))
        return tuple(new_carry)

    col = lambda value: jnp.full((tq, 1), value, F32)
    init = tuple(((col(MASK_VALUE), col(MASK_VALUE)), (col(0.0), col(0.0)), jnp.zeros((tq, LANES), F32))
                 for _ in range(halves))
    carry = lax.fori_loop(0, qi, functools.partial(block, masked=False), init)
    carry = block(qi, carry, True)
    for half in range(halves):
        _, l_fin, acc = carry[half]
        o = acc / jnp.where(lo_half, l_fin[0], l_fin[1])
        o_ref[:, half * LANES:(half + 1) * LANES] = _rms_heads64(o, on_ref[...]).astype(o_ref.dtype)


def _fox(fox3, small3, bias, qn, kn, on):
    b, seq, _ = fox3.shape
    tq = FOX_TILE
    kern = functools.partial(_fox_kernel, seq=seq)
    vec = pl.BlockSpec((1, LANES), lambda bi, qi: (0, 0))
    return pl.pallas_call(
        kern,
        grid=(b, seq // tq),
        in_specs=[
            pl.BlockSpec((None, tq, GROUP_WIDTH), lambda bi, qi: (bi, qi, 0)),
            pl.BlockSpec((None, seq, GROUP_WIDTH), lambda bi, qi: (bi, 0, 1)),
            pl.BlockSpec((None, seq, GROUP_WIDTH), lambda bi, qi: (bi, 0, 2)),
            pl.BlockSpec((None, seq, LANES), lambda bi, qi: (bi, 0, 0)),
            vec, vec, vec, vec,
        ],
        out_specs=pl.BlockSpec((None, tq, GROUP_WIDTH), lambda bi, qi: (bi, qi, 0)),
        out_shape=jax.ShapeDtypeStruct((b, seq, GROUP_WIDTH), BF16),
        scratch_shapes=[pltpu.VMEM((N_HEADS, seq, LANES), BF16), pltpu.VMEM((N_HEADS, seq, LANES), F32)],
        compiler_params=_cparams(("parallel", "arbitrary")),
        name="fox",
    )(fox3, fox3, fox3, small3, bias, qn, kn, on)


def _ssd_kernel(ssd_ref, small_ref, cw_ref, cb_ref, dtb_ref, alog_ref, dskip_ref, gn_ref, o_ref,
                xc_ref, dt_ref, st_ref, *, seq):
    c_len = SSD_CHUNK
    conv_dim = GROUP_WIDTH + 2 * LANES
    blk = 256
    for gb in range(MIX_BATCH):
        for r in range(seq // blk):
            if r == 0:
                ext = jnp.concatenate([jnp.zeros((HALO, conv_dim), F32),
                                       ssd_ref[gb, 0:blk, GROUP_WIDTH:].astype(F32)], axis=0)
            else:
                ext = ssd_ref[gb, r * blk - HALO:(r + 1) * blk, GROUP_WIDTH:].astype(F32)
            acc = cb_ref[...] + cw_ref[SSD_CONV - 1:SSD_CONV, :] * ext[HALO:, :]
            for shift in range(1, SSD_CONV):
                acc = acc + cw_ref[SSD_CONV - 1 - shift:SSD_CONV - shift, :] * pltpu.roll(ext, shift, 0)[HALO:, :]
            xc_ref[gb, r * blk:(r + 1) * blk, :] = _silu(acc)
    dt_ref[...] = _softplus(small_ref[...] + dtb_ref[...])
    st_ref[...] = jnp.zeros_like(st_ref)

    a_lane = -jnp.exp(alog_ref[...])
    tri = (_iota2((c_len, c_len), 0) >= _iota2((c_len, c_len), 1))
    tri16 = tri.astype(BF16)
    sel16 = (_iota2((8, LANES), 0) + SMALL_DT == _iota2((8, LANES), 1)).astype(BF16)
    lane = _iota2((c_len, LANES), 1)
    lo = lane < HEAD_DIM

    def chunk(n, _):
        for gb in range(MIX_BATCH):
            chunk_one(n, gb)
        return 0

    def chunk_one(n, gb):
        rows = pl.ds(pl.multiple_of(n * c_len, c_len), c_len)
        xs = xc_ref[gb, rows, 0:GROUP_WIDTH]
        bm = xc_ref[gb, rows, GROUP_WIDTH:GROUP_WIDTH + LANES]
        cm = xc_ref[gb, rows, GROUP_WIDTH + LANES:]
        dt = dt_ref[gb, rows, :]
        a_cs = _masked_sums(tri16, dt * a_lane, 3)
        a_row = _masked_sums(sel16, a_cs, 3, transpose_rhs=True)
        a_last = a_cs[c_len - 1:c_len, :]
        to_end = jnp.exp(a_last - a_cs)
        from_start = jnp.exp(a_cs)
        chunk_decay = jnp.exp(a_last)
        cm16 = cm.astype(BF16)
        y_halves = []
        for g in range(2):
            in_group = (lane >= SSD_STATE * g) & (lane < SSD_STATE * (g + 1))
            b_g = jnp.where(in_group, bm, 0.0).astype(BF16)
            scores = _dot_nt(cm16, b_g)
            heads = (2 * g, 2 * g + 1)

            def per_head(x):
                return jnp.where(lo[:x.shape[0]], _lane_col(x, SMALL_DT + heads[0]),
                                 _lane_col(x, SMALL_DT + heads[1]))

            xdt = xs[:, g * LANES:(g + 1) * LANES] * per_head(dt)
            xdt16 = xdt.astype(BF16)
            y_intra = []
            for h in heads:
                seg = _lane_col(a_cs, SMALL_DT + h) - a_row[h:h + 1, :]
                decay = jnp.where(tri, jnp.exp(jnp.where(tri, seg, 0.0)), 0.0)
                y_intra.append(_dot((scores * decay).astype(BF16), xdt16))
            y = jnp.where(lo, y_intra[0], y_intra[1])
            state = st_ref[gb, g]
            y = y + _dot(cm16, state.astype(BF16)) * per_head(from_start)
            d_state = _dot_tn(b_g, (xdt * per_head(to_end)).astype(BF16))
            st_ref[gb, g] = state * per_head(chunk_decay) + d_state
            y_halves.append(y)
        y = jnp.concatenate(y_halves, axis=-1) + xs * dskip_ref[...]
        y = y * _silu(ssd_ref[gb, rows, 0:GROUP_WIDTH].astype(F32))
        out = jnp.concatenate([_rms(y[:, i * LANES:(i + 1) * LANES], gn_ref[:, i * LANES:(i + 1) * LANES])
                               for i in range(2)], axis=-1)
        o_ref[gb, rows, :] = out.astype(o_ref.dtype)

    lax.fori_loop(0, seq // c_len, chunk, 0)


def _batch_spec(seq, width):
    return pl.BlockSpec((MIX_BATCH, seq, width), lambda bi: (bi, 0, 0))


def _ssd(ssd3, small3, cw, cb, dtb, alog, dskip, gn):
    b, seq, width = ssd3.shape
    conv_dim = GROUP_WIDTH + 2 * LANES
    kern = functools.partial(_ssd_kernel, seq=seq)

    def vec(n, rows=1):
        return pl.BlockSpec((rows, n), lambda bi: (0, 0))

    return pl.pallas_call(
        kern,
        grid=(b // MIX_BATCH,),
        in_specs=[
            _batch_spec(seq, width), _batch_spec(seq, LANES),
            vec(conv_dim, SSD_CONV), vec(conv_dim), vec(LANES), vec(LANES), vec(GROUP_WIDTH), vec(GROUP_WIDTH),
        ],
        out_specs=_batch_spec(seq, GROUP_WIDTH),
        out_shape=jax.ShapeDtypeStruct((b, seq, GROUP_WIDTH), BF16),
        scratch_shapes=[pltpu.VMEM((MIX_BATCH, seq, conv_dim), F32), pltpu.VMEM((MIX_BATCH, seq, LANES), F32),
                        pltpu.VMEM((MIX_BATCH, 2, LANES, LANES), F32)],
        compiler_params=_cparams(("parallel",)),
        name="ssd",
    )(ssd3, small3, cw, cb, dtb, alog, dskip, gn)


def _gla_levels(c_len, sub):
    sizes = []
    size = sub
    while size <= c_len:
        sizes.append(size)
        size *= 2
    return sizes


def _gla_consts(c_len, dk, dv, sub):
    dkh = dk // N_HEADS
    dvh = dv // N_HEADS
    sizes = _gla_levels(c_len, sub)
    row = _iota2((c_len, c_len), 0)
    col = _iota2((c_len, c_len), 1)
    sum_masks = []
    for size in sizes:
        same = (row // size) == (col // size)
        sum_masks.append((same & (col <= row)).astype(BF16))
        sum_masks.append(same.astype(BF16))
    arow = _iota2((c_len, N_HEADS * c_len), 0)
    acol = _iota2((c_len, N_HEADS * c_len), 1) % c_len
    keep = [((arow // sub) == (acol // sub)) & (acol <= arow)]
    for level in range(1, len(sizes)):
        size, half = sizes[level], sizes[level - 1]
        keep.append(((arow // size) == (acol // size)) & ((arow % size) >= half) & ((acol % size) < half))
    klane = _iota2((c_len, dk), 1) // dkh
    vlane = _iota2((c_len, dv), 1) // dvh
    diag = (_iota2((dv, dk), 0) // dvh) == (_iota2((dv, dk), 1) // dkh)
    return dict(sizes=sizes, sum_mask=jnp.concatenate(sum_masks, axis=0), keep=keep,
                klane=klane, vlane=vlane, diag=diag)


def _gla_chunk(q, k, v, logf, state_t, consts):
    c_len = q.shape[0]
    sizes = consts["sizes"]
    sums = _masked_sums(consts["sum_mask"], logf, 2)
    cs = [sums[(2 * i) * c_len:(2 * i + 1) * c_len] for i in range(len(sizes))]
    tot = [sums[(2 * i + 1) * c_len:(2 * i + 2) * c_len] for i in range(len(sizes))]

    def stack_heads(x, lane_head):
        return jnp.concatenate([jnp.where(lane_head == h, x, 0.0) for h in range(N_HEADS)], axis=0).astype(BF16)

    q0 = (q * jnp.exp(cs[0])).astype(BF16)
    k0 = k * jnp.exp(-cs[0])
    att = jnp.where(consts["keep"][0], _dot_nt(q0, stack_heads(k0, consts["klane"])), 0.0)
    for level in range(1, len(sizes)):
        q_l = q0 if level == 1 else (q * jnp.exp(cs[level - 1])).astype(BF16)
        k_l = k * jnp.exp(tot[level - 1] - cs[level - 1])
        att = jnp.where(consts["keep"][level], _dot_nt(q_l, stack_heads(k_l, consts["klane"])), att)
    o = _dot(att.astype(BF16), stack_heads(v, consts["vlane"]))

    q_s = (q * jnp.exp(cs[-1])).astype(BF16)
    k_s = (k * jnp.exp(tot[-1] - cs[-1])).astype(BF16)
    o = o + _dot_nt(q_s, state_t.astype(BF16))
    d_state = jnp.where(consts["diag"], _dot_tn(v.astype(BF16), k_s), 0.0)
    new_state = state_t * jnp.exp(tot[-1][0:1, :]) + d_state
    return o, new_state


def _gla_kernel(gla_ref, small_ref, wg_ref, bg_ref, on_ref, o_ref, st_ref, *, seq):
    c_len = GLA_CHUNK
    dk = GLA_KEY_DIM
    st_ref[...] = jnp.zeros_like(st_ref)
    scale = (dk // N_HEADS) ** -0.5
    consts = _gla_consts(c_len, dk, GROUP_WIDTH, c_len)

    def chunk(n, _):
        rows = pl.ds(pl.multiple_of(n * c_len, c_len), c_len)
        for gb in range(MIX_BATCH):
            q = gla_ref[gb, rows, 0:dk].astype(F32) * scale
            k = gla_ref[gb, rows, dk:2 * dk].astype(F32)
            v = gla_ref[gb, rows, 2 * dk:2 * dk + GROUP_WIDTH].astype(F32)
            r = gla_ref[gb, rows, 2 * dk + GROUP_WIDTH:].astype(F32)
            gate = _dot(small_ref[gb, rows, :].astype(BF16), wg_ref[...]) + bg_ref[...]
            logf = _log_sigmoid(gate) * (1.0 / GLA_TAU)
            o, new_state = _gla_chunk(q, k, v, logf, st_ref[gb], consts)
            st_ref[gb] = new_state
            o = _rms_heads64_wide(o, on_ref[...]) * _silu(r)
            o_ref[gb, rows, :] = o.astype(o_ref.dtype)
        return 0

    lax.fori_loop(0, seq // c_len, chunk, 0)


def _gla(gla3, small3, wg, bg, on):
    b, seq, width = gla3.shape
    kern = functools.partial(_gla_kernel, seq=seq)
    return pl.pallas_call(
        kern,
        grid=(b // MIX_BATCH,),
        in_specs=[
            _batch_spec(seq, width), _batch_spec(seq, LANES),
            pl.BlockSpec((LANES, GLA_KEY_DIM), lambda bi: (0, 0)),
            pl.BlockSpec((1, GLA_KEY_DIM), lambda bi: (0, 0)),
            pl.BlockSpec((1, GROUP_WIDTH), lambda bi: (0, 0)),
        ],
        out_specs=_batch_spec(seq, GROUP_WIDTH),
        out_shape=jax.ShapeDtypeStruct((b, seq, GROUP_WIDTH), BF16),
        scratch_shapes=[pltpu.VMEM((MIX_BATCH, GROUP_WIDTH, GLA_KEY_DIM), F32)],
        compiler_params=_cparams(("parallel",)),
        name="gla",
    )(gla3, small3, wg, bg, on)


def _hgrn_kernel(hg_ref, lbl_ref, on_ref, o_ref, st_ref, *, seq, layer):
    c_len = HGRN_CHUNK
    dk = HGRN_FDIM
    st_ref[...] = jnp.zeros_like(st_ref)
    logits = lbl_ref[...]
    e = jnp.exp(logits - jnp.max(logits, axis=0, keepdims=True))
    soft = e / jnp.sum(e, axis=0, keepdims=True)
    lb = jnp.zeros((1, dk), F32)
    for i in range(1, layer + 1):
        lb = lb + soft[i:i + 1, :]
    lb = jnp.clip(lb, 0.0, 1.0 - 1e-6)
    consts = _gla_consts(c_len, dk, GROUP_WIDTH, HGRN_SUB)

    def chunk(n, _):
        rows = pl.ds(pl.multiple_of(n * c_len, c_len), c_len)
        for gb in range(MIX_BATCH):
            q = hg_ref[gb, rows, 0:dk].astype(F32)
            z = hg_ref[gb, rows, dk:2 * dk].astype(F32)
            v = hg_ref[gb, rows, 2 * dk:2 * dk + GROUP_WIDTH].astype(F32)
            g = hg_ref[gb, rows, 2 * dk + GROUP_WIDTH:].astype(F32)
            sig, sig_neg = _sigmoid_pair(z)
            logf = jnp.log(lb + (1.0 - lb) * sig)
            k = (1.0 - lb) * sig_neg
            o, new_state = _gla_chunk(q, k, v, logf, st_ref[gb], consts)
            st_ref[gb] = new_state
            o = _rms_heads64_wide(o, on_ref[...]) * _silu(g)
            o_ref[gb, rows, :] = o.astype(o_ref.dtype)
        return 0

    lax.fori_loop(0, seq // c_len, chunk, 0)


def _hgrn(hg3, lb_logits, on, layer):
    b, seq, width = hg3.shape
    depth = lb_logits.shape[0]
    kern = functools.partial(_hgrn_kernel, seq=seq, layer=layer)
    return pl.pallas_call(
        kern,
        grid=(b // MIX_BATCH,),
        in_specs=[
            _batch_spec(seq, width),
            pl.BlockSpec((depth, HGRN_FDIM), lambda bi: (0, 0)),
            pl.BlockSpec((1, GROUP_WIDTH), lambda bi: (0, 0)),
        ],
        out_specs=_batch_spec(seq, GROUP_WIDTH),
        out_shape=jax.ShapeDtypeStruct((b, seq, GROUP_WIDTH), BF16),
        scratch_shapes=[pltpu.VMEM((MIX_BATCH, GROUP_WIDTH, HGRN_FDIM), F32)],
        compiler_params=_cparams(("parallel",)),
        name="hgrn2",
    )(hg3, lb_logits, on)


def _kv_kernel(mem_ref, mn_ref, w_ref, kn_ref, k_ref, v_ref):
    mem_n = _rms(mem_ref[...], mn_ref[...]).astype(BF16)
    kv = _dot(mem_n, w_ref[...])
    for h in range(XA_HEADS):
        cols = slice(h * XA_HEAD_DIM, (h + 1) * XA_HEAD_DIM)
        k_ref[:, cols] = _rms(kv[:, cols], kn_ref[...]).astype(k_ref.dtype)
    v_ref[...] = kv[:, D_MODEL:].astype(v_ref.dtype)


def _kv_proj(mem2d, mem_norm, w_kv, kn, tm):
    m = mem2d.shape[0]
    depth = w_kv.shape[0]
    out = jax.ShapeDtypeStruct((depth, m, D_MODEL), BF16)
    return pl.pallas_call(
        _kv_kernel,
        grid=(depth, m // tm),
        in_specs=[
            pl.BlockSpec((tm, D_MODEL), lambda l, i: (i, 0)),
            pl.BlockSpec((1, D_MODEL), lambda l, i: (0, 0)),
            pl.BlockSpec((None, D_MODEL, 2 * D_MODEL), lambda l, i: (l, 0, 0)),
            pl.BlockSpec((None, 1, XA_HEAD_DIM), lambda l, i: (l, 0, 0)),
        ],
        out_specs=[pl.BlockSpec((None, tm, D_MODEL), lambda l, i: (l, i, 0))] * 2,
        out_shape=[out, out],
        compiler_params=_cparams(("parallel", "parallel")),
        name="mem_kv",
    )(mem2d, mem_norm, w_kv, kn)


def _mix_xattn_kernel(x_ref, yf_ref, ys_ref, yg_ref, yh_ref, wout_ref, ln_ref, wq_ref, qn_ref,
                      k_ref, v_ref, wo_ref, o_ref):
    x = x_ref[...]
    for i, y_ref in enumerate((yf_ref, ys_ref, yg_ref, yh_ref)):
        x = x + _dot(y_ref[...], wout_ref[i * GROUP_WIDTH:(i + 1) * GROUP_WIDTH, :])
    h = _rms(x, ln_ref[...]).astype(BF16)
    q = _dot(h, wq_ref[...])
    scale = XA_HEAD_DIM ** -0.5
    outs = []
    for hd in range(XA_HEADS):
        cols = slice(hd * XA_HEAD_DIM, (hd + 1) * XA_HEAD_DIM)
        qh = (_rms(q[:, cols], qn_ref[...]) * scale).astype(BF16)
        logits = _dot_nt(qh, k_ref[:, cols])
        p = jnp.exp(logits - jnp.max(logits, axis=-1, keepdims=True))
        p = p / jnp.sum(p, axis=-1, keepdims=True)
        outs.append(_dot(p.astype(BF16), v_ref[:, cols]).astype(BF16))
    o = jnp.concatenate(outs, axis=-1)
    o_ref[...] = x + _dot(o, wo_ref[...])


def _mix_xattn(x2d, ys, w_out, ln, wq, qn, k3, v3, wo, tm, seq):
    m = x2d.shape[0]
    n_mem = k3.shape[1]
    tiles_per_seq = seq // tm
    row = lambda i: (i, 0)
    const = lambda i: (0, 0)
    mem_spec = pl.BlockSpec((None, n_mem, D_MODEL), lambda i: (i // tiles_per_seq, 0, 0))
    return pl.pallas_call(
        _mix_xattn_kernel,
        grid=(m // tm,),
        in_specs=[
            pl.BlockSpec((tm, D_MODEL), row),
            *[pl.BlockSpec((tm, GROUP_WIDTH), row)] * 4,
            pl.BlockSpec((D_MODEL, D_MODEL), const),
            pl.BlockSpec((1, D_MODEL), const),
            pl.BlockSpec((D_MODEL, D_MODEL), const),
            pl.BlockSpec((1, XA_HEAD_DIM), const),
            mem_spec, mem_spec,
            pl.BlockSpec((D_MODEL, D_MODEL), const),
        ],
        out_specs=pl.BlockSpec((tm, D_MODEL), row),
        out_shape=jax.ShapeDtypeStruct((m, D_MODEL), F32),
        compiler_params=_cparams(("parallel",)),
        name="mix_xattn",
    )(x2d, *ys, w_out, ln, wq, qn, k3, v3, wo)


def _ffn_kernel(x_ref, xp_ref, ln_ref, wup_ref, cw_ref, cb_ref, wdown_ref, o_ref, *, tiles_per_seq):
    i = pl.program_id(0)
    tm = x_ref.shape[0]
    x = x_ref[...]
    first = (i % tiles_per_seq) == 0
    h = _rms(x, ln_ref[...]).astype(BF16)
    hp = jnp.where(first, 0.0, _rms(xp_ref[...], ln_ref[...])).astype(BF16)
    h_ext = jnp.concatenate([hp, h], axis=0)
    acc = x
    for c in range(D_FF // FF_TILE):
        cols = slice(c * FF_TILE, (c + 1) * FF_TILE)
        vcols = slice(D_FF + c * FF_TILE, D_FF + (c + 1) * FF_TILE)
        gate = _dot(h_ext, wup_ref[:, cols])
        conv = cb_ref[:, cols] + cw_ref[FFN_CONV - 1:FFN_CONV, cols] * gate[HALO:, :]
        for shift in range(1, FFN_CONV):
            conv = conv + (cw_ref[FFN_CONV - 1 - shift:FFN_CONV - shift, cols]
                           * pltpu.roll(gate, shift, 0)[HALO:, :])
        val = _dot(h, wup_ref[:, vcols])
        act = (_silu(conv) * val).astype(BF16)
        acc = acc + _dot(act, wdown_ref[cols, :])
    o_ref[...] = acc


def _ffn(x2d, ln, w_up, cw, cb, w_down, tm, seq):
    m = x2d.shape[0]
    kern = functools.partial(_ffn_kernel, tiles_per_seq=seq // tm)
    const = lambda i: (0, 0)
    halo_blocks = tm // HALO
    return pl.pallas_call(
        kern,
        grid=(m // tm,),
        in_specs=[
            pl.BlockSpec((tm, D_MODEL), lambda i: (i, 0)),
            pl.BlockSpec((HALO, D_MODEL), lambda i: (jnp.maximum(i * halo_blocks - 1, 0), 0)),
            pl.BlockSpec((1, D_MODEL), const),
            pl.BlockSpec((D_MODEL, 2 * D_FF), const),
            pl.BlockSpec((FFN_CONV, D_FF), const),
            pl.BlockSpec((1, D_FF), const),
            pl.BlockSpec((D_FF, D_MODEL), const),
        ],
        out_specs=pl.BlockSpec((tm, D_MODEL), lambda i: (i, 0)),
        out_shape=jax.ShapeDtypeStruct((m, D_MODEL), F32),
        compiler_params=_cparams(("parallel",)),
        name="conv_glu_ffn",
    )(x2d, x2d, ln, w_up, cw, cb, w_down)


def _pad_lanes(vec, offset, total=LANES):
    vec = vec.astype(F32).reshape(1, -1)
    return jnp.pad(vec, ((0, 0), (offset, total - offset - vec.shape[1])))


def _arrange_w_in(w):
    fq, fk, fv, ff, sz, sxbc, sdt, gq, gk, gv, ga, gr, hq, hf, hi, hg = jnp.split(
        w, [256, 512, 768, 772, 1028, 1540, 1544, 1672, 1800, 2056, 2072, 2328, 2584, 2840, 3096], axis=1)
    small = jnp.concatenate([ff, sdt, ga], axis=1)
    small = jnp.pad(small, ((0, 0), (0, LANES - small.shape[1])))
    return jnp.concatenate([fq, fk, fv, sz, sxbc, gq, gk, gv, gr, hq, hf, hi, hg, small], axis=1).astype(BF16)


def kernel(x, mem, ln_mix, w_in, w_out, fox_f_bias, fox_qn, fox_kn, fox_on, ssd_conv_w, ssd_conv_b, ssd_dt_bias,
           ssd_a_log, ssd_d, ssd_norm, gla_w_g2, gla_b_g2, gla_norm, hgrn_lb_logits, hgrn_norm, ln_xattn, mem_norm,
           xa_wq, xa_wkv, xa_wo, xa_qn, xa_kn, ln_ffn, ffn_w_up, ffn_conv_w, ffn_conv_b, ffn_w_down):
    b, seq, d = x.shape
    depth = w_in.shape[0]
    n_mem = mem.shape[1]
    m = b * seq
    tm = min(512, seq)
    x2d = x.reshape(m, d)

    k_all, v_all = _kv_proj(mem.reshape(b * n_mem, d), mem_norm.reshape(1, d), xa_wkv.astype(BF16),
                            xa_kn.reshape(depth, 1, XA_HEAD_DIM), min(512, b * n_mem))
    for l in range(depth):
        fox, ssd, gla, hgrn, small = _in_proj(x2d, ln_mix[l].reshape(1, d), _arrange_w_in(w_in[l]), tm)
        small3 = small.reshape(b, seq, LANES)
        tile2 = lambda v: jnp.tile(v.astype(F32).reshape(1, -1), (1, LANES // HEAD_DIM))
        tile4 = lambda v: jnp.tile(v.astype(F32).reshape(1, -1), (1, GROUP_WIDTH // HEAD_DIM))
        y_fox = _fox(fox.reshape(b, seq, -1), small3, _pad_lanes(fox_f_bias[l], SMALL_FF),
                     tile2(fox_qn[l]), tile2(fox_kn[l]), tile2(fox_on[l]))
        y_ssd = _ssd(ssd.reshape(b, seq, -1), small3, ssd_conv_w[l], ssd_conv_b[l].reshape(1, -1),
                     _pad_lanes(ssd_dt_bias[l], SMALL_DT), _pad_lanes(ssd_a_log[l], SMALL_DT),
                     jnp.repeat(ssd_d[l].astype(F32), HEAD_DIM).reshape(1, -1), ssd_norm[l].reshape(1, -1))
        wg = jnp.pad(gla_w_g2[l], ((SMALL_GA, LANES - SMALL_GA - GLA_GATE_RANK), (0, 0))).astype(BF16)
        y_gla = _gla(gla.reshape(b, seq, -1), small3, wg, gla_b_g2[l].reshape(1, -1), tile4(gla_norm[l]))
        y_hgrn = _hgrn(hgrn.reshape(b, seq, -1), hgrn_lb_logits, tile4(hgrn_norm[l]), l)
        ys = [y.reshape(m, GROUP_WIDTH) for y in (y_fox, y_ssd, y_gla, y_hgrn)]
        x2d = _mix_xattn(x2d, ys, w_out[l].astype(BF16), ln_xattn[l].reshape(1, d), xa_wq[l].astype(BF16),
                         xa_qn[l].reshape(1, -1), k_all[l].reshape(b, n_mem, d), v_all[l].reshape(b, n_mem, d),
                         xa_wo[l].astype(BF16), tm, seq)
        x2d = _ffn(x2d, ln_ffn[l].reshape(1, d), ffn_w_up[l].astype(BF16), ffn_conv_w[l],
                   ffn_conv_b[l].reshape(1, -1), ffn_w_down[l].astype(BF16), tm, seq)
    return x2d.reshape(b, seq, d)
```

```python
import functools

import jax
import jax.numpy as jnp
from jax import lax
from jax.experimental import pallas as pl
from jax.experimental.pallas import tpu as pltpu

F32 = jnp.float32
BF16 = jnp.bfloat16

EPS = 1e-6
MASK_VALUE = -1e30
D_MODEL = 1024
GROUP_WIDTH = 256
HEAD_DIM = 64
LANES = 128
N_HEADS = 4
SSD_STATE = 64
SSD_CONV = 4
SSD_CHUNK = 128
GLA_KEY_DIM = 128
GLA_GATE_RANK = 16
GLA_TAU = 16.0
GLA_CHUNK = 64
HGRN_FDIM = 256
HGRN_CHUNK = 64
HGRN_SUB = 16
XA_HEADS = 4
XA_HEAD_DIM = 256
D_FF = 2816
FFN_CONV = 3
FF_TILE = 256
FFN_ROWS = 1024
HALO = 16
SSD_BATCH = 2
GLA_BATCH = 4
HGRN_BATCH = 4

SMALL_FF = 0
SMALL_DT = 4
SMALL_GA = 8

VMEM_LIMIT = 56 * 1024 * 1024

HIGHEST = lax.Precision.HIGHEST


def _cparams(sem):
    return pltpu.CompilerParams(dimension_semantics=sem, vmem_limit_bytes=VMEM_LIMIT)


def _resident(shape):
    return pl.BlockSpec(shape, lambda *_: (0,) * len(shape), pipeline_mode=pl.Buffered(1))


def _dot(a, b, precision=None):
    return jnp.dot(a, b, preferred_element_type=F32, precision=precision)


def _dot_nt(a, b, precision=None):
    return lax.dot_general(a, b, (((1,), (1,)), ((), ())), preferred_element_type=F32, precision=precision)


def _dot_tn(a, b, precision=None):
    return lax.dot_general(a, b, (((0,), (0,)), ((), ())), preferred_element_type=F32, precision=precision)


def _rms(x, w):
    return x * lax.rsqrt(jnp.mean(x * x, axis=-1, keepdims=True) + EPS) * w


def _rms_heads64(x, w):
    lane = lax.broadcasted_iota(jnp.int32, x.shape, 1)
    lo = lane < HEAD_DIM
    sq = x * x
    s_lo = jnp.sum(jnp.where(lo, sq, 0.0), axis=-1, keepdims=True)
    s_hi = jnp.sum(jnp.where(lo, 0.0, sq), axis=-1, keepdims=True)
    ms = jnp.where(lo, s_lo, s_hi) * (1.0 / HEAD_DIM)
    return x * lax.rsqrt(ms + EPS) * w


def _rms_heads64_wide(x, w):
    return jnp.concatenate(
        [_rms_heads64(x[:, i * LANES:(i + 1) * LANES], w[:, i * LANES:(i + 1) * LANES]) for i in range(2)], axis=-1)


def _sigmoid_pair(z):
    e = jnp.exp(-jnp.abs(z))
    big = 1.0 / (1.0 + e)
    small = e * big
    pos = z >= 0
    return jnp.where(pos, big, small), jnp.where(pos, small, big)


def _silu(z):
    return z * (1.0 / (1.0 + jnp.exp(-z)))


def _log_sigmoid(z):
    return jnp.minimum(z, 0.0) - jnp.log1p(jnp.exp(-jnp.abs(z)))


def _softplus(z):
    return jnp.maximum(z, 0.0) + jnp.log1p(jnp.exp(-jnp.abs(z)))


def _iota2(shape, axis):
    return lax.broadcasted_iota(jnp.int32, shape, axis)


def _lane_col(x, lane_index):
    return x[:, lane_index:lane_index + 1]


def _masked_sums(mask16, x, pieces, form="mask_x"):
    total = None
    rest = x
    for index in range(pieces):
        part = rest.astype(BF16)
        if form == "mask_x":
            term = _dot(mask16, part)
        elif form == "mask_xt":
            term = _dot_nt(mask16, part)
        else:
            term = _dot(part, mask16)
        total = term if total is None else total + term
        if index + 1 < pieces:
            rest = rest - part.astype(F32)
    return total


IN_GROUPS = (768, 768, 768, 1024, 128)
D_IN_PAD = sum(IN_GROUPS)


def _in_proj_kernel(x_ref, ln_ref, w_ref, fox_ref, ssd_ref, gla_ref, hgrn_ref, small_ref):
    h = _rms(x_ref[...], ln_ref[...]).astype(BF16)
    outs = (fox_ref, ssd_ref, gla_ref, hgrn_ref, small_ref)
    start = 0
    for width, out in zip(IN_GROUPS, outs):
        out[...] = _dot(h, w_ref[:, start:start + width]).astype(out.dtype)
        start += width


def _in_proj(x2d, ln, w_r, tm):
    m = x2d.shape[0]
    out_shape = [jax.ShapeDtypeStruct((m, n), BF16) for n in IN_GROUPS[:4]]
    out_shape.append(jax.ShapeDtypeStruct((m, IN_GROUPS[4]), F32))
    return pl.pallas_call(
        _in_proj_kernel,
        grid=(m // tm,),
        in_specs=[
            pl.BlockSpec((tm, D_MODEL), lambda i: (i, 0)),
            pl.BlockSpec((1, D_MODEL), lambda i: (0, 0)),
            _resident((D_MODEL, D_IN_PAD)),
        ],
        out_specs=[pl.BlockSpec((tm, n), lambda i: (i, 0)) for n in IN_GROUPS],
        out_shape=out_shape,
        compiler_params=_cparams(("parallel",)),
        name="in_proj",
    )(x2d, ln, w_r)


FOX_TILE = 256
FOX_CUM_BLOCK = 256
N_SPLIT = 3


def _split3(c):
    hi = c.astype(BF16).astype(F32)
    r = c - hi
    mid = r.astype(BF16).astype(F32)
    lo = (r - mid).astype(BF16).astype(F32)
    return hi, mid, lo


def _fox_kernel(q_ref, k_ref, v_ref, small_ref, bias_ref, qn_ref, kn_ref, on_ref, o_ref,
                ka_ref, qaug_ref, va_ref, *, seq):
    qi = pl.program_id(1)
    tq = FOX_TILE
    halves = GROUP_WIDTH // LANES

    def own_lanes(lane, hh):
        return (lane >= HEAD_DIM * hh) & (lane < HEAD_DIM * (hh + 1))

    @pl.when(qi == 0)
    def _prepare():
        nblk = seq // FOX_CUM_BLOCK
        tri = (_iota2((FOX_CUM_BLOCK, FOX_CUM_BLOCK), 0) >= _iota2((FOX_CUM_BLOCK, FOX_CUM_BLOCK), 1)).astype(BF16)
        lane = _iota2((FOX_CUM_BLOCK, LANES), 1)
        carry = jnp.zeros((1, LANES), F32)
        for blk in range(nblk):
            rows = pl.ds(blk * FOX_CUM_BLOCK, FOX_CUM_BLOCK)
            logf = _log_sigmoid(small_ref[rows, :] + bias_ref[...])
            c = _masked_sums(tri, logf, 3) + carry
            carry = c[FOX_CUM_BLOCK - 1:FOX_CUM_BLOCK, :]
            for half in range(halves):
                lanes = slice(half * LANES, (half + 1) * LANES)
                kn = _rms_heads64(k_ref[rows, lanes].astype(F32), kn_ref[...])
                v_half = v_ref[rows, lanes]
                for hh in range(2):
                    va_ref[2 * half + hh, rows, :] = jnp.where(own_lanes(lane, hh), v_half, jnp.ones_like(v_half))
                    head = 2 * half + hh
                    hi, mid, lo = _split3(_lane_col(c, SMALL_FF + head))
                    base = HEAD_DIM * (1 - hh)
                    k_aug = jnp.where((lane >= base) & (lane < base + N_SPLIT), 1.0, 0.0)
                    q_aug = jnp.where((lane >= base + N_SPLIT) & (lane < base + 2 * N_SPLIT), 1.0, 0.0)
                    for piece_index, piece in enumerate((hi, mid, lo)):
                        k_aug = jnp.where(lane == base + N_SPLIT + piece_index, -piece, k_aug)
                        q_aug = jnp.where(lane == base + piece_index, piece, q_aug)
                    own = own_lanes(lane, hh)
                    ka_ref[head, rows, :] = jnp.where(own, kn, k_aug).astype(BF16)
                    qaug_ref[head, rows, :] = jnp.where(own, 0.0, q_aug)

    scale = HEAD_DIM ** -0.5
    qrows = pl.ds(pl.multiple_of(qi * tq, tq), tq)
    lane = _iota2((tq, LANES), 1)
    lo_half = lane < HEAD_DIM
    causal = _iota2((tq, tq), 0) >= _iota2((tq, tq), 1)
    qas = []
    for half in range(halves):
        lanes = slice(half * LANES, (half + 1) * LANES)
        qn = _rms_heads64(q_ref[:, lanes].astype(F32), qn_ref[...]) * scale
        for hh in range(2):
            qas.append(jnp.where(own_lanes(lane, hh), qn, qaug_ref[2 * half + hh, qrows, :]).astype(BF16))

    def block(j, carry, masked):
        krows = pl.ds(pl.multiple_of(j * tq, tq), tq)
        heads = range(N_HEADS)
        m_prev = [carry[head][0] for head in heads]
        acc = [carry[head][1] for head in heads]
        s = [_dot_nt(qas[head], ka_ref[head, krows, :]) for head in heads]
        if masked:
            s = [jnp.where(causal, x, MASK_VALUE) for x in s]
        m_new = _each(lambda m, x: jnp.maximum(m, jnp.max(x, axis=-1, keepdims=True)), m_prev, s)
        p = _each(lambda x, m: jnp.exp(x - m).astype(BF16), s, m_new)
        pv = [_dot(p[head], va_ref[head, krows, :]) for head in heads]
        acc = _each(lambda m0, m1, a, x: jnp.exp(m0 - m1) * a + x, m_prev, m_new, acc, pv)
        return tuple(zip(m_new, acc))

    init = tuple((jnp.full((tq, 1), MASK_VALUE, F32), jnp.zeros((tq, LANES), F32)) for _ in range(N_HEADS))
    carry = lax.fori_loop(0, qi, functools.partial(block, masked=False), init)
    carry = block(qi, carry, True)
    for half in range(halves):
        acc_lo, acc_hi = carry[2 * half][1], carry[2 * half + 1][1]
        o = jnp.where(lo_half, acc_lo / pltpu.roll(acc_lo, HEAD_DIM, 1), acc_hi / pltpu.roll(acc_hi, HEAD_DIM, 1))
        o_ref[:, half * LANES:(half + 1) * LANES] = _rms_heads64(o, on_ref[...]).astype(o_ref.dtype)


def _fox(fox3, small3, bias, qn, kn, on):
    b, seq, _ = fox3.shape
    tq = FOX_TILE
    kern = functools.partial(_fox_kernel, seq=seq)
    vec = pl.BlockSpec((1, LANES), lambda bi, qi: (0, 0))
    return pl.pallas_call(
        kern,
        grid=(b, seq // tq),
        in_specs=[
            pl.BlockSpec((None, tq, GROUP_WIDTH), lambda bi, qi: (bi, qi, 0)),
            pl.BlockSpec((None, seq, GROUP_WIDTH), lambda bi, qi: (bi, 0, 1)),
            pl.BlockSpec((None, seq, GROUP_WIDTH), lambda bi, qi: (bi, 0, 2)),
            pl.BlockSpec((None, seq, LANES), lambda bi, qi: (bi, 0, 0)),
            vec, vec, vec, vec,
        ],
        out_specs=pl.BlockSpec((None, tq, GROUP_WIDTH), lambda bi, qi: (bi, qi, 0)),
        out_shape=jax.ShapeDtypeStruct((b, seq, GROUP_WIDTH), BF16),
        scratch_shapes=[pltpu.VMEM((N_HEADS, seq, LANES), BF16), pltpu.VMEM((N_HEADS, seq, LANES), F32),
                        pltpu.VMEM((N_HEADS, seq, LANES), BF16)],
        compiler_params=_cparams(("parallel", "arbitrary")),
        name="fox",
    )(fox3, fox3, fox3, small3, bias, qn, kn, on)


def _ssd_kernel(ssd_ref, small_ref, cw_ref, cb_ref, dtb_ref, alog_ref, dskip_ref, gn_ref, o_ref,
                st_ref, *, seq):
    c_len = SSD_CHUNK
    assert c_len == LANES
    n_batch = ssd_ref.shape[0]
    st_ref[...] = jnp.zeros_like(st_ref)

    a_lane = -jnp.exp(alog_ref[...])
    tri = (_iota2((c_len, c_len), 0) >= _iota2((c_len, c_len), 1))
    tri16 = tri.astype(BF16)
    sel16 = (_iota2((8, LANES), 0) + SMALL_DT == _iota2((8, LANES), 1)).astype(BF16)
    lane = _iota2((c_len, LANES), 1)
    lo = lane < HEAD_DIM

    def chunk(n, _):
        for gb in range(n_batch):
            chunk_one(n, gb)
        return 0

    def chunk_one(n, gb):
        rows = pl.ds(pl.multiple_of(n * c_len, c_len), c_len)
        halo_rows = pl.ds(pl.multiple_of(jnp.maximum(n * c_len - HALO, 0), HALO), HALO)
        halo = jnp.where(n > 0, ssd_ref[gb, halo_rows, GROUP_WIDTH:].astype(F32), 0.0)
        cur = ssd_ref[gb, rows, GROUP_WIDTH:].astype(F32)
        ext = jnp.concatenate([halo, cur], axis=0)
        conv = cb_ref[...] + cw_ref[SSD_CONV - 1:SSD_CONV, :] * cur
        for shift in range(1, SSD_CONV):
            conv = conv + cw_ref[SSD_CONV - 1 - shift:SSD_CONV - shift, :] * pltpu.roll(ext, shift, 0)[HALO:, :]
        xc = _silu(conv)
        xs = xc[:, 0:GROUP_WIDTH]
        bm = xc[:, GROUP_WIDTH:GROUP_WIDTH + LANES]
        cm = xc[:, GROUP_WIDTH + LANES:]
        dt = _softplus(small_ref[gb, rows, :] + dtb_ref[...])
        a_cs = _masked_sums(tri16, dt * a_lane, 3)
        a_row = _masked_sums(sel16, a_cs, 3, form="mask_xt")
        cm16 = cm.astype(BF16)
        y_halves = []
        for g in range(2):
            in_group = (lane >= SSD_STATE * g) & (lane < SSD_STATE * (g + 1))
            b_g = jnp.where(in_group, bm, 0.0).astype(BF16)
            scores = _dot_nt(cm16, b_g)
            heads = (2 * g, 2 * g + 1)
            a_col = [jnp.broadcast_to(_lane_col(a_cs, SMALL_DT + h), (c_len, LANES)) for h in heads]
            dt_col = [_lane_col(dt, SMALL_DT + h) for h in heads]
            a_own = jnp.where(lo, a_col[0], a_col[1])
            a_last = a_own[c_len - 1:c_len, :]
            xdt = xs[:, g * LANES:(g + 1) * LANES] * jnp.where(lo, dt_col[0], dt_col[1])
            xdt16 = xdt.astype(BF16)
            y_intra = []
            for index, h in enumerate(heads):
                seg = a_col[index] - a_row[h:h + 1, :]
                decay = jnp.where(tri, jnp.exp(jnp.where(tri, seg, 0.0)), 0.0)
                y_intra.append(_dot((scores * decay).astype(BF16), xdt16))
            y = jnp.where(lo, y_intra[0], y_intra[1])
            state = st_ref[gb, g]
            y = y + _dot(cm16, state.astype(BF16)) * jnp.exp(a_own)
            d_state = _dot_tn(b_g, (xdt * jnp.exp(a_last - a_own)).astype(BF16))
            st_ref[gb, g] = state * jnp.exp(a_last) + d_state
            y_halves.append(y)
        y = jnp.concatenate(y_halves, axis=-1) + xs * dskip_ref[...]
        y = y * _silu(ssd_ref[gb, rows, 0:GROUP_WIDTH].astype(F32))
        out = jnp.concatenate([_rms(y[:, i * LANES:(i + 1) * LANES], gn_ref[:, i * LANES:(i + 1) * LANES])
                               for i in range(2)], axis=-1)
        o_ref[gb, rows, :] = out.astype(o_ref.dtype)

    lax.fori_loop(0, seq // c_len, chunk, 0)


def _batch_spec(n_batch, seq, width):
    return pl.BlockSpec((n_batch, seq, width), lambda bi: (bi, 0, 0))


def _ssd(ssd3, small3, cw, cb, dtb, alog, dskip, gn):
    b, seq, width = ssd3.shape
    conv_dim = GROUP_WIDTH + 2 * LANES
    kern = functools.partial(_ssd_kernel, seq=seq)
    nb = min(SSD_BATCH, b)

    def vec(n, rows=1):
        return pl.BlockSpec((rows, n), lambda bi: (0, 0))

    return pl.pallas_call(
        kern,
        grid=(b // nb,),
        in_specs=[
            _batch_spec(nb, seq, width), _batch_spec(nb, seq, LANES),
            vec(conv_dim, SSD_CONV), vec(conv_dim), vec(LANES), vec(LANES), vec(GROUP_WIDTH), vec(GROUP_WIDTH),
        ],
        out_specs=_batch_spec(nb, seq, GROUP_WIDTH),
        out_shape=jax.ShapeDtypeStruct((b, seq, GROUP_WIDTH), BF16),
        scratch_shapes=[pltpu.VMEM((nb, 2, LANES, LANES), F32)],
        compiler_params=_cparams(("parallel",)),
        name="ssd",
    )(ssd3, small3, cw, cb, dtb, alog, dskip, gn)


def _gla_levels(c_len, sub):
    sizes = []
    size = sub
    while size <= c_len:
        sizes.append(size)
        size *= 2
    return sizes


def _gla_consts(c_len, dk, dv, sub):
    dkh = dk // N_HEADS
    dvh = dv // N_HEADS
    sizes = _gla_levels(c_len, sub)
    row = _iota2((c_len, c_len), 0)
    col = _iota2((c_len, c_len), 1)
    sum_masks = []
    for size in sizes:
        same = (row // size) == (col // size)
        sum_masks.append((same & (col <= row)).astype(BF16))
        sum_masks.append(same.astype(BF16))
    arow = _iota2((c_len, N_HEADS * c_len), 0)
    acol = _iota2((c_len, N_HEADS * c_len), 1) % c_len
    keep = [((arow // sub) == (acol // sub)) & (acol <= arow)]
    for level in range(1, len(sizes)):
        size, half = sizes[level], sizes[level - 1]
        keep.append(((arow // size) == (acol // size)) & ((arow % size) >= half) & ((acol % size) < half))
    klane = _iota2((c_len, dk), 1) // dkh
    vlane = _iota2((c_len, dv), 1) // dvh
    diag = (_iota2((dv, dk), 0) // dvh) == (_iota2((dv, dk), 1) // dkh)
    return dict(sizes=sizes, sum_mask=jnp.concatenate(sum_masks, axis=0), keep=keep,
                klane=klane, vlane=vlane, diag=diag)


def _each(fn, *lists):
    return [fn(*args) for args in zip(*lists)]


def _gla_chunk(q, k, v, logf, state_t, consts):
    c_len = q[0].shape[0]
    sizes = consts["sizes"]
    n_levels = len(sizes)
    sums = _each(lambda x: _masked_sums(consts["sum_mask"], x, 2), logf)
    cs = [[s[(2 * i) * c_len:(2 * i + 1) * c_len] for i in range(n_levels)] for s in sums]
    tot = [[s[(2 * i + 1) * c_len:(2 * i + 2) * c_len] for i in range(n_levels)] for s in sums]

    def stack_heads(x, lane_head):
        return jnp.concatenate([jnp.where(lane_head == h, x, 0.0) for h in range(N_HEADS)], axis=0).astype(BF16)

    q0 = _each(lambda x, c: (x * jnp.exp(c[0])).astype(BF16), q, cs)
    k0 = _each(lambda x, c: stack_heads(x * jnp.exp(-c[0]), consts["klane"]), k, cs)
    att = _each(lambda a, b: jnp.where(consts["keep"][0], _dot_nt(a, b), 0.0), q0, k0)
    for level in range(1, n_levels):
        q_l = q0 if level == 1 else _each(lambda x, c: (x * jnp.exp(c[level - 1])).astype(BF16), q, cs)
        k_l = _each(lambda x, c, t: stack_heads(x * jnp.exp(t[level - 1] - c[level - 1]), consts["klane"]), k, cs, tot)
        att = _each(lambda a, b, prev: jnp.where(consts["keep"][level], _dot_nt(a, b), prev), q_l, k_l, att)
    v_stack = _each(lambda x: stack_heads(x, consts["vlane"]), v)
    o = _each(lambda a, b: _dot(a.astype(BF16), b), att, v_stack)

    q_s = _each(lambda x, c: (x * jnp.exp(c[-1])).astype(BF16), q, cs)
    k_s = _each(lambda x, c, t: (x * jnp.exp(t[-1] - c[-1])).astype(BF16), k, cs, tot)
    o = _each(lambda acc, a, s: acc + _dot_nt(a, s.astype(BF16)), o, q_s, state_t)
    d_state = _each(lambda x, y: jnp.where(consts["diag"], _dot_tn(x.astype(BF16), y), 0.0), v, k_s)
    new_state = _each(lambda s, t, d: s * jnp.exp(t[-1][0:1, :]) + d, state_t, tot, d_state)
    return o, new_state


def _gla_kernel(gla_ref, small_ref, wg_ref, bg_ref, on_ref, o_ref, st_ref, *, seq):
    c_len = GLA_CHUNK
    dk = GLA_KEY_DIM
    st_ref[...] = jnp.zeros_like(st_ref)
    scale = (dk // N_HEADS) ** -0.5
    consts = _gla_consts(c_len, dk, GROUP_WIDTH, c_len)

    def chunk(n, _):
        rows = pl.ds(pl.multiple_of(n * c_len, c_len), c_len)
        batch = range(gla_ref.shape[0])
        gate = [_dot(small_ref[gb, rows, :].astype(BF16), wg_ref[...]) + bg_ref[...] for gb in batch]
        logf = [_log_sigmoid(x) * (1.0 / GLA_TAU) for x in gate]
        q = [gla_ref[gb, rows, 0:dk].astype(F32) * scale for gb in batch]
        k = [gla_ref[gb, rows, dk:2 * dk].astype(F32) for gb in batch]
        v = [gla_ref[gb, rows, 2 * dk:2 * dk + GROUP_WIDTH].astype(F32) for gb in batch]
        o, new_state = _gla_chunk(q, k, v, logf, [st_ref[gb] for gb in batch], consts)
        for gb in batch:
            st_ref[gb] = new_state[gb]
            r = gla_ref[gb, rows, 2 * dk + GROUP_WIDTH:].astype(F32)
            o_ref[gb, rows, :] = (_rms_heads64_wide(o[gb], on_ref[...]) * _silu(r)).astype(o_ref.dtype)
        return 0

    lax.fori_loop(0, seq // c_len, chunk, 0)


def _gla(gla3, small3, wg, bg, on):
    b, seq, width = gla3.shape
    kern = functools.partial(_gla_kernel, seq=seq)
    nb = min(GLA_BATCH, b)
    return pl.pallas_call(
        kern,
        grid=(b // nb,),
        in_specs=[
            _batch_spec(nb, seq, width), _batch_spec(nb, seq, LANES),
            pl.BlockSpec((LANES, GLA_KEY_DIM), lambda bi: (0, 0)),
            pl.BlockSpec((1, GLA_KEY_DIM), lambda bi: (0, 0)),
            pl.BlockSpec((1, GROUP_WIDTH), lambda bi: (0, 0)),
        ],
        out_specs=_batch_spec(nb, seq, GROUP_WIDTH),
        out_shape=jax.ShapeDtypeStruct((b, seq, GROUP_WIDTH), BF16),
        scratch_shapes=[pltpu.VMEM((nb, GROUP_WIDTH, GLA_KEY_DIM), F32)],
        compiler_params=_cparams(("parallel",)),
        name="gla",
    )(gla3, small3, wg, bg, on)


def _hgrn_kernel(hg_ref, lbl_ref, on_ref, o_ref, st_ref, *, seq, layer):
    c_len = HGRN_CHUNK
    dk = HGRN_FDIM
    st_ref[...] = jnp.zeros_like(st_ref)
    logits = lbl_ref[...]
    e = jnp.exp(logits - jnp.max(logits, axis=0, keepdims=True))
    soft = e / jnp.sum(e, axis=0, keepdims=True)
    lb = jnp.zeros((1, dk), F32)
    for i in range(1, layer + 1):
        lb = lb + soft[i:i + 1, :]
    lb = jnp.clip(lb, 0.0, 1.0 - 1e-6)
    consts = _gla_consts(c_len, dk, GROUP_WIDTH, HGRN_SUB)

    def chunk(n, _):
        rows = pl.ds(pl.multiple_of(n * c_len, c_len), c_len)
        batch = range(hg_ref.shape[0])
        sig = [_sigmoid_pair(hg_ref[gb, rows, dk:2 * dk].astype(F32)) for gb in batch]
        logf = [jnp.log(lb + (1.0 - lb) * s[0]) for s in sig]
        k = [(1.0 - lb) * s[1] for s in sig]
        q = [hg_ref[gb, rows, 0:dk].astype(F32) for gb in batch]
        v = [hg_ref[gb, rows, 2 * dk:2 * dk + GROUP_WIDTH].astype(F32) for gb in batch]
        o, new_state = _gla_chunk(q, k, v, logf, [st_ref[gb] for gb in batch], consts)
        for gb in batch:
            st_ref[gb] = new_state[gb]
            g = hg_ref[gb, rows, 2 * dk + GROUP_WIDTH:].astype(F32)
            o_ref[gb, rows, :] = (_rms_heads64_wide(o[gb], on_ref[...]) * _silu(g)).astype(o_ref.dtype)
        return 0

    lax.fori_loop(0, seq // c_len, chunk, 0)


def _hgrn(hg3, lb_logits, on, layer):
    b, seq, width = hg3.shape
    depth = lb_logits.shape[0]
    kern = functools.partial(_hgrn_kernel, seq=seq, layer=layer)
    nb = min(HGRN_BATCH, b)
    return pl.pallas_call(
        kern,
        grid=(b // nb,),
        in_specs=[
            _batch_spec(nb, seq, width),
            pl.BlockSpec((depth, HGRN_FDIM), lambda bi: (0, 0)),
            pl.BlockSpec((1, GROUP_WIDTH), lambda bi: (0, 0)),
        ],
        out_specs=_batch_spec(nb, seq, GROUP_WIDTH),
        out_shape=jax.ShapeDtypeStruct((b, seq, GROUP_WIDTH), BF16),
        scratch_shapes=[pltpu.VMEM((nb, GROUP_WIDTH, HGRN_FDIM), F32)],
        compiler_params=_cparams(("parallel",)),
        name="hgrn2",
    )(hg3, lb_logits, on)


def _kv_kernel(mem_ref, mn_ref, w_ref, kn_ref, k_ref, v_ref):
    mem_n = _rms(mem_ref[...], mn_ref[...]).astype(BF16)
    kv = _dot(mem_n, w_ref[...])
    for h in range(XA_HEADS):
        cols = slice(h * XA_HEAD_DIM, (h + 1) * XA_HEAD_DIM)
        k_ref[:, cols] = _rms(kv[:, cols], kn_ref[...]).astype(k_ref.dtype)
    v_ref[...] = kv[:, D_MODEL:].astype(v_ref.dtype)


def _kv_proj(mem2d, mem_norm, w_kv, kn, tm):
    m = mem2d.shape[0]
    depth = w_kv.shape[0]
    out = jax.ShapeDtypeStruct((depth, m, D_MODEL), BF16)
    return pl.pallas_call(
        _kv_kernel,
        grid=(depth, m // tm),
        in_specs=[
            pl.BlockSpec((tm, D_MODEL), lambda l, i: (i, 0)),
            pl.BlockSpec((1, D_MODEL), lambda l, i: (0, 0)),
            pl.BlockSpec((None, D_MODEL, 2 * D_MODEL), lambda l, i: (l, 0, 0)),
            pl.BlockSpec((None, 1, XA_HEAD_DIM), lambda l, i: (l, 0, 0)),
        ],
        out_specs=[pl.BlockSpec((None, tm, D_MODEL), lambda l, i: (l, i, 0))] * 2,
        out_shape=[out, out],
        compiler_params=_cparams(("parallel", "parallel")),
        name="mem_kv",
    )(mem2d, mem_norm, w_kv, kn)


def _mix_xattn_kernel(x_ref, yf_ref, ys_ref, yg_ref, yh_ref, wout_ref, ln_ref, wq_ref, qn_ref,
                      k_ref, v_ref, wo_ref, o_ref):
    x = x_ref[...]
    for i, y_ref in enumerate((yf_ref, ys_ref, yg_ref, yh_ref)):
        x = x + _dot(y_ref[...], wout_ref[i * GROUP_WIDTH:(i + 1) * GROUP_WIDTH, :])
    h = _rms(x, ln_ref[...]).astype(BF16)
    q = _dot(h, wq_ref[...])
    scale = XA_HEAD_DIM ** -0.5
    outs = []
    for hd in range(XA_HEADS):
        cols = slice(hd * XA_HEAD_DIM, (hd + 1) * XA_HEAD_DIM)
        qh = (_rms(q[:, cols], qn_ref[...]) * scale).astype(BF16)
        logits = _dot_nt(qh, k_ref[:, cols])
        p = jnp.exp(logits - jnp.max(logits, axis=-1, keepdims=True))
        p = p / jnp.sum(p, axis=-1, keepdims=True)
        outs.append(_dot(p.astype(BF16), v_ref[:, cols]).astype(BF16))
    o = jnp.concatenate(outs, axis=-1)
    o_ref[...] = x + _dot(o, wo_ref[...])


def _mix_xattn(x2d, ys, w_out, ln, wq, qn, k3, v3, wo, tm, seq):
    m = x2d.shape[0]
    n_mem = k3.shape[1]
    tiles_per_seq = seq // tm
    row = lambda i: (i, 0)
    const = lambda i: (0, 0)
    mem_spec = pl.BlockSpec((None, n_mem, D_MODEL), lambda i: (i // tiles_per_seq, 0, 0))
    return pl.pallas_call(
        _mix_xattn_kernel,
        grid=(m // tm,),
        in_specs=[
            pl.BlockSpec((tm, D_MODEL), row),
            *[pl.BlockSpec((tm, GROUP_WIDTH), row)] * 4,
            _resident((D_MODEL, D_MODEL)),
            pl.BlockSpec((1, D_MODEL), const),
            _resident((D_MODEL, D_MODEL)),
            pl.BlockSpec((1, XA_HEAD_DIM), const),
            mem_spec, mem_spec,
            _resident((D_MODEL, D_MODEL)),
        ],
        out_specs=pl.BlockSpec((tm, D_MODEL), row),
        out_shape=jax.ShapeDtypeStruct((m, D_MODEL), F32),
        compiler_params=_cparams(("parallel",)),
        name="mix_xattn",
    )(x2d, *ys, w_out, ln, wq, qn, k3, v3, wo)


def _ffn_kernel(x_ref, xp_ref, ln_ref, wup_ref, cw_ref, cb_ref, wdown_ref, o_ref, *, tiles_per_seq):
    i = pl.program_id(0)
    tm = x_ref.shape[0]
    x = x_ref[...]
    first = (i % tiles_per_seq) == 0
    h = _rms(x, ln_ref[...]).astype(BF16)
    hp = jnp.where(first, 0.0, _rms(xp_ref[...], ln_ref[...])).astype(BF16)
    h_ext = jnp.concatenate([hp, h], axis=0)
    acc = x
    for c in range(D_FF // FF_TILE):
        cols = slice(c * FF_TILE, (c + 1) * FF_TILE)
        vcols = slice(D_FF + c * FF_TILE, D_FF + (c + 1) * FF_TILE)
        gate = _dot(h_ext, wup_ref[:, cols])
        conv = cb_ref[:, cols] + cw_ref[FFN_CONV - 1:FFN_CONV, cols] * gate[HALO:, :]
        for shift in range(1, FFN_CONV):
            conv = conv + (cw_ref[FFN_CONV - 1 - shift:FFN_CONV - shift, cols]
                           * pltpu.roll(gate, shift, 0)[HALO:, :])
        val = _dot(h, wup_ref[:, vcols])
        act = (_silu(conv) * val).astype(BF16)
        acc = acc + _dot(act, wdown_ref[cols, :])
    o_ref[...] = acc


def _ffn(x2d, ln, w_up, cw, cb, w_down, tm, seq):
    m = x2d.shape[0]
    kern = functools.partial(_ffn_kernel, tiles_per_seq=seq // tm)
    const = lambda i: (0, 0)
    halo_blocks = tm // HALO
    return pl.pallas_call(
        kern,
        grid=(m // tm,),
        in_specs=[
            pl.BlockSpec((tm, D_MODEL), lambda i: (i, 0)),
            pl.BlockSpec((HALO, D_MODEL), lambda i: (jnp.maximum(i * halo_blocks - 1, 0), 0)),
            pl.BlockSpec((1, D_MODEL), const),
            _resident((D_MODEL, 2 * D_FF)),
            pl.BlockSpec((FFN_CONV, D_FF), const),
            pl.BlockSpec((1, D_FF), const),
            _resident((D_FF, D_MODEL)),
        ],
        out_specs=pl.BlockSpec((tm, D_MODEL), lambda i: (i, 0)),
        out_shape=jax.ShapeDtypeStruct((m, D_MODEL), F32),
        compiler_params=_cparams(("parallel",)),
        name="conv_glu_ffn",
    )(x2d, x2d, ln, w_up, cw, cb, w_down)


def _pad_lanes(vec, offset, total=LANES):
    vec = vec.astype(F32).reshape(1, -1)
    return jnp.pad(vec, ((0, 0), (offset, total - offset - vec.shape[1])))


def _arrange_w_in(w):
    fq, fk, fv, ff, sz, sxbc, sdt, gq, gk, gv, ga, gr, hq, hf, hi, hg = jnp.split(
        w, [256, 512, 768, 772, 1028, 1540, 1544, 1672, 1800, 2056, 2072, 2328, 2584, 2840, 3096], axis=1)
    small = jnp.concatenate([ff, sdt, ga], axis=1)
    small = jnp.pad(small, ((0, 0), (0, LANES - small.shape[1])))
    return jnp.concatenate([fq, fk, fv, sz, sxbc, gq, gk, gv, gr, hq, hf, hi, hg, small], axis=1).astype(BF16)


def kernel(x, mem, ln_mix, w_in, w_out, fox_f_bias, fox_qn, fox_kn, fox_on, ssd_conv_w, ssd_conv_b, ssd_dt_bias,
           ssd_a_log, ssd_d, ssd_norm, gla_w_g2, gla_b_g2, gla_norm, hgrn_lb_logits, hgrn_norm, ln_xattn, mem_norm,
           xa_wq, xa_wkv, xa_wo, xa_qn, xa_kn, ln_ffn, ffn_w_up, ffn_conv_w, ffn_conv_b, ffn_w_down):
    b, seq, d = x.shape
    depth = w_in.shape[0]
    n_mem = mem.shape[1]
    m = b * seq
    tm = min(512, seq)
    x2d = x.reshape(m, d)

    k_all, v_all = _kv_proj(mem.reshape(b * n_mem, d), mem_norm.reshape(1, d), xa_wkv.astype(BF16),
                            xa_kn.reshape(depth, 1, XA_HEAD_DIM), min(512, b * n_mem))
    for l in range(depth):
        fox, ssd, gla, hgrn, small = _in_proj(x2d, ln_mix[l].reshape(1, d), _arrange_w_in(w_in[l]), tm)
        small3 = small.reshape(b, seq, LANES)
        tile2 = lambda v: jnp.tile(v.astype(F32).reshape(1, -1), (1, LANES // HEAD_DIM))
        tile4 = lambda v: jnp.tile(v.astype(F32).reshape(1, -1), (1, GROUP_WIDTH // HEAD_DIM))
        y_fox = _fox(fox.reshape(b, seq, -1), small3, _pad_lanes(fox_f_bias[l], SMALL_FF),
                     tile2(fox_qn[l]), tile2(fox_kn[l]), tile2(fox_on[l]))
        y_ssd = _ssd(ssd.reshape(b, seq, -1), small3, ssd_conv_w[l], ssd_conv_b[l].reshape(1, -1),
                     _pad_lanes(ssd_dt_bias[l], SMALL_DT), _pad_lanes(ssd_a_log[l], SMALL_DT),
                     jnp.repeat(ssd_d[l].astype(F32), HEAD_DIM).reshape(1, -1), ssd_norm[l].reshape(1, -1))
        wg = jnp.pad(gla_w_g2[l], ((SMALL_GA, LANES - SMALL_GA - GLA_GATE_RANK), (0, 0))).astype(BF16)
        y_gla = _gla(gla.reshape(b, seq, -1), small3, wg, gla_b_g2[l].reshape(1, -1), tile4(gla_norm[l]))
        y_hgrn = _hgrn(hgrn.reshape(b, seq, -1), hgrn_lb_logits, tile4(hgrn_norm[l]), l)
        ys = [y.reshape(m, GROUP_WIDTH) for y in (y_fox, y_ssd, y_gla, y_hgrn)]
        x2d = _mix_xattn(x2d, ys, w_out[l].astype(BF16), ln_xattn[l].reshape(1, d), xa_wq[l].astype(BF16),
                         xa_qn[l].reshape(1, -1), k_all[l].reshape(b, n_mem, d), v_all[l].reshape(b, n_mem, d),
                         xa_wo[l].astype(BF16), tm, seq)
        x2d = _ffn(x2d, ln_ffn[l].reshape(1, d), ffn_w_up[l].astype(BF16), ffn_conv_w[l],
                   ffn_conv_b[l].reshape(1, -1), ffn_w_down[l].astype(BF16), min(FFN_ROWS, seq), seq)
    return x2d.reshape(b, seq, d)
```

```python
import functools

import jax
import jax.numpy as jnp
import numpy as np
from jax import lax
from jax.experimental import pallas as pl
from jax.experimental.pallas import tpu as pltpu

F32 = jnp.float32
BF16 = jnp.bfloat16

EPS = 1e-6
MASK_VALUE = -1e30
D_MODEL = 1024
GROUP_WIDTH = 256
HEAD_DIM = 64
LANES = 128
N_HEADS = 4
SSD_STATE = 64
SSD_CONV = 4
SSD_CHUNK = 128
GLA_KEY_DIM = 128
GLA_GATE_RANK = 16
GLA_TAU = 16.0
GLA_CHUNK = 64
HGRN_FDIM = 256
HGRN_CHUNK = 64
HGRN_SUB = 16
XA_HEADS = 4
XA_HEAD_DIM = 256
D_FF = 2816
FFN_CONV = 3
FF_TILE = 256
FFN_ROWS = 1024
ROW_TILE = 1024
HALO = 16
SSD_BATCH = 2
GLA_BATCH = 4
HGRN_BATCH = 4

SMALL_FF = 0
SMALL_DT = 4
SMALL_GA = 8

VMEM_LIMIT = 56 * 1024 * 1024

HIGHEST = lax.Precision.HIGHEST


def _cparams(sem):
    return pltpu.CompilerParams(dimension_semantics=sem, vmem_limit_bytes=VMEM_LIMIT)


def _resident(shape):
    return pl.BlockSpec(shape, lambda *_: (0,) * len(shape), pipeline_mode=pl.Buffered(1))


def _dot(a, b, precision=None):
    return jnp.dot(a, b, preferred_element_type=F32, precision=precision)


def _dot_nt(a, b, precision=None):
    return lax.dot_general(a, b, (((1,), (1,)), ((), ())), preferred_element_type=F32, precision=precision)


def _dot_tn(a, b, precision=None):
    return lax.dot_general(a, b, (((0,), (0,)), ((), ())), preferred_element_type=F32, precision=precision)


def _rms(x, w):
    return x * lax.rsqrt(jnp.mean(x * x, axis=-1, keepdims=True) + EPS) * w


def _rms_heads64(x, w):
    lane = lax.broadcasted_iota(jnp.int32, x.shape, 1)
    lo = lane < HEAD_DIM
    sq = x * x
    s_lo = jnp.sum(jnp.where(lo, sq, 0.0), axis=-1, keepdims=True)
    s_hi = jnp.sum(jnp.where(lo, 0.0, sq), axis=-1, keepdims=True)
    ms = jnp.where(lo, s_lo, s_hi) * (1.0 / HEAD_DIM)
    return x * lax.rsqrt(ms + EPS) * w


def _rms_heads64_wide(x, w):
    return jnp.concatenate(
        [_rms_heads64(x[:, i * LANES:(i + 1) * LANES], w[:, i * LANES:(i + 1) * LANES]) for i in range(2)], axis=-1)


def _sigmoid_pair(z):
    e = jnp.exp(-jnp.abs(z))
    big = 1.0 / (1.0 + e)
    small = e * big
    pos = z >= 0
    return jnp.where(pos, big, small), jnp.where(pos, small, big)


def _silu(z):
    return z * (1.0 / (1.0 + jnp.exp(-z)))


def _log_sigmoid(z):
    return jnp.minimum(z, 0.0) - jnp.log1p(jnp.exp(-jnp.abs(z)))


def _softplus(z):
    return jnp.maximum(z, 0.0) + jnp.log1p(jnp.exp(-jnp.abs(z)))


def _iota2(shape, axis):
    return lax.broadcasted_iota(jnp.int32, shape, axis)


def _lane_col(x, lane_index):
    return x[:, lane_index:lane_index + 1]


def _masked_sums(mask16, x, pieces, form="mask_x"):
    total = None
    rest = x
    for index in range(pieces):
        part = rest.astype(BF16)
        if form == "mask_x":
            term = _dot(mask16, part)
        elif form == "mask_xt":
            term = _dot_nt(mask16, part)
        else:
            term = _dot(part, mask16)
        total = term if total is None else total + term
        if index + 1 < pieces:
            rest = rest - part.astype(F32)
    return total


IN_GROUPS = (768, 768, 768, 1024, 128)
D_IN_PAD = sum(IN_GROUPS)


def _in_proj_kernel(x_ref, ln_ref, w_ref, fox_ref, ssd_ref, gla_ref, hgrn_ref, small_ref):
    h = _rms(x_ref[...], ln_ref[...]).astype(BF16)
    outs = (fox_ref, ssd_ref, gla_ref, hgrn_ref, small_ref)
    start = 0
    for width, out in zip(IN_GROUPS, outs):
        out[...] = _dot(h, w_ref[:, start:start + width]).astype(out.dtype)
        start += width


def _in_proj(x2d, ln, w_r, tm):
    m = x2d.shape[0]
    out_shape = [jax.ShapeDtypeStruct((m, n), BF16) for n in IN_GROUPS[:4]]
    out_shape.append(jax.ShapeDtypeStruct((m, IN_GROUPS[4]), F32))
    return pl.pallas_call(
        _in_proj_kernel,
        grid=(m // tm,),
        in_specs=[
            pl.BlockSpec((tm, D_MODEL), lambda i: (i, 0)),
            pl.BlockSpec((1, D_MODEL), lambda i: (0, 0)),
            _resident((D_MODEL, D_IN_PAD)),
        ],
        out_specs=[pl.BlockSpec((tm, n), lambda i: (i, 0)) for n in IN_GROUPS],
        out_shape=out_shape,
        compiler_params=_cparams(("parallel",)),
        name="in_proj",
    )(x2d, ln, w_r)


FOX_TILE = 256
FOX_CUM_BLOCK = 256
N_SPLIT = 3


def _split3(c):
    hi = c.astype(BF16).astype(F32)
    r = c - hi
    mid = r.astype(BF16).astype(F32)
    lo = (r - mid).astype(BF16).astype(F32)
    return hi, mid, lo


def _fox_selectors():
    selq = np.zeros(((N_SPLIT + 1) * LANES, GROUP_WIDTH), np.float32)
    selk = np.zeros_like(selq)
    for h in range(N_HEADS):
        base = (h // 2) * LANES + HEAD_DIM * (1 - h % 2)
        for p in range(N_SPLIT):
            selq[p * LANES + SMALL_FF + h, base + p] = 1.0
            selq[N_SPLIT * LANES, base + N_SPLIT + p] = 1.0
            selk[N_SPLIT * LANES, base + p] = 1.0
            selk[p * LANES + SMALL_FF + h, base + N_SPLIT + p] = -1.0
    return jnp.asarray(selq, BF16), jnp.asarray(selk, BF16)


def _fox_kernel(q_ref, k_ref, v_ref, small_ref, bias_ref, qn_ref, kn_ref, on_ref, selq_ref, selk_ref, o_ref,
                ka_ref, qaug_ref, va_ref, *, seq):
    qi = pl.program_id(1)
    tq = FOX_TILE
    halves = GROUP_WIDTH // LANES

    def own_lanes(lane, hh):
        return (lane >= HEAD_DIM * hh) & (lane < HEAD_DIM * (hh + 1))

    @pl.when(qi == 0)
    def _prepare():
        nblk = seq // FOX_CUM_BLOCK
        tri = (_iota2((FOX_CUM_BLOCK, FOX_CUM_BLOCK), 0) >= _iota2((FOX_CUM_BLOCK, FOX_CUM_BLOCK), 1)).astype(BF16)
        lane = _iota2((FOX_CUM_BLOCK, LANES), 1)
        carry = jnp.zeros((1, LANES), F32)
        for blk in range(nblk):
            rows = pl.ds(blk * FOX_CUM_BLOCK, FOX_CUM_BLOCK)
            logf = _log_sigmoid(small_ref[rows, :] + bias_ref[...])
            c = _masked_sums(tri, logf, 3) + carry
            carry = c[FOX_CUM_BLOCK - 1:FOX_CUM_BLOCK, :]
            pieces = jnp.concatenate([*_split3(c), jnp.ones_like(c)], axis=-1).astype(BF16)
            k_aug = _dot(pieces, selk_ref[...]).astype(BF16)
            qaug_ref[rows, :] = _dot(pieces, selq_ref[...]).astype(BF16)
            for half in range(halves):
                lanes = slice(half * LANES, (half + 1) * LANES)
                kn = _rms_heads64(k_ref[rows, lanes].astype(F32), kn_ref[...]).astype(BF16)
                v_half = v_ref[rows, lanes]
                for hh in range(2):
                    head = 2 * half + hh
                    own = own_lanes(lane, hh)
                    va_ref[head, rows, :] = jnp.where(own, v_half, jnp.ones_like(v_half))
                    ka_ref[head, rows, :] = jnp.where(own, kn, k_aug[:, lanes])

    scale = HEAD_DIM ** -0.5
    qrows = pl.ds(pl.multiple_of(qi * tq, tq), tq)
    lane = _iota2((tq, LANES), 1)
    lo_half = lane < HEAD_DIM
    causal = _iota2((tq, tq), 0) >= _iota2((tq, tq), 1)
    qas = []
    for half in range(halves):
        lanes = slice(half * LANES, (half + 1) * LANES)
        qn = (_rms_heads64(q_ref[:, lanes].astype(F32), qn_ref[...]) * scale).astype(BF16)
        for hh in range(2):
            qas.append(jnp.where(own_lanes(lane, hh), qn, qaug_ref[qrows, lanes]))

    heads = range(N_HEADS)

    def key_rows(j):
        return pl.ds(pl.multiple_of(j * tq, tq), tq)

    def logits(j):
        return [_dot_nt(qas[head], ka_ref[head, key_rows(j), :]) for head in heads]

    def consume(j, s, m_prev, acc):
        m_new = _each(lambda m, x: jnp.maximum(m, jnp.max(x, axis=-1, keepdims=True)), m_prev, s)
        p = _each(lambda x, m: jnp.exp(x - m).astype(BF16), s, m_new)
        pv = [_dot(p[head], va_ref[head, key_rows(j), :]) for head in heads]
        acc = _each(lambda m0, m1, a, x: jnp.exp(m0 - m1) * a + x, m_prev, m_new, acc, pv)
        return m_new, acc

    def body(j, carry):
        return consume(j, logits(j), *carry)

    init = ([jnp.full((tq, 1), MASK_VALUE, F32) for _ in heads], [jnp.zeros((tq, LANES), F32) for _ in heads])
    m_prev, acc = lax.fori_loop(0, qi, body, init)
    _, acc = consume(qi, [jnp.where(causal, x, MASK_VALUE) for x in logits(qi)], m_prev, acc)
    for half in range(halves):
        acc_lo, acc_hi = acc[2 * half], acc[2 * half + 1]
        o = jnp.where(lo_half, acc_lo / pltpu.roll(acc_lo, HEAD_DIM, 1), acc_hi / pltpu.roll(acc_hi, HEAD_DIM, 1))
        o_ref[:, half * LANES:(half + 1) * LANES] = _rms_heads64(o, on_ref[...]).astype(o_ref.dtype)


def _fox(fox3, small3, bias, qn, kn, on):
    b, seq, _ = fox3.shape
    tq = FOX_TILE
    kern = functools.partial(_fox_kernel, seq=seq)
    selq, selk = _fox_selectors()
    vec = pl.BlockSpec((1, LANES), lambda bi, qi: (0, 0))
    return pl.pallas_call(
        kern,
        grid=(b, seq // tq),
        in_specs=[
            pl.BlockSpec((None, tq, GROUP_WIDTH), lambda bi, qi: (bi, qi, 0)),
            pl.BlockSpec((None, seq, GROUP_WIDTH), lambda bi, qi: (bi, 0, 1)),
            pl.BlockSpec((None, seq, GROUP_WIDTH), lambda bi, qi: (bi, 0, 2)),
            pl.BlockSpec((None, seq, LANES), lambda bi, qi: (bi, 0, 0)),
            vec, vec, vec, vec,
            _resident(selq.shape), _resident(selk.shape),
        ],
        out_specs=pl.BlockSpec((None, tq, GROUP_WIDTH), lambda bi, qi: (bi, qi, 0)),
        out_shape=jax.ShapeDtypeStruct((b, seq, GROUP_WIDTH), BF16),
        scratch_shapes=[pltpu.VMEM((N_HEADS, seq, LANES), BF16), pltpu.VMEM((seq, GROUP_WIDTH), BF16),
                        pltpu.VMEM((N_HEADS, seq, LANES), BF16)],
        compiler_params=_cparams(("parallel", "arbitrary")),
        name="fox",
    )(fox3, fox3, fox3, small3, bias, qn, kn, on, selq, selk)


def _ssd_kernel(ssd_ref, small_ref, cw_ref, cb_ref, dtb_ref, alog_ref, dskip_ref, gn_ref, o_ref,
                st_ref, *, seq):
    c_len = SSD_CHUNK
    assert c_len == LANES
    n_batch = ssd_ref.shape[0]
    st_ref[...] = jnp.zeros_like(st_ref)

    a_lane = -jnp.exp(alog_ref[...])
    tri = (_iota2((c_len, c_len), 0) >= _iota2((c_len, c_len), 1))
    tri16 = tri.astype(BF16)
    sel16 = (_iota2((8, LANES), 0) + SMALL_DT == _iota2((8, LANES), 1)).astype(BF16)
    lane = _iota2((c_len, LANES), 1)
    lo = lane < HEAD_DIM

    def chunk(n, _):
        for gb in range(n_batch):
            chunk_one(n, gb)
        return 0

    def chunk_one(n, gb):
        rows = pl.ds(pl.multiple_of(n * c_len, c_len), c_len)
        halo_rows = pl.ds(pl.multiple_of(jnp.maximum(n * c_len - HALO, 0), HALO), HALO)
        halo = jnp.where(n > 0, ssd_ref[gb, halo_rows, GROUP_WIDTH:].astype(F32), 0.0)
        cur = ssd_ref[gb, rows, GROUP_WIDTH:].astype(F32)
        ext = jnp.concatenate([halo, cur], axis=0)
        conv = cb_ref[...] + cw_ref[SSD_CONV - 1:SSD_CONV, :] * cur
        for shift in range(1, SSD_CONV):
            conv = conv + cw_ref[SSD_CONV - 1 - shift:SSD_CONV - shift, :] * pltpu.roll(ext, shift, 0)[HALO:, :]
        xc = _silu(conv)
        xs = xc[:, 0:GROUP_WIDTH]
        bm = xc[:, GROUP_WIDTH:GROUP_WIDTH + LANES]
        cm = xc[:, GROUP_WIDTH + LANES:]
        dt = _softplus(small_ref[gb, rows, :] + dtb_ref[...])
        a_cs = _masked_sums(tri16, dt * a_lane, 3)
        a_row = _masked_sums(sel16, a_cs, 3, form="mask_xt")
        cm16 = cm.astype(BF16)
        y_halves = []
        for g in range(2):
            in_group = (lane >= SSD_STATE * g) & (lane < SSD_STATE * (g + 1))
            b_g = jnp.where(in_group, bm, 0.0).astype(BF16)
            scores = _dot_nt(cm16, b_g)
            heads = (2 * g, 2 * g + 1)
            a_col = [jnp.broadcast_to(_lane_col(a_cs, SMALL_DT + h), (c_len, LANES)) for h in heads]
            dt_col = [_lane_col(dt, SMALL_DT + h) for h in heads]
            a_own = jnp.where(lo, a_col[0], a_col[1])
            a_last = a_own[c_len - 1:c_len, :]
            xdt = xs[:, g * LANES:(g + 1) * LANES] * jnp.where(lo, dt_col[0], dt_col[1])
            xdt16 = xdt.astype(BF16)
            y_intra = []
            for index, h in enumerate(heads):
                seg = a_col[index] - a_row[h:h + 1, :]
                decay = jnp.where(tri, jnp.exp(jnp.where(tri, seg, 0.0)), 0.0)
                y_intra.append(_dot((scores * decay).astype(BF16), xdt16))
            y = jnp.where(lo, y_intra[0], y_intra[1])
            state = st_ref[gb, g]
            y = y + _dot(cm16, state.astype(BF16)) * jnp.exp(a_own)
            d_state = _dot_tn(b_g, (xdt * jnp.exp(a_last - a_own)).astype(BF16))
            st_ref[gb, g] = state * jnp.exp(a_last) + d_state
            y_halves.append(y)
        y = jnp.concatenate(y_halves, axis=-1) + xs * dskip_ref[...]
        y = y * _silu(ssd_ref[gb, rows, 0:GROUP_WIDTH].astype(F32))
        out = jnp.concatenate([_rms(y[:, i * LANES:(i + 1) * LANES], gn_ref[:, i * LANES:(i + 1) * LANES])
                               for i in range(2)], axis=-1)
        o_ref[gb, rows, :] = out.astype(o_ref.dtype)

    lax.fori_loop(0, seq // c_len, chunk, 0)


def _batch_spec(n_batch, seq, width):
    return pl.BlockSpec((n_batch, seq, width), lambda bi: (bi, 0, 0))


def _ssd(ssd3, small3, cw, cb, dtb, alog, dskip, gn):
    b, seq, width = ssd3.shape
    conv_dim = GROUP_WIDTH + 2 * LANES
    kern = functools.partial(_ssd_kernel, seq=seq)
    nb = min(SSD_BATCH, b)

    def vec(n, rows=1):
        return pl.BlockSpec((rows, n), lambda bi: (0, 0))

    return pl.pallas_call(
        kern,
        grid=(b // nb,),
        in_specs=[
            _batch_spec(nb, seq, width), _batch_spec(nb, seq, LANES),
            vec(conv_dim, SSD_CONV), vec(conv_dim), vec(LANES), vec(LANES), vec(GROUP_WIDTH), vec(GROUP_WIDTH),
        ],
        out_specs=_batch_spec(nb, seq, GROUP_WIDTH),
        out_shape=jax.ShapeDtypeStruct((b, seq, GROUP_WIDTH), BF16),
        scratch_shapes=[pltpu.VMEM((nb, 2, LANES, LANES), F32)],
        compiler_params=_cparams(("parallel",)),
        name="ssd",
    )(ssd3, small3, cw, cb, dtb, alog, dskip, gn)


def _gla_levels(c_len, sub):
    sizes = []
    size = sub
    while size <= c_len:
        sizes.append(size)
        size *= 2
    return sizes


def _gla_consts(c_len, dk, dv, sub):
    dkh = dk // N_HEADS
    dvh = dv // N_HEADS
    sizes = _gla_levels(c_len, sub)
    row = _iota2((c_len, c_len), 0)
    col = _iota2((c_len, c_len), 1)
    sum_masks = []
    for size in sizes:
        same = (row // size) == (col // size)
        sum_masks.append((same & (col <= row)).astype(BF16))
        sum_masks.append(same.astype(BF16))
    arow = _iota2((c_len, N_HEADS * c_len), 0)
    acol = _iota2((c_len, N_HEADS * c_len), 1) % c_len
    keep = [((arow // sub) == (acol // sub)) & (acol <= arow)]
    for level in range(1, len(sizes)):
        size, half = sizes[level], sizes[level - 1]
        keep.append(((arow // size) == (acol // size)) & ((arow % size) >= half) & ((acol % size) < half))
    klane = _iota2((c_len, dk), 1) // dkh
    vlane = _iota2((c_len, dv), 1) // dvh
    diag = (_iota2((dv, dk), 0) // dvh) == (_iota2((dv, dk), 1) // dkh)
    return dict(sizes=sizes, sum_mask=jnp.concatenate(sum_masks, axis=0), keep=keep,
                klane=klane, vlane=vlane, diag=diag)


def _each(fn, *lists):
    return [fn(*args) for args in zip(*lists)]


def _gla_chunk(q, k, v, logf, state_t, consts):
    c_len = q[0].shape[0]
    sizes = consts["sizes"]
    n_levels = len(sizes)
    sums = _each(lambda x: _masked_sums(consts["sum_mask"], x, 2), logf)
    cs = [[s[(2 * i) * c_len:(2 * i + 1) * c_len] for i in range(n_levels)] for s in sums]
    tot = [[s[(2 * i + 1) * c_len:(2 * i + 2) * c_len] for i in range(n_levels)] for s in sums]

    def stack_heads(x, lane_head):
        return jnp.concatenate([jnp.where(lane_head == h, x, 0.0) for h in range(N_HEADS)], axis=0).astype(BF16)

    q0 = _each(lambda x, c: (x * jnp.exp(c[0])).astype(BF16), q, cs)
    k0 = _each(lambda x, c: stack_heads(x * jnp.exp(-c[0]), consts["klane"]), k, cs)
    att = _each(lambda a, b: jnp.where(consts["keep"][0], _dot_nt(a, b), 0.0), q0, k0)
    for level in range(1, n_levels):
        q_l = q0 if level == 1 else _each(lambda x, c: (x * jnp.exp(c[level - 1])).astype(BF16), q, cs)
        k_l = _each(lambda x, c, t: stack_heads(x * jnp.exp(t[level - 1] - c[level - 1]), consts["klane"]), k, cs, tot)
        att = _each(lambda a, b, prev: jnp.where(consts["keep"][level], _dot_nt(a, b), prev), q_l, k_l, att)
    v_stack = _each(lambda x: stack_heads(x, consts["vlane"]), v)
    o = _each(lambda a, b: _dot(a.astype(BF16), b), att, v_stack)

    q_s = _each(lambda x, c: (x * jnp.exp(c[-1])).astype(BF16), q, cs)
    k_s = _each(lambda x, c, t: (x * jnp.exp(t[-1] - c[-1])).astype(BF16), k, cs, tot)
    o = _each(lambda acc, a, s: acc + _dot_nt(a, s.astype(BF16)), o, q_s, state_t)
    d_state = _each(lambda x, y: jnp.where(consts["diag"], _dot_tn(x.astype(BF16), y), 0.0), v, k_s)
    new_state = _each(lambda s, t, d: s * jnp.exp(t[-1][0:1, :]) + d, state_t, tot, d_state)
    return o, new_state


def _gla_kernel(gla_ref, small_ref, wg_ref, bg_ref, on_ref, o_ref, st_ref, *, seq):
    c_len = GLA_CHUNK
    dk = GLA_KEY_DIM
    st_ref[...] = jnp.zeros_like(st_ref)
    scale = (dk // N_HEADS) ** -0.5
    consts = _gla_consts(c_len, dk, GROUP_WIDTH, c_len)

    def chunk(n, _):
        rows = pl.ds(pl.multiple_of(n * c_len, c_len), c_len)
        batch = range(gla_ref.shape[0])
        gate = [_dot(small_ref[gb, rows, :].astype(BF16), wg_ref[...]) + bg_ref[...] for gb in batch]
        logf = [_log_sigmoid(x) * (1.0 / GLA_TAU) for x in gate]
        q = [gla_ref[gb, rows, 0:dk].astype(F32) * scale for gb in batch]
        k = [gla_ref[gb, rows, dk:2 * dk].astype(F32) for gb in batch]
        v = [gla_ref[gb, rows, 2 * dk:2 * dk + GROUP_WIDTH].astype(F32) for gb in batch]
        o, new_state = _gla_chunk(q, k, v, logf, [st_ref[gb] for gb in batch], consts)
        for gb in batch:
            st_ref[gb] = new_state[gb]
            r = gla_ref[gb, rows, 2 * dk + GROUP_WIDTH:].astype(F32)
            o_ref[gb, rows, :] = (_rms_heads64_wide(o[gb], on_ref[...]) * _silu(r)).astype(o_ref.dtype)
        return 0

    lax.fori_loop(0, seq // c_len, chunk, 0)


def _gla(gla3, small3, wg, bg, on):
    b, seq, width = gla3.shape
    kern = functools.partial(_gla_kernel, seq=seq)
    nb = min(GLA_BATCH, b)
    return pl.pallas_call(
        kern,
        grid=(b // nb,),
        in_specs=[
            _batch_spec(nb, seq, width), _batch_spec(nb, seq, LANES),
            pl.BlockSpec((LANES, GLA_KEY_DIM), lambda bi: (0, 0)),
            pl.BlockSpec((1, GLA_KEY_DIM), lambda bi: (0, 0)),
            pl.BlockSpec((1, GROUP_WIDTH), lambda bi: (0, 0)),
        ],
        out_specs=_batch_spec(nb, seq, GROUP_WIDTH),
        out_shape=jax.ShapeDtypeStruct((b, seq, GROUP_WIDTH), BF16),
        scratch_shapes=[pltpu.VMEM((nb, GROUP_WIDTH, GLA_KEY_DIM), F32)],
        compiler_params=_cparams(("parallel",)),
        name="gla",
    )(gla3, small3, wg, bg, on)


def _hgrn_kernel(hg_ref, lbl_ref, on_ref, o_ref, st_ref, *, seq, layer):
    c_len = HGRN_CHUNK
    dk = HGRN_FDIM
    st_ref[...] = jnp.zeros_like(st_ref)
    logits = lbl_ref[...]
    e = jnp.exp(logits - jnp.max(logits, axis=0, keepdims=True))
    soft = e / jnp.sum(e, axis=0, keepdims=True)
    lb = jnp.zeros((1, dk), F32)
    for i in range(1, layer + 1):
        lb = lb + soft[i:i + 1, :]
    lb = jnp.clip(lb, 0.0, 1.0 - 1e-6)
    consts = _gla_consts(c_len, dk, GROUP_WIDTH, HGRN_SUB)

    def chunk(n, _):
        rows = pl.ds(pl.multiple_of(n * c_len, c_len), c_len)
        batch = range(hg_ref.shape[0])
        sig = [_sigmoid_pair(hg_ref[gb, rows, dk:2 * dk].astype(F32)) for gb in batch]
        logf = [jnp.log(lb + (1.0 - lb) * s[0]) for s in sig]
        k = [(1.0 - lb) * s[1] for s in sig]
        q = [hg_ref[gb, rows, 0:dk].astype(F32) for gb in batch]
        v = [hg_ref[gb, rows, 2 * dk:2 * dk + GROUP_WIDTH].astype(F32) for gb in batch]
        o, new_state = _gla_chunk(q, k, v, logf, [st_ref[gb] for gb in batch], consts)
        for gb in batch:
            st_ref[gb] = new_state[gb]
            g = hg_ref[gb, rows, 2 * dk + GROUP_WIDTH:].astype(F32)
            o_ref[gb, rows, :] = (_rms_heads64_wide(o[gb], on_ref[...]) * _silu(g)).astype(o_ref.dtype)
        return 0

    lax.fori_loop(0, seq // c_len, chunk, 0)


def _hgrn(hg3, lb_logits, on, layer):
    b, seq, width = hg3.shape
    depth = lb_logits.shape[0]
    kern = functools.partial(_hgrn_kernel, seq=seq, layer=layer)
    nb = min(HGRN_BATCH, b)
    return pl.pallas_call(
        kern,
        grid=(b // nb,),
        in_specs=[
            _batch_spec(nb, seq, width),
            pl.BlockSpec((depth, HGRN_FDIM), lambda bi: (0, 0)),
            pl.BlockSpec((1, GROUP_WIDTH), lambda bi: (0, 0)),
        ],
        out_specs=_batch_spec(nb, seq, GROUP_WIDTH),
        out_shape=jax.ShapeDtypeStruct((b, seq, GROUP_WIDTH), BF16),
        scratch_shapes=[pltpu.VMEM((nb, GROUP_WIDTH, HGRN_FDIM), F32)],
        compiler_params=_cparams(("parallel",)),
        name="hgrn2",
    )(hg3, lb_logits, on)


def _kv_kernel(mem_ref, mn_ref, w_ref, kn_ref, k_ref, v_ref):
    mem_n = _rms(mem_ref[...], mn_ref[...]).astype(BF16)
    kv = _dot(mem_n, w_ref[...])
    for h in range(XA_HEADS):
        cols = slice(h * XA_HEAD_DIM, (h + 1) * XA_HEAD_DIM)
        k_ref[:, cols] = _rms(kv[:, cols], kn_ref[...]).astype(k_ref.dtype)
    v_ref[...] = kv[:, D_MODEL:].astype(v_ref.dtype)


def _kv_proj(mem2d, mem_norm, w_kv, kn, tm):
    m = mem2d.shape[0]
    depth = w_kv.shape[0]
    out = jax.ShapeDtypeStruct((depth, m, D_MODEL), BF16)
    return pl.pallas_call(
        _kv_kernel,
        grid=(depth, m // tm),
        in_specs=[
            pl.BlockSpec((tm, D_MODEL), lambda l, i: (i, 0)),
            pl.BlockSpec((1, D_MODEL), lambda l, i: (0, 0)),
            pl.BlockSpec((None, D_MODEL, 2 * D_MODEL), lambda l, i: (l, 0, 0)),
            pl.BlockSpec((None, 1, XA_HEAD_DIM), lambda l, i: (l, 0, 0)),
        ],
        out_specs=[pl.BlockSpec((None, tm, D_MODEL), lambda l, i: (l, i, 0))] * 2,
        out_shape=[out, out],
        compiler_params=_cparams(("parallel", "parallel")),
        name="mem_kv",
    )(mem2d, mem_norm, w_kv, kn)


def _mix_xattn_kernel(x_ref, yf_ref, ys_ref, yg_ref, yh_ref, wout_ref, ln_ref, wq_ref, qn_ref,
                      k_ref, v_ref, wo_ref, o_ref):
    mixed = jnp.concatenate([yf_ref[...], ys_ref[...], yg_ref[...], yh_ref[...]], axis=-1)
    x = x_ref[...] + _dot(mixed, wout_ref[...])
    h = _rms(x, ln_ref[...]).astype(BF16)
    q = _dot(h, wq_ref[...])
    scale = XA_HEAD_DIM ** -0.5
    outs = []
    for hd in range(XA_HEADS):
        cols = slice(hd * XA_HEAD_DIM, (hd + 1) * XA_HEAD_DIM)
        qh = (_rms(q[:, cols], qn_ref[...]) * scale).astype(BF16)
        logits = _dot_nt(qh, k_ref[:, cols])
        p = jnp.exp(logits - jnp.max(logits, axis=-1, keepdims=True))
        p = p / jnp.sum(p, axis=-1, keepdims=True)
        outs.append(_dot(p.astype(BF16), v_ref[:, cols]).astype(BF16))
    o = jnp.concatenate(outs, axis=-1)
    o_ref[...] = x + _dot(o, wo_ref[...])


def _mix_xattn(x2d, ys, w_out, ln, wq, qn, k3, v3, wo, tm, seq):
    m = x2d.shape[0]
    n_mem = k3.shape[1]
    tiles_per_seq = seq // tm
    row = lambda i: (i, 0)
    const = lambda i: (0, 0)
    mem_spec = pl.BlockSpec((None, n_mem, D_MODEL), lambda i: (i // tiles_per_seq, 0, 0))
    return pl.pallas_call(
        _mix_xattn_kernel,
        grid=(m // tm,),
        in_specs=[
            pl.BlockSpec((tm, D_MODEL), row),
            *[pl.BlockSpec((tm, GROUP_WIDTH), row)] * 4,
            _resident((D_MODEL, D_MODEL)),
            pl.BlockSpec((1, D_MODEL), const),
            _resident((D_MODEL, D_MODEL)),
            pl.BlockSpec((1, XA_HEAD_DIM), const),
            mem_spec, mem_spec,
            _resident((D_MODEL, D_MODEL)),
        ],
        out_specs=pl.BlockSpec((tm, D_MODEL), row),
        out_shape=jax.ShapeDtypeStruct((m, D_MODEL), F32),
        compiler_params=_cparams(("parallel",)),
        name="mix_xattn",
    )(x2d, *ys, w_out, ln, wq, qn, k3, v3, wo)


def _ffn_kernel(x_ref, xp_ref, ln_ref, wup_ref, cw_ref, cb_ref, wdown_ref, o_ref, act_ref, *, tiles_per_seq):
    i = pl.program_id(0)
    x = x_ref[...]
    first = (i % tiles_per_seq) == 0
    h = _rms(x, ln_ref[...]).astype(BF16)
    hp = jnp.where(first, 0.0, _rms(xp_ref[...], ln_ref[...])).astype(BF16)
    h_ext = jnp.concatenate([hp, h], axis=0)
    for c in range(D_FF // FF_TILE):
        cols = slice(c * FF_TILE, (c + 1) * FF_TILE)
        vcols = slice(D_FF + c * FF_TILE, D_FF + (c + 1) * FF_TILE)
        gate = _dot(h_ext, wup_ref[:, cols])
        conv = cb_ref[:, cols] + cw_ref[FFN_CONV - 1:FFN_CONV, cols] * gate[HALO:, :]
        for shift in range(1, FFN_CONV):
            conv = conv + (cw_ref[FFN_CONV - 1 - shift:FFN_CONV - shift, cols]
                           * pltpu.roll(gate, shift, 0)[HALO:, :])
        val = _dot(h, wup_ref[:, vcols])
        act_ref[:, cols] = (_silu(conv) * val).astype(BF16)
    o_ref[...] = x + _dot(act_ref[...], wdown_ref[...])


def _ffn(x2d, ln, w_up, cw, cb, w_down, tm, seq):
    m = x2d.shape[0]
    kern = functools.partial(_ffn_kernel, tiles_per_seq=seq // tm)
    const = lambda i: (0, 0)
    halo_blocks = tm // HALO
    return pl.pallas_call(
        kern,
        grid=(m // tm,),
        in_specs=[
            pl.BlockSpec((tm, D_MODEL), lambda i: (i, 0)),
            pl.BlockSpec((HALO, D_MODEL), lambda i: (jnp.maximum(i * halo_blocks - 1, 0), 0)),
            pl.BlockSpec((1, D_MODEL), const),
            _resident((D_MODEL, 2 * D_FF)),
            pl.BlockSpec((FFN_CONV, D_FF), const),
            pl.BlockSpec((1, D_FF), const),
            _resident((D_FF, D_MODEL)),
        ],
        out_specs=pl.BlockSpec((tm, D_MODEL), lambda i: (i, 0)),
        out_shape=jax.ShapeDtypeStruct((m, D_MODEL), F32),
        scratch_shapes=[pltpu.VMEM((tm, D_FF), BF16)],
        compiler_params=_cparams(("parallel",)),
        name="conv_glu_ffn",
    )(x2d, x2d, ln, w_up, cw, cb, w_down)


def _pad_lanes(vec, offset, total=LANES):
    vec = vec.astype(F32).reshape(1, -1)
    return jnp.pad(vec, ((0, 0), (offset, total - offset - vec.shape[1])))


def _arrange_w_in(w):
    fq, fk, fv, ff, sz, sxbc, sdt, gq, gk, gv, ga, gr, hq, hf, hi, hg = jnp.split(
        w, [256, 512, 768, 772, 1028, 1540, 1544, 1672, 1800, 2056, 2072, 2328, 2584, 2840, 3096], axis=1)
    small = jnp.concatenate([ff, sdt, ga], axis=1)
    small = jnp.pad(small, ((0, 0), (0, LANES - small.shape[1])))
    return jnp.concatenate([fq, fk, fv, sz, sxbc, gq, gk, gv, gr, hq, hf, hi, hg, small], axis=1).astype(BF16)


def kernel(x, mem, ln_mix, w_in, w_out, fox_f_bias, fox_qn, fox_kn, fox_on, ssd_conv_w, ssd_conv_b, ssd_dt_bias,
           ssd_a_log, ssd_d, ssd_norm, gla_w_g2, gla_b_g2, gla_norm, hgrn_lb_logits, hgrn_norm, ln_xattn, mem_norm,
           xa_wq, xa_wkv, xa_wo, xa_qn, xa_kn, ln_ffn, ffn_w_up, ffn_conv_w, ffn_conv_b, ffn_w_down):
    b, seq, d = x.shape
    depth = w_in.shape[0]
    n_mem = mem.shape[1]
    m = b * seq
    tm = min(ROW_TILE, seq)
    x2d = x.reshape(m, d)

    k_all, v_all = _kv_proj(mem.reshape(b * n_mem, d), mem_norm.reshape(1, d), xa_wkv.astype(BF16),
                            xa_kn.reshape(depth, 1, XA_HEAD_DIM), min(512, b * n_mem))
    for l in range(depth):
        fox, ssd, gla, hgrn, small = _in_proj(x2d, ln_mix[l].reshape(1, d), _arrange_w_in(w_in[l]), tm)
        small3 = small.reshape(b, seq, LANES)
        tile2 = lambda v: jnp.tile(v.astype(F32).reshape(1, -1), (1, LANES // HEAD_DIM))
        tile4 = lambda v: jnp.tile(v.astype(F32).reshape(1, -1), (1, GROUP_WIDTH // HEAD_DIM))
        y_fox = _fox(fox.reshape(b, seq, -1), small3, _pad_lanes(fox_f_bias[l], SMALL_FF),
                     tile2(fox_qn[l]), tile2(fox_kn[l]), tile2(fox_on[l]))
        y_ssd = _ssd(ssd.reshape(b, seq, -1), small3, ssd_conv_w[l], ssd_conv_b[l].reshape(1, -1),
                     _pad_lanes(ssd_dt_bias[l], SMALL_DT), _pad_lanes(ssd_a_log[l], SMALL_DT),
                     jnp.repeat(ssd_d[l].astype(F32), HEAD_DIM).reshape(1, -1), ssd_norm[l].reshape(1, -1))
        wg = jnp.pad(gla_w_g2[l], ((SMALL_GA, LANES - SMALL_GA - GLA_GATE_RANK), (0, 0))).astype(BF16)
        y_gla = _gla(gla.reshape(b, seq, -1), small3, wg, gla_b_g2[l].reshape(1, -1), tile4(gla_norm[l]))
        y_hgrn = _hgrn(hgrn.reshape(b, seq, -1), hgrn_lb_logits, tile4(hgrn_norm[l]), l)
        ys = [y.reshape(m, GROUP_WIDTH) for y in (y_fox, y_ssd, y_gla, y_hgrn)]
        x2d = _mix_xattn(x2d, ys, w_out[l].astype(BF16), ln_xattn[l].reshape(1, d), xa_wq[l].astype(BF16),
                         xa_qn[l].reshape(1, -1), k_all[l].reshape(b, n_mem, d), v_all[l].reshape(b, n_mem, d),
                         xa_wo[l].astype(BF16), tm, seq)
        x2d = _ffn(x2d, ln_ffn[l].reshape(1, d), ffn_w_up[l].astype(BF16), ffn_conv_w[l],
                   ffn_conv_b[l].reshape(1, -1), ffn_w_down[l].astype(BF16), min(FFN_ROWS, seq), seq)
    return x2d.reshape(b, seq, d)
```

```python
import functools

import jax
import jax.numpy as jnp
import numpy as np
from jax import lax
from jax.experimental import pallas as pl
from jax.experimental.pallas import tpu as pltpu

F32 = jnp.float32
BF16 = jnp.bfloat16

EPS = 1e-6
MASK_VALUE = -1e30
D_MODEL = 1024
GROUP_WIDTH = 256
HEAD_DIM = 64
LANES = 128
N_HEADS = 4
SSD_STATE = 64
SSD_CONV = 4
SSD_CHUNK = 128
GLA_KEY_DIM = 128
GLA_GATE_RANK = 16
GLA_TAU = 16.0
GLA_CHUNK = 64
HGRN_FDIM = 256
HGRN_CHUNK = 64
HGRN_SUB = 16
XA_HEADS = 4
XA_HEAD_DIM = 256
D_FF = 2816
FFN_CONV = 3
FF_TILE = 256
FFN_ROWS = 1024
ROW_TILE = 1024
HALO = 16
SSD_BATCH = 4
GLA_BATCH = 4
HGRN_BATCH = 4

SMALL_FF = 0
SMALL_DT = 4
SMALL_GA = 8

VMEM_LIMIT = 56 * 1024 * 1024

HIGHEST = lax.Precision.HIGHEST


def _cparams(sem):
    return pltpu.CompilerParams(dimension_semantics=sem, vmem_limit_bytes=VMEM_LIMIT)


def _resident(shape):
    return pl.BlockSpec(shape, lambda *_: (0,) * len(shape), pipeline_mode=pl.Buffered(1))


def _dot(a, b, precision=None):
    return jnp.dot(a, b, preferred_element_type=F32, precision=precision)


def _dot_nt(a, b, precision=None):
    return lax.dot_general(a, b, (((1,), (1,)), ((), ())), preferred_element_type=F32, precision=precision)


def _dot_tn(a, b, precision=None):
    return lax.dot_general(a, b, (((0,), (0,)), ((), ())), preferred_element_type=F32, precision=precision)


def _rms(x, w):
    return x * lax.rsqrt(jnp.mean(x * x, axis=-1, keepdims=True) + EPS) * w


def _rms_heads64(x, w):
    lane = lax.broadcasted_iota(jnp.int32, x.shape, 1)
    lo = lane < HEAD_DIM
    sq = x * x
    s_lo = jnp.sum(jnp.where(lo, sq, 0.0), axis=-1, keepdims=True)
    s_hi = jnp.sum(jnp.where(lo, 0.0, sq), axis=-1, keepdims=True)
    ms = jnp.where(lo, s_lo, s_hi) * (1.0 / HEAD_DIM)
    return x * lax.rsqrt(ms + EPS) * w


def _rms_heads64_wide(x, w):
    return jnp.concatenate(
        [_rms_heads64(x[:, i * LANES:(i + 1) * LANES], w[:, i * LANES:(i + 1) * LANES]) for i in range(2)], axis=-1)


def _sigmoid_pair(z):
    e = jnp.exp(-jnp.abs(z))
    big = 1.0 / (1.0 + e)
    small = e * big
    pos = z >= 0
    return jnp.where(pos, big, small), jnp.where(pos, small, big)


def _silu(z):
    return z * (1.0 / (1.0 + jnp.exp(-z)))


def _log_sigmoid(z):
    return jnp.minimum(z, 0.0) - jnp.log1p(jnp.exp(-jnp.abs(z)))


def _softplus(z):
    return jnp.maximum(z, 0.0) + jnp.log1p(jnp.exp(-jnp.abs(z)))


def _iota2(shape, axis):
    return lax.broadcasted_iota(jnp.int32, shape, axis)


def _lane_col(x, lane_index):
    return x[:, lane_index:lane_index + 1]


def _masked_sums(mask16, x, pieces, form="mask_x"):
    total = None
    rest = x
    for index in range(pieces):
        part = rest.astype(BF16)
        if form == "mask_x":
            term = _dot(mask16, part)
        elif form == "mask_xt":
            term = _dot_nt(mask16, part)
        else:
            term = _dot(part, mask16)
        total = term if total is None else total + term
        if index + 1 < pieces:
            rest = rest - part.astype(F32)
    return total


IN_GROUPS = (768, 768, 768, 1024, 128)
D_IN_PAD = sum(IN_GROUPS)


def _in_proj_kernel(x_ref, ln_ref, w_ref, fqn_ref, fkn_ref, fox_ref, ssd_ref, gla_ref, hgrn_ref, small_ref):
    h = _rms(x_ref[...], ln_ref[...]).astype(BF16)
    outs = (fox_ref, ssd_ref, gla_ref, hgrn_ref, small_ref)
    start = 0
    for width, out in zip(IN_GROUPS, outs):
        proj = _dot(h, w_ref[:, start:start + width])
        if out is fox_ref:
            gw = GROUP_WIDTH
            out[:, 0:gw] = (_rms_heads64_wide(proj[:, 0:gw], fqn_ref[...]) * HEAD_DIM ** -0.5).astype(out.dtype)
            out[:, gw:2 * gw] = _rms_heads64_wide(proj[:, gw:2 * gw], fkn_ref[...]).astype(out.dtype)
            out[:, 2 * gw:] = proj[:, 2 * gw:].astype(out.dtype)
        else:
            out[...] = proj.astype(out.dtype)
        start += width


def _in_proj(x2d, ln, w_r, fox_qn, fox_kn, tm):
    m = x2d.shape[0]
    out_shape = [jax.ShapeDtypeStruct((m, n), BF16) for n in IN_GROUPS[:4]]
    out_shape.append(jax.ShapeDtypeStruct((m, IN_GROUPS[4]), F32))
    vec = pl.BlockSpec((1, GROUP_WIDTH), lambda i: (0, 0))
    return pl.pallas_call(
        _in_proj_kernel,
        grid=(m // tm,),
        in_specs=[
            pl.BlockSpec((tm, D_MODEL), lambda i: (i, 0)),
            pl.BlockSpec((1, D_MODEL), lambda i: (0, 0)),
            _resident((D_MODEL, D_IN_PAD)),
            vec, vec,
        ],
        out_specs=[pl.BlockSpec((tm, n), lambda i: (i, 0)) for n in IN_GROUPS],
        out_shape=out_shape,
        compiler_params=_cparams(("parallel",)),
        name="in_proj",
    )(x2d, ln, w_r, fox_qn, fox_kn)


FOX_TILE = 256
FOX_KEY_BLOCKS = 4
FOX_CUM_BLOCK = 256
N_SPLIT = 3


def _split3(c):
    hi = c.astype(BF16).astype(F32)
    r = c - hi
    mid = r.astype(BF16).astype(F32)
    lo = (r - mid).astype(BF16).astype(F32)
    return hi, mid, lo


def _fox_selectors():
    selq = np.zeros(((N_SPLIT + 1) * LANES, GROUP_WIDTH), np.float32)
    selk = np.zeros_like(selq)
    for h in range(N_HEADS):
        base = (h // 2) * LANES + HEAD_DIM * (1 - h % 2)
        for p in range(N_SPLIT):
            selq[p * LANES + SMALL_FF + h, base + p] = 1.0
            selq[N_SPLIT * LANES, base + N_SPLIT + p] = 1.0
            selk[N_SPLIT * LANES, base + p] = 1.0
            selk[p * LANES + SMALL_FF + h, base + N_SPLIT + p] = -1.0
    return jnp.asarray(selq, BF16), jnp.asarray(selk, BF16)


def _fox_kernel(q_ref, k_ref, v_ref, small_ref, bias_ref, selq_ref, selk_ref, o_ref,
                ka_ref, qaug_ref, va_ref, *, seq):
    qi = pl.program_id(1)
    tq = FOX_TILE
    halves = GROUP_WIDTH // LANES

    def own_lanes(lane, hh):
        return (lane >= HEAD_DIM * hh) & (lane < HEAD_DIM * (hh + 1))

    @pl.when(qi == 0)
    def _prepare():
        nblk = seq // FOX_CUM_BLOCK
        tri = (_iota2((FOX_CUM_BLOCK, FOX_CUM_BLOCK), 0) >= _iota2((FOX_CUM_BLOCK, FOX_CUM_BLOCK), 1)).astype(BF16)
        lane = _iota2((FOX_CUM_BLOCK, LANES), 1)
        carry = jnp.zeros((1, LANES), F32)
        for blk in range(nblk):
            rows = pl.ds(blk * FOX_CUM_BLOCK, FOX_CUM_BLOCK)
            logf = _log_sigmoid(small_ref[rows, :] + bias_ref[...])
            c = _masked_sums(tri, logf, 3) + carry
            carry = c[FOX_CUM_BLOCK - 1:FOX_CUM_BLOCK, :]
            pieces = jnp.concatenate([*_split3(c), jnp.ones_like(c)], axis=-1).astype(BF16)
            k_aug = _dot(pieces, selk_ref[...]).astype(BF16)
            qaug_ref[rows, :] = _dot(pieces, selq_ref[...]).astype(BF16)
            for half in range(halves):
                lanes = slice(half * LANES, (half + 1) * LANES)
                kn = k_ref[rows, lanes]
                v_half = v_ref[rows, lanes]
                for hh in range(2):
                    head = 2 * half + hh
                    own = own_lanes(lane, hh)
                    va_ref[head, rows, :] = jnp.where(own, v_half, jnp.ones_like(v_half))
                    ka_ref[head, rows, :] = jnp.where(own, kn, k_aug[:, lanes])

    qrows = pl.ds(pl.multiple_of(qi * tq, tq), tq)
    lane = _iota2((tq, LANES), 1)
    lo_half = lane < HEAD_DIM
    qas = []
    for half in range(halves):
        lanes = slice(half * LANES, (half + 1) * LANES)
        for hh in range(2):
            qas.append(jnp.where(own_lanes(lane, hh), q_ref[:, lanes], qaug_ref[qrows, lanes]))

    heads = range(N_HEADS)

    def consume(first_block, n_blocks, carry, diagonal_last):
        width = n_blocks * tq
        rows = pl.ds(pl.multiple_of(first_block * tq, tq), width)
        m_prev, acc = carry
        s = [_dot_nt(qas[head], ka_ref[head, rows, :]) for head in heads]
        if diagonal_last:
            visible = _iota2((tq, width), 0) + (n_blocks - 1) * tq >= _iota2((tq, width), 1)
            s = [jnp.where(visible, x, MASK_VALUE) for x in s]
        m_new = _each(lambda m, x: jnp.maximum(m, jnp.max(x, axis=-1, keepdims=True)), m_prev, s)
        p = _each(lambda x, m: jnp.exp(x - m).astype(BF16), s, m_new)
        pv = [_dot(p[head], va_ref[head, rows, :]) for head in heads]
        acc = _each(lambda m0, m1, a, x: jnp.exp(m0 - m1) * a + x, m_prev, m_new, acc, pv)
        return m_new, acc

    group = FOX_KEY_BLOCKS
    init = ([jnp.full((tq, 1), MASK_VALUE, F32) for _ in heads], [jnp.zeros((tq, LANES), F32) for _ in heads])
    carry = lax.fori_loop(0, qi // group, lambda g, c: consume(g * group, group, c, False), init)
    leftover = qi % group
    branches = [functools.partial(lambda c, n: consume(qi - n, n + 1, c, True), n=n) for n in range(group)]
    _, acc = lax.switch(leftover, branches, carry)
    for half in range(halves):
        acc_lo, acc_hi = acc[2 * half], acc[2 * half + 1]
        o = jnp.where(lo_half, acc_lo / pltpu.roll(acc_lo, HEAD_DIM, 1), acc_hi / pltpu.roll(acc_hi, HEAD_DIM, 1))
        o_ref[:, half * LANES:(half + 1) * LANES] = o.astype(o_ref.dtype)


def _fox(fox3, small3, bias):
    b, seq, _ = fox3.shape
    tq = FOX_TILE
    kern = functools.partial(_fox_kernel, seq=seq)
    selq, selk = _fox_selectors()
    vec = pl.BlockSpec((1, LANES), lambda bi, qi: (0, 0))
    return pl.pallas_call(
        kern,
        grid=(b, seq // tq),
        in_specs=[
            pl.BlockSpec((None, tq, GROUP_WIDTH), lambda bi, qi: (bi, qi, 0)),
            pl.BlockSpec((None, seq, GROUP_WIDTH), lambda bi, qi: (bi, 0, 1)),
            pl.BlockSpec((None, seq, GROUP_WIDTH), lambda bi, qi: (bi, 0, 2)),
            pl.BlockSpec((None, seq, LANES), lambda bi, qi: (bi, 0, 0)),
            vec,
            _resident(selq.shape), _resident(selk.shape),
        ],
        out_specs=pl.BlockSpec((None, tq, GROUP_WIDTH), lambda bi, qi: (bi, qi, 0)),
        out_shape=jax.ShapeDtypeStruct((b, seq, GROUP_WIDTH), BF16),
        scratch_shapes=[pltpu.VMEM((N_HEADS, seq, LANES), BF16), pltpu.VMEM((seq, GROUP_WIDTH), BF16),
                        pltpu.VMEM((N_HEADS, seq, LANES), BF16)],
        compiler_params=_cparams(("parallel", "arbitrary")),
        name="fox",
    )(fox3, fox3, fox3, small3, bias, selq, selk)


def _ssd_kernel(ssd_ref, small_ref, cw_ref, cb_ref, dtb_ref, alog_ref, dskip_ref, gn_ref, o_ref,
                st_ref, *, seq):
    c_len = SSD_CHUNK
    assert c_len == LANES
    n_batch = ssd_ref.shape[0]
    st_ref[...] = jnp.zeros_like(st_ref)

    a_lane = -jnp.exp(alog_ref[...])
    tri = (_iota2((c_len, c_len), 0) >= _iota2((c_len, c_len), 1))
    tri16 = tri.astype(BF16)
    sel16 = (_iota2((8, LANES), 0) + SMALL_DT == _iota2((8, LANES), 1)).astype(BF16)
    lane = _iota2((c_len, LANES), 1)
    lo = lane < HEAD_DIM

    def chunk(n, _):
        _round_robin([chunk_one(n, gb) for gb in range(n_batch)])
        return 0

    def chunk_one(n, gb):
        rows = pl.ds(pl.multiple_of(n * c_len, c_len), c_len)
        halo_rows = pl.ds(pl.multiple_of(jnp.maximum(n * c_len - HALO, 0), HALO), HALO)
        halo = jnp.where(n > 0, ssd_ref[gb, halo_rows, GROUP_WIDTH:].astype(F32), 0.0)
        cur = ssd_ref[gb, rows, GROUP_WIDTH:].astype(F32)
        ext = jnp.concatenate([halo, cur], axis=0)
        conv = cb_ref[...] + cw_ref[SSD_CONV - 1:SSD_CONV, :] * cur
        for shift in range(1, SSD_CONV):
            conv = conv + cw_ref[SSD_CONV - 1 - shift:SSD_CONV - shift, :] * pltpu.roll(ext, shift, 0)[HALO:, :]
        xc = _silu(conv)
        xs = xc[:, 0:GROUP_WIDTH]
        bm = xc[:, GROUP_WIDTH:GROUP_WIDTH + LANES]
        cm = xc[:, GROUP_WIDTH + LANES:]
        dt = _softplus(small_ref[gb, rows, :] + dtb_ref[...])
        a_cs = _masked_sums(tri16, dt * a_lane, 3)
        yield
        a_row = _masked_sums(sel16, a_cs, 3, form="mask_xt")
        cm16 = cm.astype(BF16)
        y_halves = []
        for g in range(2):
            in_group = (lane >= SSD_STATE * g) & (lane < SSD_STATE * (g + 1))
            b_g = jnp.where(in_group, bm, 0.0).astype(BF16)
            scores = _dot_nt(cm16, b_g)
            yield
            heads = (2 * g, 2 * g + 1)
            a_col = [jnp.broadcast_to(_lane_col(a_cs, SMALL_DT + h), (c_len, LANES)) for h in heads]
            dt_col = [_lane_col(dt, SMALL_DT + h) for h in heads]
            a_own = jnp.where(lo, a_col[0], a_col[1])
            a_last = a_own[c_len - 1:c_len, :]
            xdt = xs[:, g * LANES:(g + 1) * LANES] * jnp.where(lo, dt_col[0], dt_col[1])
            xdt16 = xdt.astype(BF16)
            y_intra = []
            for index, h in enumerate(heads):
                seg = a_col[index] - a_row[h:h + 1, :]
                decay = jnp.where(tri, jnp.exp(jnp.where(tri, seg, 0.0)), 0.0)
                y_intra.append(_dot((scores * decay).astype(BF16), xdt16))
            state = st_ref[gb, g]
            y_state = _dot(cm16, state.astype(BF16))
            d_state = _dot_tn(b_g, (xdt * jnp.exp(a_last - a_own)).astype(BF16))
            yield
            y_halves.append(jnp.where(lo, y_intra[0], y_intra[1]) + y_state * jnp.exp(a_own))
            st_ref[gb, g] = state * jnp.exp(a_last) + d_state
        y = jnp.concatenate(y_halves, axis=-1) + xs * dskip_ref[...]
        y = y * _silu(ssd_ref[gb, rows, 0:GROUP_WIDTH].astype(F32))
        out = jnp.concatenate([_rms(y[:, i * LANES:(i + 1) * LANES], gn_ref[:, i * LANES:(i + 1) * LANES])
                               for i in range(2)], axis=-1)
        o_ref[gb, rows, :] = out.astype(o_ref.dtype)

    lax.fori_loop(0, seq // c_len, chunk, 0)


def _batch_spec(n_batch, seq, width):
    return pl.BlockSpec((n_batch, seq, width), lambda bi: (bi, 0, 0))


def _ssd(ssd3, small3, cw, cb, dtb, alog, dskip, gn):
    b, seq, width = ssd3.shape
    conv_dim = GROUP_WIDTH + 2 * LANES
    kern = functools.partial(_ssd_kernel, seq=seq)
    nb = min(SSD_BATCH, b)

    def vec(n, rows=1):
        return pl.BlockSpec((rows, n), lambda bi: (0, 0))

    return pl.pallas_call(
        kern,
        grid=(b // nb,),
        in_specs=[
            _batch_spec(nb, seq, width), _batch_spec(nb, seq, LANES),
            vec(conv_dim, SSD_CONV), vec(conv_dim), vec(LANES), vec(LANES), vec(GROUP_WIDTH), vec(GROUP_WIDTH),
        ],
        out_specs=_batch_spec(nb, seq, GROUP_WIDTH),
        out_shape=jax.ShapeDtypeStruct((b, seq, GROUP_WIDTH), BF16),
        scratch_shapes=[pltpu.VMEM((nb, 2, LANES, LANES), F32)],
        compiler_params=_cparams(("parallel",)),
        name="ssd",
    )(ssd3, small3, cw, cb, dtb, alog, dskip, gn)


def _gla_levels(c_len, sub):
    sizes = []
    size = sub
    while size <= c_len:
        sizes.append(size)
        size *= 2
    return sizes


def _gla_consts(c_len, dk, dv, sub):
    dkh = dk // N_HEADS
    dvh = dv // N_HEADS
    sizes = _gla_levels(c_len, sub)
    row = _iota2((c_len, c_len), 0)
    col = _iota2((c_len, c_len), 1)
    sum_masks = []
    for size in sizes:
        same = (row // size) == (col // size)
        sum_masks.append((same & (col <= row)).astype(BF16))
        sum_masks.append(same.astype(BF16))
    arow = _iota2((c_len, N_HEADS * c_len), 0)
    acol = _iota2((c_len, N_HEADS * c_len), 1) % c_len
    keep = [((arow // sub) == (acol // sub)) & (acol <= arow)]
    for level in range(1, len(sizes)):
        size, half = sizes[level], sizes[level - 1]
        keep.append(((arow // size) == (acol // size)) & ((arow % size) >= half) & ((acol % size) < half))
    klane = _iota2((c_len, dk), 1) // dkh
    vlane = _iota2((c_len, dv), 1) // dvh
    diag = (_iota2((dv, dk), 0) // dvh) == (_iota2((dv, dk), 1) // dkh)
    return dict(sizes=sizes, sum_mask=jnp.concatenate(sum_masks, axis=0), keep=keep,
                klane=klane, vlane=vlane, diag=diag)


def _each(fn, *lists):
    return [fn(*args) for args in zip(*lists)]


def _round_robin(generators):
    live = list(generators)
    while live:
        still = []
        for gen in live:
            try:
                next(gen)
                still.append(gen)
            except StopIteration:
                pass
        live = still


def _gla_chunk(q, k, v, logf, state_t, consts):
    c_len = q[0].shape[0]
    sizes = consts["sizes"]
    n_levels = len(sizes)
    sums = _each(lambda x: _masked_sums(consts["sum_mask"], x, 2), logf)
    cs = [[s[(2 * i) * c_len:(2 * i + 1) * c_len] for i in range(n_levels)] for s in sums]
    tot = [[s[(2 * i + 1) * c_len:(2 * i + 2) * c_len] for i in range(n_levels)] for s in sums]

    def stack_heads(x, lane_head):
        return jnp.concatenate([jnp.where(lane_head == h, x, 0.0) for h in range(N_HEADS)], axis=0).astype(BF16)

    q0 = _each(lambda x, c: (x * jnp.exp(c[0])).astype(BF16), q, cs)
    k0 = _each(lambda x, c: stack_heads(x * jnp.exp(-c[0]), consts["klane"]), k, cs)
    att = _each(lambda a, b: jnp.where(consts["keep"][0], _dot_nt(a, b), 0.0), q0, k0)
    for level in range(1, n_levels):
        q_l = q0 if level == 1 else _each(lambda x, c: (x * jnp.exp(c[level - 1])).astype(BF16), q, cs)
        k_l = _each(lambda x, c, t: stack_heads(x * jnp.exp(t[level - 1] - c[level - 1]), consts["klane"]), k, cs, tot)
        att = _each(lambda a, b, prev: jnp.where(consts["keep"][level], _dot_nt(a, b), prev), q_l, k_l, att)
    v_stack = _each(lambda x: stack_heads(x, consts["vlane"]), v)
    o = _each(lambda a, b: _dot(a.astype(BF16), b), att, v_stack)

    q_s = _each(lambda x, c: (x * jnp.exp(c[-1])).astype(BF16), q, cs)
    k_s = _each(lambda x, c, t: (x * jnp.exp(t[-1] - c[-1])).astype(BF16), k, cs, tot)
    o = _each(lambda acc, a, s: acc + _dot_nt(a, s.astype(BF16)), o, q_s, state_t)
    d_state = _each(lambda x, y: jnp.where(consts["diag"], _dot_tn(x.astype(BF16), y), 0.0), v, k_s)
    new_state = _each(lambda s, t, d: s * jnp.exp(t[-1][0:1, :]) + d, state_t, tot, d_state)
    return o, new_state


def _gla_kernel(gla_ref, small_ref, wg_ref, bg_ref, on_ref, o_ref, st_ref, *, seq):
    c_len = GLA_CHUNK
    dk = GLA_KEY_DIM
    st_ref[...] = jnp.zeros_like(st_ref)
    scale = (dk // N_HEADS) ** -0.5
    consts = _gla_consts(c_len, dk, GROUP_WIDTH, c_len)

    def chunk(n, _):
        rows = pl.ds(pl.multiple_of(n * c_len, c_len), c_len)
        batch = range(gla_ref.shape[0])
        gate = [_dot(small_ref[gb, rows, :].astype(BF16), wg_ref[...]) + bg_ref[...] for gb in batch]
        logf = [_log_sigmoid(x) * (1.0 / GLA_TAU) for x in gate]
        q = [gla_ref[gb, rows, 0:dk].astype(F32) * scale for gb in batch]
        k = [gla_ref[gb, rows, dk:2 * dk].astype(F32) for gb in batch]
        v = [gla_ref[gb, rows, 2 * dk:2 * dk + GROUP_WIDTH].astype(F32) for gb in batch]
        o, new_state = _gla_chunk(q, k, v, logf, [st_ref[gb] for gb in batch], consts)
        for gb in batch:
            st_ref[gb] = new_state[gb]
            r = gla_ref[gb, rows, 2 * dk + GROUP_WIDTH:].astype(F32)
            o_ref[gb, rows, :] = (_rms_heads64_wide(o[gb], on_ref[...]) * _silu(r)).astype(o_ref.dtype)
        return 0

    lax.fori_loop(0, seq // c_len, chunk, 0)


def _gla(gla3, small3, wg, bg, on):
    b, seq, width = gla3.shape
    kern = functools.partial(_gla_kernel, seq=seq)
    nb = min(GLA_BATCH, b)
    return pl.pallas_call(
        kern,
        grid=(b // nb,),
        in_specs=[
            _batch_spec(nb, seq, width), _batch_spec(nb, seq, LANES),
            pl.BlockSpec((LANES, GLA_KEY_DIM), lambda bi: (0, 0)),
            pl.BlockSpec((1, GLA_KEY_DIM), lambda bi: (0, 0)),
            pl.BlockSpec((1, GROUP_WIDTH), lambda bi: (0, 0)),
        ],
        out_specs=_batch_spec(nb, seq, GROUP_WIDTH),
        out_shape=jax.ShapeDtypeStruct((b, seq, GROUP_WIDTH), BF16),
        scratch_shapes=[pltpu.VMEM((nb, GROUP_WIDTH, GLA_KEY_DIM), F32)],
        compiler_params=_cparams(("parallel",)),
        name="gla",
    )(gla3, small3, wg, bg, on)


def _hgrn_kernel(hg_ref, lbl_ref, on_ref, o_ref, st_ref, *, seq, layer):
    c_len = HGRN_CHUNK
    dk = HGRN_FDIM
    st_ref[...] = jnp.zeros_like(st_ref)
    logits = lbl_ref[...]
    e = jnp.exp(logits - jnp.max(logits, axis=0, keepdims=True))
    soft = e / jnp.sum(e, axis=0, keepdims=True)
    lb = jnp.zeros((1, dk), F32)
    for i in range(1, layer + 1):
        lb = lb + soft[i:i + 1, :]
    lb = jnp.clip(lb, 0.0, 1.0 - 1e-6)
    consts = _gla_consts(c_len, dk, GROUP_WIDTH, HGRN_SUB)

    def chunk(n, _):
        rows = pl.ds(pl.multiple_of(n * c_len, c_len), c_len)
        batch = range(hg_ref.shape[0])
        sig = [_sigmoid_pair(hg_ref[gb, rows, dk:2 * dk].astype(F32)) for gb in batch]
        logf = [jnp.log(lb + (1.0 - lb) * s[0]) for s in sig]
        k = [(1.0 - lb) * s[1] for s in sig]
        q = [hg_ref[gb, rows, 0:dk].astype(F32) for gb in batch]
        v = [hg_ref[gb, rows, 2 * dk:2 * dk + GROUP_WIDTH].astype(F32) for gb in batch]
        o, new_state = _gla_chunk(q, k, v, logf, [st_ref[gb] for gb in batch], consts)
        for gb in batch:
            st_ref[gb] = new_state[gb]
            g = hg_ref[gb, rows, 2 * dk + GROUP_WIDTH:].astype(F32)
            o_ref[gb, rows, :] = (_rms_heads64_wide(o[gb], on_ref[...]) * _silu(g)).astype(o_ref.dtype)
        return 0

    lax.fori_loop(0, seq // c_len, chunk, 0)


def _hgrn(hg3, lb_logits, on, layer):
    b, seq, width = hg3.shape
    depth = lb_logits.shape[0]
    kern = functools.partial(_hgrn_kernel, seq=seq, layer=layer)
    nb = min(HGRN_BATCH, b)
    return pl.pallas_call(
        kern,
        grid=(b // nb,),
        in_specs=[
            _batch_spec(nb, seq, width),
            pl.BlockSpec((depth, HGRN_FDIM), lambda bi: (0, 0)),
            pl.BlockSpec((1, GROUP_WIDTH), lambda bi: (0, 0)),
        ],
        out_specs=_batch_spec(nb, seq, GROUP_WIDTH),
        out_shape=jax.ShapeDtypeStruct((b, seq, GROUP_WIDTH), BF16),
        scratch_shapes=[pltpu.VMEM((nb, GROUP_WIDTH, HGRN_FDIM), F32)],
        compiler_params=_cparams(("parallel",)),
        name="hgrn2",
    )(hg3, lb_logits, on)


def _kv_kernel(mem_ref, mn_ref, w_ref, kn_ref, k_ref, v_ref):
    mem_n = _rms(mem_ref[...], mn_ref[...]).astype(BF16)
    kv = _dot(mem_n, w_ref[...])
    for h in range(XA_HEADS):
        cols = slice(h * XA_HEAD_DIM, (h + 1) * XA_HEAD_DIM)
        k_ref[:, cols] = _rms(kv[:, cols], kn_ref[...]).astype(k_ref.dtype)
    v_ref[...] = kv[:, D_MODEL:].astype(v_ref.dtype)


def _kv_proj(mem2d, mem_norm, w_kv, kn, tm):
    m = mem2d.shape[0]
    depth = w_kv.shape[0]
    out = jax.ShapeDtypeStruct((depth, m, D_MODEL), BF16)
    return pl.pallas_call(
        _kv_kernel,
        grid=(depth, m // tm),
        in_specs=[
            pl.BlockSpec((tm, D_MODEL), lambda l, i: (i, 0)),
            pl.BlockSpec((1, D_MODEL), lambda l, i: (0, 0)),
            pl.BlockSpec((None, D_MODEL, 2 * D_MODEL), lambda l, i: (l, 0, 0)),
            pl.BlockSpec((None, 1, XA_HEAD_DIM), lambda l, i: (l, 0, 0)),
        ],
        out_specs=[pl.BlockSpec((None, tm, D_MODEL), lambda l, i: (l, i, 0))] * 2,
        out_shape=[out, out],
        compiler_params=_cparams(("parallel", "parallel")),
        name="mem_kv",
    )(mem2d, mem_norm, w_kv, kn)


def _mix_xattn_kernel(x_ref, yf_ref, ys_ref, yg_ref, yh_ref, fon_ref, wout_ref, ln_ref, wq_ref, qn_ref,
                      k_ref, v_ref, wo_ref, o_ref):
    y_fox = _rms_heads64_wide(yf_ref[...].astype(F32), fon_ref[...]).astype(BF16)
    mixed = jnp.concatenate([y_fox, ys_ref[...], yg_ref[...], yh_ref[...]], axis=-1)
    x = x_ref[...] + _dot(mixed, wout_ref[...])
    h = _rms(x, ln_ref[...]).astype(BF16)
    q = _dot(h, wq_ref[...])
    scale = XA_HEAD_DIM ** -0.5
    outs = []
    for hd in range(XA_HEADS):
        cols = slice(hd * XA_HEAD_DIM, (hd + 1) * XA_HEAD_DIM)
        qh = (_rms(q[:, cols], qn_ref[...]) * scale).astype(BF16)
        logits = _dot_nt(qh, k_ref[:, cols])
        p = jnp.exp(logits - jnp.max(logits, axis=-1, keepdims=True))
        p = p / jnp.sum(p, axis=-1, keepdims=True)
        outs.append(_dot(p.astype(BF16), v_ref[:, cols]).astype(BF16))
    o = jnp.concatenate(outs, axis=-1)
    o_ref[...] = x + _dot(o, wo_ref[...])


def _mix_xattn(x2d, ys, fox_on, w_out, ln, wq, qn, k3, v3, wo, tm, seq):
    m = x2d.shape[0]
    n_mem = k3.shape[1]
    tiles_per_seq = seq // tm
    row = lambda i: (i, 0)
    const = lambda i: (0, 0)
    mem_spec = pl.BlockSpec((None, n_mem, D_MODEL), lambda i: (i // tiles_per_seq, 0, 0))
    return pl.pallas_call(
        _mix_xattn_kernel,
        grid=(m // tm,),
        in_specs=[
            pl.BlockSpec((tm, D_MODEL), row),
            *[pl.BlockSpec((tm, GROUP_WIDTH), row)] * 4,
            pl.BlockSpec((1, GROUP_WIDTH), const),
            _resident((D_MODEL, D_MODEL)),
            pl.BlockSpec((1, D_MODEL), const),
            _resident((D_MODEL, D_MODEL)),
            pl.BlockSpec((1, XA_HEAD_DIM), const),
            mem_spec, mem_spec,
            _resident((D_MODEL, D_MODEL)),
        ],
        out_specs=pl.BlockSpec((tm, D_MODEL), row),
        out_shape=jax.ShapeDtypeStruct((m, D_MODEL), F32),
        compiler_params=_cparams(("parallel",)),
        name="mix_xattn",
    )(x2d, *ys, fox_on, w_out, ln, wq, qn, k3, v3, wo)


def _ffn_kernel(x_ref, xp_ref, ln_ref, wup_ref, cw_ref, cb_ref, wdown_ref, o_ref, act_ref, *, tiles_per_seq):
    i = pl.program_id(0)
    x = x_ref[...]
    first = (i % tiles_per_seq) == 0
    h = _rms(x, ln_ref[...]).astype(BF16)
    hp = jnp.where(first, 0.0, _rms(xp_ref[...], ln_ref[...])).astype(BF16)
    h_ext = jnp.concatenate([hp, h], axis=0)
    for c in range(D_FF // FF_TILE):
        cols = slice(c * FF_TILE, (c + 1) * FF_TILE)
        vcols = slice(D_FF + c * FF_TILE, D_FF + (c + 1) * FF_TILE)
        gate = _dot(h_ext, wup_ref[:, cols])
        conv = cb_ref[:, cols] + cw_ref[FFN_CONV - 1:FFN_CONV, cols] * gate[HALO:, :]
        for shift in range(1, FFN_CONV):
            conv = conv + (cw_ref[FFN_CONV - 1 - shift:FFN_CONV - shift, cols]
                           * pltpu.roll(gate, shift, 0)[HALO:, :])
        val = _dot(h, wup_ref[:, vcols])
        act_ref[:, cols] = (_silu(conv) * val).astype(BF16)
    o_ref[...] = x + _dot(act_ref[...], wdown_ref[...])


def _ffn(x2d, ln, w_up, cw, cb, w_down, tm, seq):
    m = x2d.shape[0]
    kern = functools.partial(_ffn_kernel, tiles_per_seq=seq // tm)
    const = lambda i: (0, 0)
    halo_blocks = tm // HALO
    return pl.pallas_call(
        kern,
        grid=(m // tm,),
        in_specs=[
            pl.BlockSpec((tm, D_MODEL), lambda i: (i, 0)),
            pl.BlockSpec((HALO, D_MODEL), lambda i: (jnp.maximum(i * halo_blocks - 1, 0), 0)),
            pl.BlockSpec((1, D_MODEL), const),
            _resident((D_MODEL, 2 * D_FF)),
            pl.BlockSpec((FFN_CONV, D_FF), const),
            pl.BlockSpec((1, D_FF), const),
            _resident((D_FF, D_MODEL)),
        ],
        out_specs=pl.BlockSpec((tm, D_MODEL), lambda i: (i, 0)),
        out_shape=jax.ShapeDtypeStruct((m, D_MODEL), F32),
        scratch_shapes=[pltpu.VMEM((tm, D_FF), BF16)],
        compiler_params=_cparams(("parallel",)),
        name="conv_glu_ffn",
    )(x2d, x2d, ln, w_up, cw, cb, w_down)


def _pad_lanes(vec, offset, total=LANES):
    vec = vec.astype(F32).reshape(1, -1)
    return jnp.pad(vec, ((0, 0), (offset, total - offset - vec.shape[1])))


def _arrange_w_in(w):
    fq, fk, fv, ff, sz, sxbc, sdt, gq, gk, gv, ga, gr, hq, hf, hi, hg = jnp.split(
        w, [256, 512, 768, 772, 1028, 1540, 1544, 1672, 1800, 2056, 2072, 2328, 2584, 2840, 3096], axis=1)
    small = jnp.concatenate([ff, sdt, ga], axis=1)
    small = jnp.pad(small, ((0, 0), (0, LANES - small.shape[1])))
    return jnp.concatenate([fq, fk, fv, sz, sxbc, gq, gk, gv, gr, hq, hf, hi, hg, small], axis=1).astype(BF16)


def kernel(x, mem, ln_mix, w_in, w_out, fox_f_bias, fox_qn, fox_kn, fox_on, ssd_conv_w, ssd_conv_b, ssd_dt_bias,
           ssd_a_log, ssd_d, ssd_norm, gla_w_g2, gla_b_g2, gla_norm, hgrn_lb_logits, hgrn_norm, ln_xattn, mem_norm,
           xa_wq, xa_wkv, xa_wo, xa_qn, xa_kn, ln_ffn, ffn_w_up, ffn_conv_w, ffn_conv_b, ffn_w_down):
    b, seq, d = x.shape
    depth = w_in.shape[0]
    n_mem = mem.shape[1]
    m = b * seq
    tm = min(ROW_TILE, seq)
    x2d = x.reshape(m, d)

    k_all, v_all = _kv_proj(mem.reshape(b * n_mem, d), mem_norm.reshape(1, d), xa_wkv.astype(BF16),
                            xa_kn.reshape(depth, 1, XA_HEAD_DIM), min(512, b * n_mem))
    for l in range(depth):
        tile4 = lambda v: jnp.tile(v.astype(F32).reshape(1, -1), (1, GROUP_WIDTH // HEAD_DIM))
        fox, ssd, gla, hgrn, small = _in_proj(x2d, ln_mix[l].reshape(1, d), _arrange_w_in(w_in[l]),
                                              tile4(fox_qn[l]), tile4(fox_kn[l]), tm)
        small3 = small.reshape(b, seq, LANES)
        y_fox = _fox(fox.reshape(b, seq, -1), small3, _pad_lanes(fox_f_bias[l], SMALL_FF))
        y_ssd = _ssd(ssd.reshape(b, seq, -1), small3, ssd_conv_w[l], ssd_conv_b[l].reshape(1, -1),
                     _pad_lanes(ssd_dt_bias[l], SMALL_DT), _pad_lanes(ssd_a_log[l], SMALL_DT),
                     jnp.repeat(ssd_d[l].astype(F32), HEAD_DIM).reshape(1, -1), ssd_norm[l].reshape(1, -1))
        wg = jnp.pad(gla_w_g2[l], ((SMALL_GA, LANES - SMALL_GA - GLA_GATE_RANK), (0, 0))).astype(BF16)
        y_gla = _gla(gla.reshape(b, seq, -1), small3, wg, gla_b_g2[l].reshape(1, -1), tile4(gla_norm[l]))
        y_hgrn = _hgrn(hgrn.reshape(b, seq, -1), hgrn_lb_logits, tile4(hgrn_norm[l]), l)
        ys = [y.reshape(m, GROUP_WIDTH) for y in (y_fox, y_ssd, y_gla, y_hgrn)]
        x2d = _mix_xattn(x2d, ys, tile4(fox_on[l]), w_out[l].astype(BF16), ln_xattn[l].reshape(1, d),
                         xa_wq[l].astype(BF16),
                         xa_qn[l].reshape(1, -1), k_all[l].reshape(b, n_mem, d), v_all[l].reshape(b, n_mem, d),
                         xa_wo[l].astype(BF16), tm, seq)
        x2d = _ffn(x2d, ln_ffn[l].reshape(1, d), ffn_w_up[l].astype(BF16), ffn_conv_w[l],
                   ffn_conv_b[l].reshape(1, -1), ffn_w_down[l].astype(BF16), min(FFN_ROWS, seq), seq)
    return x2d.reshape(b, seq, d)
```

```python
import functools

import jax
import jax.numpy as jnp
import numpy as np
from jax import lax
from jax.experimental import pallas as pl
from jax.experimental.pallas import tpu as pltpu

F32 = jnp.float32
BF16 = jnp.bfloat16

EPS = 1e-6
MASK_VALUE = -1e30
D_MODEL = 1024
GROUP_WIDTH = 256
HEAD_DIM = 64
LANES = 128
N_HEADS = 4
SSD_STATE = 64
SSD_CONV = 4
SSD_CHUNK = 128
GLA_KEY_DIM = 128
GLA_GATE_RANK = 16
GLA_TAU = 16.0
GLA_CHUNK = 64
HGRN_FDIM = 256
HGRN_CHUNK = 64
HGRN_SUB = 16
XA_HEADS = 4
XA_HEAD_DIM = 256
D_FF = 2816
FFN_CONV = 3
FF_TILE = 256
FFN_ROWS = 1024
ROW_TILE = 1024
HALO = 16
SSD_BATCH = 4
GLA_BATCH = 8
HGRN_BATCH = 8
MIX_SEGMENT = 1024

SMALL_FF = 0
SMALL_DT = 4
SMALL_GA = 8

VMEM_LIMIT = 56 * 1024 * 1024

HIGHEST = lax.Precision.HIGHEST


def _cparams(sem):
    return pltpu.CompilerParams(dimension_semantics=sem, vmem_limit_bytes=VMEM_LIMIT)


def _resident(shape):
    return pl.BlockSpec(shape, lambda *_: (0,) * len(shape), pipeline_mode=pl.Buffered(1))


def _dot(a, b, precision=None):
    return jnp.dot(a, b, preferred_element_type=F32, precision=precision)


def _dot_nt(a, b, precision=None):
    return lax.dot_general(a, b, (((1,), (1,)), ((), ())), preferred_element_type=F32, precision=precision)


def _dot_tn(a, b, precision=None):
    return lax.dot_general(a, b, (((0,), (0,)), ((), ())), preferred_element_type=F32, precision=precision)


def _rms(x, w):
    return x * lax.rsqrt(jnp.mean(x * x, axis=-1, keepdims=True) + EPS) * w


def _rms_heads64(x, w):
    lane = lax.broadcasted_iota(jnp.int32, x.shape, 1)
    lo = lane < HEAD_DIM
    sq = x * x
    s_lo = jnp.sum(jnp.where(lo, sq, 0.0), axis=-1, keepdims=True)
    s_hi = jnp.sum(jnp.where(lo, 0.0, sq), axis=-1, keepdims=True)
    ms = jnp.where(lo, s_lo, s_hi) * (1.0 / HEAD_DIM)
    return x * lax.rsqrt(ms + EPS) * w


def _rms_heads64_wide(x, w):
    return jnp.concatenate(
        [_rms_heads64(x[:, i * LANES:(i + 1) * LANES], w[:, i * LANES:(i + 1) * LANES]) for i in range(2)], axis=-1)


def _sigmoid_pair(z):
    e = jnp.exp(-jnp.abs(z))
    big = 1.0 / (1.0 + e)
    small = e * big
    pos = z >= 0
    return jnp.where(pos, big, small), jnp.where(pos, small, big)


def _silu(z):
    return z * (1.0 / (1.0 + jnp.exp(-z)))


def _log_sigmoid(z):
    return jnp.minimum(z, 0.0) - jnp.log1p(jnp.exp(-jnp.abs(z)))


def _softplus(z):
    return jnp.maximum(z, 0.0) + jnp.log1p(jnp.exp(-jnp.abs(z)))


def _iota2(shape, axis):
    return lax.broadcasted_iota(jnp.int32, shape, axis)


def _lane_col(x, lane_index):
    return x[:, lane_index:lane_index + 1]


def _masked_sums(mask16, x, pieces, form="mask_x"):
    total = None
    rest = x
    for index in range(pieces):
        part = rest.astype(BF16)
        if form == "mask_x":
            term = _dot(mask16, part)
        elif form == "mask_xt":
            term = _dot_nt(mask16, part)
        else:
            term = _dot(part, mask16)
        total = term if total is None else total + term
        if index + 1 < pieces:
            rest = rest - part.astype(F32)
    return total


IN_GROUPS = (768, 768, 768, 1024, 128)
D_IN_PAD = sum(IN_GROUPS)


def _in_proj_kernel(x_ref, ln_ref, w_ref, fqn_ref, fkn_ref, fox_ref, ssd_ref, gla_ref, hgrn_ref, small_ref):
    h = _rms(x_ref[...], ln_ref[...]).astype(BF16)
    outs = (fox_ref, ssd_ref, gla_ref, hgrn_ref, small_ref)
    start = 0
    for width, out in zip(IN_GROUPS, outs):
        proj = _dot(h, w_ref[:, start:start + width])
        if out is fox_ref:
            gw = GROUP_WIDTH
            out[:, 0:gw] = (_rms_heads64_wide(proj[:, 0:gw], fqn_ref[...]) * HEAD_DIM ** -0.5).astype(out.dtype)
            out[:, gw:2 * gw] = _rms_heads64_wide(proj[:, gw:2 * gw], fkn_ref[...]).astype(out.dtype)
            out[:, 2 * gw:] = proj[:, 2 * gw:].astype(out.dtype)
        else:
            out[...] = proj.astype(out.dtype)
        start += width


def _in_proj(x2d, ln, w_r, fox_qn, fox_kn, tm):
    m = x2d.shape[0]
    out_shape = [jax.ShapeDtypeStruct((m, n), BF16) for n in IN_GROUPS[:4]]
    out_shape.append(jax.ShapeDtypeStruct((m, IN_GROUPS[4]), F32))
    vec = pl.BlockSpec((1, GROUP_WIDTH), lambda i: (0, 0))
    return pl.pallas_call(
        _in_proj_kernel,
        grid=(m // tm,),
        in_specs=[
            pl.BlockSpec((tm, D_MODEL), lambda i: (i, 0)),
            pl.BlockSpec((1, D_MODEL), lambda i: (0, 0)),
            _resident((D_MODEL, D_IN_PAD)),
            vec, vec,
        ],
        out_specs=[pl.BlockSpec((tm, n), lambda i: (i, 0)) for n in IN_GROUPS],
        out_shape=out_shape,
        compiler_params=_cparams(("parallel",)),
        name="in_proj",
    )(x2d, ln, w_r, fox_qn, fox_kn)


FOX_TILE = 256
FOX_KEY_BLOCKS = 4
FOX_CUM_BLOCK = 256
N_SPLIT = 3


def _split3(c):
    hi = c.astype(BF16).astype(F32)
    r = c - hi
    mid = r.astype(BF16).astype(F32)
    lo = (r - mid).astype(BF16).astype(F32)
    return hi, mid, lo


def _fox_selectors():
    selq = np.zeros(((N_SPLIT + 1) * LANES, GROUP_WIDTH), np.float32)
    selk = np.zeros_like(selq)
    for h in range(N_HEADS):
        base = (h // 2) * LANES + HEAD_DIM * (1 - h % 2)
        for p in range(N_SPLIT):
            selq[p * LANES + SMALL_FF + h, base + p] = 1.0
            selq[N_SPLIT * LANES, base + N_SPLIT + p] = 1.0
            selk[N_SPLIT * LANES, base + p] = 1.0
            selk[p * LANES + SMALL_FF + h, base + N_SPLIT + p] = -1.0
    return jnp.asarray(selq, BF16), jnp.asarray(selk, BF16)


def _fox_kernel(q_ref, k_ref, v_ref, small_ref, bias_ref, selq_ref, selk_ref, o_ref,
                ka_ref, qaug_ref, va_ref, *, seq):
    qi = pl.program_id(1)
    tq = FOX_TILE
    halves = GROUP_WIDTH // LANES

    def own_lanes(lane, hh):
        return (lane >= HEAD_DIM * hh) & (lane < HEAD_DIM * (hh + 1))

    @pl.when(qi == 0)
    def _prepare():
        nblk = seq // FOX_CUM_BLOCK
        tri = (_iota2((FOX_CUM_BLOCK, FOX_CUM_BLOCK), 0) >= _iota2((FOX_CUM_BLOCK, FOX_CUM_BLOCK), 1)).astype(BF16)
        lane = _iota2((FOX_CUM_BLOCK, LANES), 1)
        carry = jnp.zeros((1, LANES), F32)
        for blk in range(nblk):
            rows = pl.ds(blk * FOX_CUM_BLOCK, FOX_CUM_BLOCK)
            logf = _log_sigmoid(small_ref[rows, :] + bias_ref[...])
            c = _masked_sums(tri, logf, 3) + carry
            carry = c[FOX_CUM_BLOCK - 1:FOX_CUM_BLOCK, :]
            pieces = jnp.concatenate([*_split3(c), jnp.ones_like(c)], axis=-1).astype(BF16)
            k_aug = _dot(pieces, selk_ref[...]).astype(BF16)
            qaug_ref[rows, :] = _dot(pieces, selq_ref[...]).astype(BF16)
            for half in range(halves):
                lanes = slice(half * LANES, (half + 1) * LANES)
                kn = k_ref[rows, lanes]
                v_half = v_ref[rows, lanes]
                for hh in range(2):
                    head = 2 * half + hh
                    own = own_lanes(lane, hh)
                    va_ref[head, rows, :] = jnp.where(own, v_half, jnp.ones_like(v_half))
                    ka_ref[head, rows, :] = jnp.where(own, kn, k_aug[:, lanes])

    qrows = pl.ds(pl.multiple_of(qi * tq, tq), tq)
    lane = _iota2((tq, LANES), 1)
    lo_half = lane < HEAD_DIM
    qas = []
    for half in range(halves):
        lanes = slice(half * LANES, (half + 1) * LANES)
        for hh in range(2):
            qas.append(jnp.where(own_lanes(lane, hh), q_ref[:, lanes], qaug_ref[qrows, lanes]))

    heads = range(N_HEADS)

    def consume(first_block, n_blocks, carry, diagonal_last):
        width = n_blocks * tq
        rows = pl.ds(pl.multiple_of(first_block * tq, tq), width)
        m_prev, acc = carry
        s = [_dot_nt(qas[head], ka_ref[head, rows, :]) for head in heads]
        if diagonal_last:
            visible = _iota2((tq, width), 0) + (n_blocks - 1) * tq >= _iota2((tq, width), 1)
            s = [jnp.where(visible, x, MASK_VALUE) for x in s]
        m_new = _each(lambda m, x: jnp.maximum(m, jnp.max(x, axis=-1, keepdims=True)), m_prev, s)
        p = _each(lambda x, m: jnp.exp(x - m).astype(BF16), s, m_new)
        pv = [_dot(p[head], va_ref[head, rows, :]) for head in heads]
        acc = _each(lambda m0, m1, a, x: jnp.exp(m0 - m1) * a + x, m_prev, m_new, acc, pv)
        return m_new, acc

    group = FOX_KEY_BLOCKS
    init = ([jnp.full((tq, 1), MASK_VALUE, F32) for _ in heads], [jnp.zeros((tq, LANES), F32) for _ in heads])
    carry = lax.fori_loop(0, qi // group, lambda g, c: consume(g * group, group, c, False), init)
    leftover = qi % group
    branches = [functools.partial(lambda c, n: consume(qi - n, n + 1, c, True), n=n) for n in range(group)]
    _, acc = lax.switch(leftover, branches, carry)
    for half in range(halves):
        acc_lo, acc_hi = acc[2 * half], acc[2 * half + 1]
        o = jnp.where(lo_half, acc_lo / pltpu.roll(acc_lo, HEAD_DIM, 1), acc_hi / pltpu.roll(acc_hi, HEAD_DIM, 1))
        o_ref[:, half * LANES:(half + 1) * LANES] = o.astype(o_ref.dtype)


def _fox(fox3, small3, bias):
    b, seq, _ = fox3.shape
    tq = FOX_TILE
    kern = functools.partial(_fox_kernel, seq=seq)
    selq, selk = _fox_selectors()
    vec = pl.BlockSpec((1, LANES), lambda bi, qi: (0, 0))
    return pl.pallas_call(
        kern,
        grid=(b, seq // tq),
        in_specs=[
            pl.BlockSpec((None, tq, GROUP_WIDTH), lambda bi, qi: (bi, qi, 0)),
            pl.BlockSpec((None, seq, GROUP_WIDTH), lambda bi, qi: (bi, 0, 1)),
            pl.BlockSpec((None, seq, GROUP_WIDTH), lambda bi, qi: (bi, 0, 2)),
            pl.BlockSpec((None, seq, LANES), lambda bi, qi: (bi, 0, 0)),
            vec,
            _resident(selq.shape), _resident(selk.shape),
        ],
        out_specs=pl.BlockSpec((None, tq, GROUP_WIDTH), lambda bi, qi: (bi, qi, 0)),
        out_shape=jax.ShapeDtypeStruct((b, seq, GROUP_WIDTH), BF16),
        scratch_shapes=[pltpu.VMEM((N_HEADS, seq, LANES), BF16), pltpu.VMEM((seq, GROUP_WIDTH), BF16),
                        pltpu.VMEM((N_HEADS, seq, LANES), BF16)],
        compiler_params=_cparams(("parallel", "arbitrary")),
        name="fox",
    )(fox3, fox3, fox3, small3, bias, selq, selk)


def _ssd_kernel(ssd_ref, small_ref, cw_ref, cb_ref, dtb_ref, alog_ref, dskip_ref, gn_ref, o_ref,
                st_ref, *, seq):
    c_len = SSD_CHUNK
    assert c_len == LANES
    n_batch = ssd_ref.shape[0]
    st_ref[...] = jnp.zeros_like(st_ref)

    a_lane = -jnp.exp(alog_ref[...])
    tri = (_iota2((c_len, c_len), 0) >= _iota2((c_len, c_len), 1))
    tri16 = tri.astype(BF16)
    sel16 = (_iota2((8, LANES), 0) + SMALL_DT == _iota2((8, LANES), 1)).astype(BF16)
    lane = _iota2((c_len, LANES), 1)
    lo = lane < HEAD_DIM

    def chunk(n, _):
        _round_robin([chunk_one(n, gb) for gb in range(n_batch)])
        return 0

    def chunk_one(n, gb):
        rows = pl.ds(pl.multiple_of(n * c_len, c_len), c_len)
        halo_rows = pl.ds(pl.multiple_of(jnp.maximum(n * c_len - HALO, 0), HALO), HALO)
        halo = jnp.where(n > 0, ssd_ref[gb, halo_rows, GROUP_WIDTH:].astype(F32), 0.0)
        cur = ssd_ref[gb, rows, GROUP_WIDTH:].astype(F32)
        ext = jnp.concatenate([halo, cur], axis=0)
        conv = cb_ref[...] + cw_ref[SSD_CONV - 1:SSD_CONV, :] * cur
        for shift in range(1, SSD_CONV):
            conv = conv + cw_ref[SSD_CONV - 1 - shift:SSD_CONV - shift, :] * pltpu.roll(ext, shift, 0)[HALO:, :]
        xc = _silu(conv)
        xs = xc[:, 0:GROUP_WIDTH]
        bm = xc[:, GROUP_WIDTH:GROUP_WIDTH + LANES]
        cm = xc[:, GROUP_WIDTH + LANES:]
        dt = _softplus(small_ref[gb, rows, :] + dtb_ref[...])
        a_cs = _masked_sums(tri16, dt * a_lane, 3)
        yield
        a_row = _masked_sums(sel16, a_cs, 3, form="mask_xt")
        cm16 = cm.astype(BF16)
        y_halves = []
        for g in range(2):
            in_group = (lane >= SSD_STATE * g) & (lane < SSD_STATE * (g + 1))
            b_g = jnp.where(in_group, bm, 0.0).astype(BF16)
            scores = _dot_nt(cm16, b_g)
            yield
            heads = (2 * g, 2 * g + 1)
            a_col = [jnp.broadcast_to(_lane_col(a_cs, SMALL_DT + h), (c_len, LANES)) for h in heads]
            dt_col = [_lane_col(dt, SMALL_DT + h) for h in heads]
            a_own = jnp.where(lo, a_col[0], a_col[1])
            a_last = a_own[c_len - 1:c_len, :]
            xdt = xs[:, g * LANES:(g + 1) * LANES] * jnp.where(lo, dt_col[0], dt_col[1])
            xdt16 = xdt.astype(BF16)
            y_intra = []
            for index, h in enumerate(heads):
                seg = a_col[index] - a_row[h:h + 1, :]
                decay = jnp.where(tri, jnp.exp(jnp.where(tri, seg, 0.0)), 0.0)
                y_intra.append(_dot((scores * decay).astype(BF16), xdt16))
            state = st_ref[gb, g]
            y_state = _dot(cm16, state.astype(BF16))
            d_state = _dot_tn(b_g, (xdt * jnp.exp(a_last - a_own)).astype(BF16))
            yield
            y_halves.append(jnp.where(lo, y_intra[0], y_intra[1]) + y_state * jnp.exp(a_own))
            st_ref[gb, g] = state * jnp.exp(a_last) + d_state
        y = jnp.concatenate(y_halves, axis=-1) + xs * dskip_ref[...]
        y = y * _silu(ssd_ref[gb, rows, 0:GROUP_WIDTH].astype(F32))
        out = jnp.concatenate([_rms(y[:, i * LANES:(i + 1) * LANES], gn_ref[:, i * LANES:(i + 1) * LANES])
                               for i in range(2)], axis=-1)
        o_ref[gb, rows, :] = out.astype(o_ref.dtype)

    lax.fori_loop(0, seq // c_len, chunk, 0)


def _batch_spec(n_batch, seq, width):
    return pl.BlockSpec((n_batch, seq, width), lambda bi: (bi, 0, 0))


def _segment_spec(n_batch, seg, width):
    return pl.BlockSpec((n_batch, seg, width), lambda bi, si: (bi, si, 0))


def _zero_at_first_segment(state_ref):
    @pl.when(pl.program_id(1) == 0)
    def _():
        state_ref[...] = jnp.zeros_like(state_ref)


def _ssd(ssd3, small3, cw, cb, dtb, alog, dskip, gn):
    b, seq, width = ssd3.shape
    conv_dim = GROUP_WIDTH + 2 * LANES
    kern = functools.partial(_ssd_kernel, seq=seq)
    nb = min(SSD_BATCH, b)

    def vec(n, rows=1):
        return pl.BlockSpec((rows, n), lambda bi: (0, 0))

    return pl.pallas_call(
        kern,
        grid=(b // nb,),
        in_specs=[
            _batch_spec(nb, seq, width), _batch_spec(nb, seq, LANES),
            vec(conv_dim, SSD_CONV), vec(conv_dim), vec(LANES), vec(LANES), vec(GROUP_WIDTH), vec(GROUP_WIDTH),
        ],
        out_specs=_batch_spec(nb, seq, GROUP_WIDTH),
        out_shape=jax.ShapeDtypeStruct((b, seq, GROUP_WIDTH), BF16),
        scratch_shapes=[pltpu.VMEM((nb, 2, LANES, LANES), F32)],
        compiler_params=_cparams(("parallel",)),
        name="ssd",
    )(ssd3, small3, cw, cb, dtb, alog, dskip, gn)


def _gla_levels(c_len, sub):
    sizes = []
    size = sub
    while size <= c_len:
        sizes.append(size)
        size *= 2
    return sizes


def _gla_consts(c_len, dk, dv, sub):
    dkh = dk // N_HEADS
    dvh = dv // N_HEADS
    sizes = _gla_levels(c_len, sub)
    row = _iota2((c_len, c_len), 0)
    col = _iota2((c_len, c_len), 1)
    sum_masks = []
    for size in sizes:
        same = (row // size) == (col // size)
        sum_masks.append((same & (col <= row)).astype(BF16))
        sum_masks.append(same.astype(BF16))
    arow = _iota2((c_len, N_HEADS * c_len), 0)
    acol = _iota2((c_len, N_HEADS * c_len), 1) % c_len
    keep = [((arow // sub) == (acol // sub)) & (acol <= arow)]
    for level in range(1, len(sizes)):
        size, half = sizes[level], sizes[level - 1]
        keep.append(((arow // size) == (acol // size)) & ((arow % size) >= half) & ((acol % size) < half))
    klane = (_iota2((c_len, dk), 1) // dkh).astype(BF16)
    vlane = (_iota2((c_len, dv), 1) // dvh).astype(BF16)
    diag = (_iota2((dv, dk), 0) // dvh) == (_iota2((dv, dk), 1) // dkh)
    sum_mask = jnp.concatenate(sum_masks, axis=0)
    return dict(sizes=sizes, sum_mask=jnp.concatenate([sum_mask, sum_mask], axis=1), keep=keep,
                klane=klane, vlane=vlane, diag=diag)


def _each(fn, *lists):
    return [fn(*args) for args in zip(*lists)]


def _round_robin(generators):
    live = list(generators)
    while live:
        still = []
        for gen in live:
            try:
                next(gen)
                still.append(gen)
            except StopIteration:
                pass
        live = still


def _gla_chunk(q, k, v, logf, state_t, consts):
    c_len = q[0].shape[0]
    sizes = consts["sizes"]
    n_levels = len(sizes)
    def prefix_sums(x):
        hi = x.astype(BF16)
        lo = (x - hi.astype(F32)).astype(BF16)
        return _dot(consts["sum_mask"], jnp.concatenate([hi, lo], axis=0))

    sums = _each(prefix_sums, logf)
    cs = [[s[(2 * i) * c_len:(2 * i + 1) * c_len] for i in range(n_levels)] for s in sums]
    tot = [[s[(2 * i + 1) * c_len:(2 * i + 2) * c_len] for i in range(n_levels)] for s in sums]

    def stack_heads(x, lane_head):
        x16 = x.astype(BF16)
        return jnp.concatenate([jnp.where(lane_head == h, x16, jnp.zeros_like(x16)) for h in range(N_HEADS)], axis=0)

    q0 = _each(lambda x, c: (x * jnp.exp(c[0])).astype(BF16), q, cs)
    k0 = _each(lambda x, c: stack_heads(x * jnp.exp(-c[0]), consts["klane"]), k, cs)
    att = _each(lambda a, b: jnp.where(consts["keep"][0], _dot_nt(a, b), 0.0), q0, k0)
    for level in range(1, n_levels):
        q_l = q0 if level == 1 else _each(lambda x, c: (x * jnp.exp(c[level - 1])).astype(BF16), q, cs)
        k_l = _each(lambda x, c, t: stack_heads(x * jnp.exp(t[level - 1] - c[level - 1]), consts["klane"]), k, cs, tot)
        att = _each(lambda a, b, prev: jnp.where(consts["keep"][level], _dot_nt(a, b), prev), q_l, k_l, att)
    v_stack = _each(lambda x: stack_heads(x, consts["vlane"]), v)
    o = _each(lambda a, b: _dot(a.astype(BF16), b), att, v_stack)

    q_s = _each(lambda x, c: (x * jnp.exp(c[-1])).astype(BF16), q, cs)
    k_s = _each(lambda x, c, t: (x * jnp.exp(t[-1] - c[-1])).astype(BF16), k, cs, tot)
    o = _each(lambda acc, a, s: acc + _dot_nt(a, s.astype(BF16)), o, q_s, state_t)
    d_state = _each(lambda x, y: jnp.where(consts["diag"], _dot_tn(x.astype(BF16), y), 0.0), v, k_s)
    new_state = _each(lambda s, t, d: s * jnp.exp(t[-1][0:1, :]) + d, state_t, tot, d_state)
    return o, new_state


def _gla_kernel(gla_ref, small_ref, wg_ref, bg_ref, on_ref, o_ref, st_ref):
    c_len = GLA_CHUNK
    dk = GLA_KEY_DIM
    seq = gla_ref.shape[1]
    _zero_at_first_segment(st_ref)
    scale = (dk // N_HEADS) ** -0.5
    consts = _gla_consts(c_len, dk, GROUP_WIDTH, c_len)

    def chunk(n, _):
        rows = pl.ds(pl.multiple_of(n * c_len, c_len), c_len)
        batch = range(gla_ref.shape[0])
        gate = [_dot(small_ref[gb, rows, :].astype(BF16), wg_ref[...]) + bg_ref[...] for gb in batch]
        logf = [_log_sigmoid(x) * (1.0 / GLA_TAU) for x in gate]
        q = [gla_ref[gb, rows, 0:dk].astype(F32) * scale for gb in batch]
        k = [gla_ref[gb, rows, dk:2 * dk].astype(F32) for gb in batch]
        v = [gla_ref[gb, rows, 2 * dk:2 * dk + GROUP_WIDTH].astype(F32) for gb in batch]
        o, new_state = _gla_chunk(q, k, v, logf, [st_ref[gb] for gb in batch], consts)
        for gb in batch:
            st_ref[gb] = new_state[gb]
            r = gla_ref[gb, rows, 2 * dk + GROUP_WIDTH:].astype(F32)
            o_ref[gb, rows, :] = (_rms_heads64_wide(o[gb], on_ref[...]) * _silu(r)).astype(o_ref.dtype)
        return 0

    lax.fori_loop(0, seq // c_len, chunk, 0)


def _gla(gla3, small3, wg, bg, on):
    b, seq, width = gla3.shape
    nb = min(GLA_BATCH, b)
    seg = min(MIX_SEGMENT, seq)
    const = lambda bi, si: (0, 0)
    return pl.pallas_call(
        _gla_kernel,
        grid=(b // nb, seq // seg),
        in_specs=[
            _segment_spec(nb, seg, width), _segment_spec(nb, seg, LANES),
            pl.BlockSpec((LANES, GLA_KEY_DIM), const),
            pl.BlockSpec((1, GLA_KEY_DIM), const),
            pl.BlockSpec((1, GROUP_WIDTH), const),
        ],
        out_specs=_segment_spec(nb, seg, GROUP_WIDTH),
        out_shape=jax.ShapeDtypeStruct((b, seq, GROUP_WIDTH), BF16),
        scratch_shapes=[pltpu.VMEM((nb, GROUP_WIDTH, GLA_KEY_DIM), F32)],
        compiler_params=_cparams(("parallel", "arbitrary")),
        name="gla",
    )(gla3, small3, wg, bg, on)


def _hgrn_kernel(hg_ref, lbl_ref, on_ref, o_ref, st_ref, *, layer):
    c_len = HGRN_CHUNK
    dk = HGRN_FDIM
    seq = hg_ref.shape[1]
    _zero_at_first_segment(st_ref)
    logits = lbl_ref[...]
    e = jnp.exp(logits - jnp.max(logits, axis=0, keepdims=True))
    soft = e / jnp.sum(e, axis=0, keepdims=True)
    lb = jnp.zeros((1, dk), F32)
    for i in range(1, layer + 1):
        lb = lb + soft[i:i + 1, :]
    lb = jnp.clip(lb, 0.0, 1.0 - 1e-6)
    consts = _gla_consts(c_len, dk, GROUP_WIDTH, HGRN_SUB)

    def chunk(n, _):
        rows = pl.ds(pl.multiple_of(n * c_len, c_len), c_len)
        batch = range(hg_ref.shape[0])
        sig = [_sigmoid_pair(hg_ref[gb, rows, dk:2 * dk].astype(F32)) for gb in batch]
        logf = [jnp.log(lb + (1.0 - lb) * s[0]) for s in sig]
        k = [(1.0 - lb) * s[1] for s in sig]
        q = [hg_ref[gb, rows, 0:dk].astype(F32) for gb in batch]
        v = [hg_ref[gb, rows, 2 * dk:2 * dk + GROUP_WIDTH].astype(F32) for gb in batch]
        o, new_state = _gla_chunk(q, k, v, logf, [st_ref[gb] for gb in batch], consts)
        for gb in batch:
            st_ref[gb] = new_state[gb]
            g = hg_ref[gb, rows, 2 * dk + GROUP_WIDTH:].astype(F32)
            o_ref[gb, rows, :] = (_rms_heads64_wide(o[gb], on_ref[...]) * _silu(g)).astype(o_ref.dtype)
        return 0

    lax.fori_loop(0, seq // c_len, chunk, 0)


def _hgrn(hg3, lb_logits, on, layer):
    b, seq, width = hg3.shape
    depth = lb_logits.shape[0]
    kern = functools.partial(_hgrn_kernel, layer=layer)
    nb = min(HGRN_BATCH, b)
    seg = min(MIX_SEGMENT, seq)
    const = lambda bi, si: (0, 0)
    return pl.pallas_call(
        kern,
        grid=(b // nb, seq // seg),
        in_specs=[
            _segment_spec(nb, seg, width),
            pl.BlockSpec((depth, HGRN_FDIM), const),
            pl.BlockSpec((1, GROUP_WIDTH), const),
        ],
        out_specs=_segment_spec(nb, seg, GROUP_WIDTH),
        out_shape=jax.ShapeDtypeStruct((b, seq, GROUP_WIDTH), BF16),
        scratch_shapes=[pltpu.VMEM((nb, GROUP_WIDTH, HGRN_FDIM), F32)],
        compiler_params=_cparams(("parallel", "arbitrary")),
        name="hgrn2",
    )(hg3, lb_logits, on)


def _kv_kernel(mem_ref, mn_ref, w_ref, kn_ref, k_ref, v_ref):
    mem_n = _rms(mem_ref[...], mn_ref[...]).astype(BF16)
    kv = _dot(mem_n, w_ref[...])
    for h in range(XA_HEADS):
        cols = slice(h * XA_HEAD_DIM, (h + 1) * XA_HEAD_DIM)
        k_ref[:, cols] = _rms(kv[:, cols], kn_ref[...]).astype(k_ref.dtype)
    v_ref[...] = kv[:, D_MODEL:].astype(v_ref.dtype)


def _kv_proj(mem2d, mem_norm, w_kv, kn, tm):
    m = mem2d.shape[0]
    depth = w_kv.shape[0]
    out = jax.ShapeDtypeStruct((depth, m, D_MODEL), BF16)
    return pl.pallas_call(
        _kv_kernel,
        grid=(depth, m // tm),
        in_specs=[
            pl.BlockSpec((tm, D_MODEL), lambda l, i: (i, 0)),
            pl.BlockSpec((1, D_MODEL), lambda l, i: (0, 0)),
            pl.BlockSpec((None, D_MODEL, 2 * D_MODEL), lambda l, i: (l, 0, 0)),
            pl.BlockSpec((None, 1, XA_HEAD_DIM), lambda l, i: (l, 0, 0)),
        ],
        out_specs=[pl.BlockSpec((None, tm, D_MODEL), lambda l, i: (l, i, 0))] * 2,
        out_shape=[out, out],
        compiler_params=_cparams(("parallel", "parallel")),
        name="mem_kv",
    )(mem2d, mem_norm, w_kv, kn)


def _mix_xattn_kernel(x_ref, yf_ref, ys_ref, yg_ref, yh_ref, fon_ref, wout_ref, ln_ref, wq_ref, qn_ref,
                      k_ref, v_ref, wo_ref, o_ref):
    y_fox = _rms_heads64_wide(yf_ref[...].astype(F32), fon_ref[...]).astype(BF16)
    mixed = jnp.concatenate([y_fox, ys_ref[...], yg_ref[...], yh_ref[...]], axis=-1)
    x = x_ref[...] + _dot(mixed, wout_ref[...])
    h = _rms(x, ln_ref[...]).astype(BF16)
    q = _dot(h, wq_ref[...])
    scale = XA_HEAD_DIM ** -0.5
    outs = []
    for hd in range(XA_HEADS):
        cols = slice(hd * XA_HEAD_DIM, (hd + 1) * XA_HEAD_DIM)
        qh = (_rms(q[:, cols], qn_ref[...]) * scale).astype(BF16)
        logits = _dot_nt(qh, k_ref[:, cols])
        p = jnp.exp(logits - jnp.max(logits, axis=-1, keepdims=True))
        p = p / jnp.sum(p, axis=-1, keepdims=True)
        outs.append(_dot(p.astype(BF16), v_ref[:, cols]).astype(BF16))
    o = jnp.concatenate(outs, axis=-1)
    o_ref[...] = x + _dot(o, wo_ref[...])


def _mix_xattn(x2d, ys, fox_on, w_out, ln, wq, qn, k3, v3, wo, tm, seq):
    m = x2d.shape[0]
    n_mem = k3.shape[1]
    tiles_per_seq = seq // tm
    row = lambda i: (i, 0)
    const = lambda i: (0, 0)
    mem_spec = pl.BlockSpec((None, n_mem, D_MODEL), lambda i: (i // tiles_per_seq, 0, 0))
    return pl.pallas_call(
        _mix_xattn_kernel,
        grid=(m // tm,),
        in_specs=[
            pl.BlockSpec((tm, D_MODEL), row),
            *[pl.BlockSpec((tm, GROUP_WIDTH), row)] * 4,
            pl.BlockSpec((1, GROUP_WIDTH), const),
            _resident((D_MODEL, D_MODEL)),
            pl.BlockSpec((1, D_MODEL), const),
            _resident((D_MODEL, D_MODEL)),
            pl.BlockSpec((1, XA_HEAD_DIM), const),
            mem_spec, mem_spec,
            _resident((D_MODEL, D_MODEL)),
        ],
        out_specs=pl.BlockSpec((tm, D_MODEL), row),
        out_shape=jax.ShapeDtypeStruct((m, D_MODEL), F32),
        compiler_params=_cparams(("parallel",)),
        name="mix_xattn",
    )(x2d, *ys, fox_on, w_out, ln, wq, qn, k3, v3, wo)


def _ffn_kernel(x_ref, xp_ref, ln_ref, wup_ref, cw_ref, cb_ref, wdown_ref, o_ref, act_ref, *, tiles_per_seq):
    i = pl.program_id(0)
    x = x_ref[...]
    first = (i % tiles_per_seq) == 0
    h = _rms(x, ln_ref[...]).astype(BF16)
    hp = jnp.where(first, 0.0, _rms(xp_ref[...], ln_ref[...])).astype(BF16)
    h_ext = jnp.concatenate([hp, h], axis=0)
    for c in range(D_FF // FF_TILE):
        cols = slice(c * FF_TILE, (c + 1) * FF_TILE)
        vcols = slice(D_FF + c * FF_TILE, D_FF + (c + 1) * FF_TILE)
        gate = _dot(h_ext, wup_ref[:, cols])
        conv = cb_ref[:, cols] + cw_ref[FFN_CONV - 1:FFN_CONV, cols] * gate[HALO:, :]
        for shift in range(1, FFN_CONV):
            conv = conv + (cw_ref[FFN_CONV - 1 - shift:FFN_CONV - shift, cols]
                           * pltpu.roll(gate, shift, 0)[HALO:, :])
        val = _dot(h, wup_ref[:, vcols])
        act_ref[:, cols] = (_silu(conv) * val).astype(BF16)
    o_ref[...] = x + _dot(act_ref[...], wdown_ref[...])


def _ffn(x2d, ln, w_up, cw, cb, w_down, tm, seq):
    m = x2d.shape[0]
    kern = functools.partial(_ffn_kernel, tiles_per_seq=seq // tm)
    const = lambda i: (0, 0)
    halo_blocks = tm // HALO
    return pl.pallas_call(
        kern,
        grid=(m // tm,),
        in_specs=[
            pl.BlockSpec((tm, D_MODEL), lambda i: (i, 0)),
            pl.BlockSpec((HALO, D_MODEL), lambda i: (jnp.maximum(i * halo_blocks - 1, 0), 0)),
            pl.BlockSpec((1, D_MODEL), const),
            _resident((D_MODEL, 2 * D_FF)),
            pl.BlockSpec((FFN_CONV, D_FF), const),
            pl.BlockSpec((1, D_FF), const),
            _resident((D_FF, D_MODEL)),
        ],
        out_specs=pl.BlockSpec((tm, D_MODEL), lambda i: (i, 0)),
        out_shape=jax.ShapeDtypeStruct((m, D_MODEL), F32),
        scratch_shapes=[pltpu.VMEM((tm, D_FF), BF16)],
        compiler_params=_cparams(("parallel",)),
        name="conv_glu_ffn",
    )(x2d, x2d, ln, w_up, cw, cb, w_down)


def _pad_lanes(vec, offset, total=LANES):
    vec = vec.astype(F32).reshape(1, -1)
    return jnp.pad(vec, ((0, 0), (offset, total - offset - vec.shape[1])))


def _arrange_w_in(w):
    fq, fk, fv, ff, sz, sxbc, sdt, gq, gk, gv, ga, gr, hq, hf, hi, hg = jnp.split(
        w, [256, 512, 768, 772, 1028, 1540, 1544, 1672, 1800, 2056, 2072, 2328, 2584, 2840, 3096], axis=1)
    small = jnp.concatenate([ff, sdt, ga], axis=1)
    small = jnp.pad(small, ((0, 0), (0, LANES - small.shape[1])))
    return jnp.concatenate([fq, fk, fv, sz, sxbc, gq, gk, gv, gr, hq, hf, hi, hg, small], axis=1).astype(BF16)


def kernel(x, mem, ln_mix, w_in, w_out, fox_f_bias, fox_qn, fox_kn, fox_on, ssd_conv_w, ssd_conv_b, ssd_dt_bias,
           ssd_a_log, ssd_d, ssd_norm, gla_w_g2, gla_b_g2, gla_norm, hgrn_lb_logits, hgrn_norm, ln_xattn, mem_norm,
           xa_wq, xa_wkv, xa_wo, xa_qn, xa_kn, ln_ffn, ffn_w_up, ffn_conv_w, ffn_conv_b, ffn_w_down):
    b, seq, d = x.shape
    depth = w_in.shape[0]
    n_mem = mem.shape[1]
    m = b * seq
    tm = min(ROW_TILE, seq)
    x2d = x.reshape(m, d)

    k_all, v_all = _kv_proj(mem.reshape(b * n_mem, d), mem_norm.reshape(1, d), xa_wkv.astype(BF16),
                            xa_kn.reshape(depth, 1, XA_HEAD_DIM), min(512, b * n_mem))
    for l in range(depth):
        tile4 = lambda v: jnp.tile(v.astype(F32).reshape(1, -1), (1, GROUP_WIDTH // HEAD_DIM))
        fox, ssd, gla, hgrn, small = _in_proj(x2d, ln_mix[l].reshape(1, d), _arrange_w_in(w_in[l]),
                                              tile4(fox_qn[l]), tile4(fox_kn[l]), tm)
        small3 = small.reshape(b, seq, LANES)
        y_fox = _fox(fox.reshape(b, seq, -1), small3, _pad_lanes(fox_f_bias[l], SMALL_FF))
        y_ssd = _ssd(ssd.reshape(b, seq, -1), small3, ssd_conv_w[l], ssd_conv_b[l].reshape(1, -1),
                     _pad_lanes(ssd_dt_bias[l], SMALL_DT), _pad_lanes(ssd_a_log[l], SMALL_DT),
                     jnp.repeat(ssd_d[l].astype(F32), HEAD_DIM).reshape(1, -1), ssd_norm[l].reshape(1, -1))
        wg = jnp.pad(gla_w_g2[l], ((SMALL_GA, LANES - SMALL_GA - GLA_GATE_RANK), (0, 0))).astype(BF16)
        y_gla = _gla(gla.reshape(b, seq, -1), small3, wg, gla_b_g2[l].reshape(1, -1), tile4(gla_norm[l]))
        y_hgrn = _hgrn(hgrn.reshape(b, seq, -1), hgrn_lb_logits, tile4(hgrn_norm[l]), l)
        ys = [y.reshape(m, GROUP_WIDTH) for y in (y_fox, y_ssd, y_gla, y_hgrn)]
        x2d = _mix_xattn(x2d, ys, tile4(fox_on[l]), w_out[l].astype(BF16), ln_xattn[l].reshape(1, d),
                         xa_wq[l].astype(BF16),
                         xa_qn[l].reshape(1, -1), k_all[l].reshape(b, n_mem, d), v_all[l].reshape(b, n_mem, d),
                         xa_wo[l].astype(BF16), tm, seq)
        x2d = _ffn(x2d, ln_ffn[l].reshape(1, d), ffn_w_up[l].astype(BF16), ffn_conv_w[l],
                   ffn_conv_b[l].reshape(1, -1), ffn_w_down[l].astype(BF16), min(FFN_ROWS, seq), seq)
    return x2d.reshape(b, seq, d)
```

```python
import functools

import jax
import jax.numpy as jnp
import numpy as np
from jax import lax
from jax.experimental import pallas as pl
from jax.experimental.pallas import tpu as pltpu

F32 = jnp.float32
BF16 = jnp.bfloat16

EPS = 1e-6
MASK_VALUE = -1e30
D_MODEL = 1024
GROUP_WIDTH = 256
HEAD_DIM = 64
LANES = 128
N_HEADS = 4
SSD_STATE = 64
SSD_CONV = 4
SSD_CHUNK = 128
GLA_KEY_DIM = 128
GLA_GATE_RANK = 16
GLA_TAU = 16.0
GLA_CHUNK = 64
HGRN_FDIM = 256
HGRN_CHUNK = 64
HGRN_SUB = 16
XA_HEADS = 4
XA_HEAD_DIM = 256
D_FF = 2816
FFN_CONV = 3
FF_TILE = 256
FFN_ROWS = 1024
ROW_TILE = 1024
HALO = 16
SSD_BATCH = 4
GLA_BATCH = 8
HGRN_BATCH = 8
MIX_SEGMENT = 1024

SMALL_FF = 0
SMALL_DT = 4
SMALL_GA = 8

VMEM_LIMIT = 56 * 1024 * 1024


def _cparams(sem):
    return pltpu.CompilerParams(dimension_semantics=sem, vmem_limit_bytes=VMEM_LIMIT)


def _resident(shape):
    return pl.BlockSpec(shape, lambda *_: (0,) * len(shape), pipeline_mode=pl.Buffered(1))


def _dot(a, b, precision=None):
    return jnp.dot(a, b, preferred_element_type=F32, precision=precision)


def _dot_nt(a, b, precision=None):
    return lax.dot_general(a, b, (((1,), (1,)), ((), ())), preferred_element_type=F32, precision=precision)


def _dot_tn(a, b, precision=None):
    return lax.dot_general(a, b, (((0,), (0,)), ((), ())), preferred_element_type=F32, precision=precision)


def _rms(x, w):
    return x * lax.rsqrt(jnp.mean(x * x, axis=-1, keepdims=True) + EPS) * w


def _rms_heads64(x, w):
    lane = lax.broadcasted_iota(jnp.int32, x.shape, 1)
    lo = lane < HEAD_DIM
    sq = x * x
    s_lo = jnp.sum(jnp.where(lo, sq, 0.0), axis=-1, keepdims=True)
    s_hi = jnp.sum(jnp.where(lo, 0.0, sq), axis=-1, keepdims=True)
    ms = jnp.where(lo, s_lo, s_hi) * (1.0 / HEAD_DIM)
    return x * lax.rsqrt(ms + EPS) * w


def _rms_heads64_wide(x, w):
    return jnp.concatenate(
        [_rms_heads64(x[:, i * LANES:(i + 1) * LANES], w[:, i * LANES:(i + 1) * LANES]) for i in range(2)], axis=-1)


def _sigmoid_pair(z):
    e = jnp.exp(-jnp.abs(z))
    big = 1.0 / (1.0 + e)
    small = e * big
    pos = z >= 0
    return jnp.where(pos, big, small), jnp.where(pos, small, big)


def _silu(z):
    return z * (1.0 / (1.0 + jnp.exp(-z)))


def _log_sigmoid(z):
    return jnp.minimum(z, 0.0) - jnp.log1p(jnp.exp(-jnp.abs(z)))


def _softplus(z):
    return jnp.maximum(z, 0.0) + jnp.log1p(jnp.exp(-jnp.abs(z)))


def _iota2(shape, axis):
    return lax.broadcasted_iota(jnp.int32, shape, axis)


def _lane_col(x, lane_index):
    return x[:, lane_index:lane_index + 1]


def _masked_sums(mask16, x, pieces, form="mask_x"):
    total = None
    rest = x
    for index in range(pieces):
        part = rest.astype(BF16)
        if form == "mask_x":
            term = _dot(mask16, part)
        elif form == "mask_xt":
            term = _dot_nt(mask16, part)
        else:
            term = _dot(part, mask16)
        total = term if total is None else total + term
        if index + 1 < pieces:
            rest = rest - part.astype(F32)
    return total


IN_GROUPS = (768, 768, 768, 1024, 128)
D_IN_PAD = sum(IN_GROUPS)


def _in_proj_kernel(x_ref, ln_ref, w_ref, fqn_ref, fkn_ref, fox_ref, ssd_ref, gla_ref, hgrn_ref, small_ref):
    h = _rms(x_ref[...], ln_ref[...]).astype(BF16)
    outs = (fox_ref, ssd_ref, gla_ref, hgrn_ref, small_ref)
    start = 0
    for width, out in zip(IN_GROUPS, outs):
        proj = _dot(h, w_ref[:, start:start + width])
        if out is fox_ref:
            gw = GROUP_WIDTH
            out[:, 0:gw] = (_rms_heads64_wide(proj[:, 0:gw], fqn_ref[...]) * HEAD_DIM ** -0.5).astype(out.dtype)
            out[:, gw:2 * gw] = _rms_heads64_wide(proj[:, gw:2 * gw], fkn_ref[...]).astype(out.dtype)
            out[:, 2 * gw:] = proj[:, 2 * gw:].astype(out.dtype)
        else:
            out[...] = proj.astype(out.dtype)
        start += width


def _in_proj(x2d, ln, w_r, fox_qn, fox_kn, tm):
    m = x2d.shape[0]
    out_shape = [jax.ShapeDtypeStruct((m, n), BF16) for n in IN_GROUPS[:4]]
    out_shape.append(jax.ShapeDtypeStruct((m, IN_GROUPS[4]), F32))
    vec = pl.BlockSpec((1, GROUP_WIDTH), lambda i: (0, 0))
    return pl.pallas_call(
        _in_proj_kernel,
        grid=(m // tm,),
        in_specs=[
            pl.BlockSpec((tm, D_MODEL), lambda i: (i, 0)),
            pl.BlockSpec((1, D_MODEL), lambda i: (0, 0)),
            _resident((D_MODEL, D_IN_PAD)),
            vec, vec,
        ],
        out_specs=[pl.BlockSpec((tm, n), lambda i: (i, 0)) for n in IN_GROUPS],
        out_shape=out_shape,
        compiler_params=_cparams(("parallel",)),
        name="in_proj",
    )(x2d, ln, w_r, fox_qn, fox_kn)


FOX_TILE = 256
FOX_CUM_BLOCK = 256
N_SPLIT = 3


def _split3(c):
    hi = c.astype(BF16).astype(F32)
    r = c - hi
    mid = r.astype(BF16).astype(F32)
    lo = (r - mid).astype(BF16).astype(F32)
    return hi, mid, lo


def _fox_selectors():
    selq = np.zeros(((N_SPLIT + 1) * LANES, GROUP_WIDTH), np.float32)
    selk = np.zeros_like(selq)
    for h in range(N_HEADS):
        base = (h // 2) * LANES + HEAD_DIM * (1 - h % 2)
        for p in range(N_SPLIT):
            selq[p * LANES + SMALL_FF + h, base + p] = 1.0
            selq[N_SPLIT * LANES, base + N_SPLIT + p] = 1.0
            selk[N_SPLIT * LANES, base + p] = 1.0
            selk[p * LANES + SMALL_FF + h, base + N_SPLIT + p] = -1.0
    return jnp.asarray(selq, BF16), jnp.asarray(selk, BF16)


def _fox_kernel(q_ref, k_ref, v_ref, small_ref, bias_ref, selq_ref, selk_ref, o_ref,
                ka_ref, qaug_ref, va_ref, *, seq):
    qi = pl.program_id(1)
    tq = FOX_TILE
    halves = GROUP_WIDTH // LANES

    def own_lanes(lane, hh):
        return (lane >= HEAD_DIM * hh) & (lane < HEAD_DIM * (hh + 1))

    @pl.when(qi == 0)
    def _prepare():
        nblk = seq // FOX_CUM_BLOCK
        tri = (_iota2((FOX_CUM_BLOCK, FOX_CUM_BLOCK), 0) >= _iota2((FOX_CUM_BLOCK, FOX_CUM_BLOCK), 1)).astype(BF16)
        lane = _iota2((FOX_CUM_BLOCK, LANES), 1)
        blocks = [pl.ds(blk * FOX_CUM_BLOCK, FOX_CUM_BLOCK) for blk in range(nblk)]
        local = [_masked_sums(tri, _log_sigmoid(small_ref[rows, :] + bias_ref[...]), 3) for rows in blocks]
        offset = jnp.zeros((1, LANES), F32)
        cum = []
        for c_local in local:
            cum.append(c_local + offset)
            offset = offset + c_local[FOX_CUM_BLOCK - 1:FOX_CUM_BLOCK, :]
        pieces = [jnp.concatenate([*_split3(c), jnp.ones_like(c)], axis=-1).astype(BF16) for c in cum]
        k_aug = [_dot(x, selk_ref[...]).astype(BF16) for x in pieces]
        q_aug = [_dot(x, selq_ref[...]).astype(BF16) for x in pieces]
        for rows, k_aug_blk, q_aug_blk in zip(blocks, k_aug, q_aug):
            qaug_ref[rows, :] = q_aug_blk
            for half in range(halves):
                lanes = slice(half * LANES, (half + 1) * LANES)
                kn = k_ref[rows, lanes]
                v_half = v_ref[rows, lanes]
                for hh in range(2):
                    head = 2 * half + hh
                    own = own_lanes(lane, hh)
                    va_ref[head, rows, :] = jnp.where(own, v_half, jnp.ones_like(v_half))
                    ka_ref[head, rows, :] = jnp.where(own, kn, k_aug_blk[:, lanes])

    qrows = pl.ds(pl.multiple_of(qi * tq, tq), tq)
    lane = _iota2((tq, LANES), 1)
    lo_half = lane < HEAD_DIM
    qas = []
    for half in range(halves):
        lanes = slice(half * LANES, (half + 1) * LANES)
        for hh in range(2):
            qas.append(jnp.where(own_lanes(lane, hh), q_ref[:, lanes], qaug_ref[qrows, lanes]))

    heads = range(N_HEADS)

    causal = _iota2((tq, tq), 0) >= _iota2((tq, tq), 1)

    def attend(n_blocks):
        past = (n_blocks - 1) * tq
        own = slice(past, past + tq)
        s_own = [jnp.where(causal, _dot_nt(qas[head], ka_ref[head, own, :]), MASK_VALUE) for head in heads]
        m = [jnp.max(x, axis=-1, keepdims=True) for x in s_own]
        if past:
            s_past = [_dot_nt(qas[head], ka_ref[head, 0:past, :]) for head in heads]
            m = _each(lambda a, x: jnp.maximum(a, jnp.max(x, axis=-1, keepdims=True)), m, s_past)
        acc = [_dot(jnp.exp(s_own[head] - m[head]).astype(BF16), va_ref[head, own, :]) for head in heads]
        if past:
            p_past = _each(lambda x, a: jnp.exp(x - a).astype(BF16), s_past, m)
            acc = [acc[head] + _dot(p_past[head], va_ref[head, 0:past, :]) for head in heads]
        for half in range(halves):
            acc_lo, acc_hi = acc[2 * half], acc[2 * half + 1]
            o = jnp.where(lo_half, acc_lo / pltpu.roll(acc_lo, HEAD_DIM, 1),
                          acc_hi / pltpu.roll(acc_hi, HEAD_DIM, 1))
            o_ref[:, half * LANES:(half + 1) * LANES] = o.astype(o_ref.dtype)
        return 0

    lax.switch(qi, [functools.partial(attend, n + 1) for n in range(seq // tq)])


def _fox(fox3, small3, bias):
    b, seq, _ = fox3.shape
    tq = FOX_TILE
    kern = functools.partial(_fox_kernel, seq=seq)
    selq, selk = _fox_selectors()
    vec = pl.BlockSpec((1, LANES), lambda bi, qi: (0, 0))
    return pl.pallas_call(
        kern,
        grid=(b, seq // tq),
        in_specs=[
            pl.BlockSpec((None, tq, GROUP_WIDTH), lambda bi, qi: (bi, qi, 0)),
            pl.BlockSpec((None, seq, GROUP_WIDTH), lambda bi, qi: (bi, 0, 1)),
            pl.BlockSpec((None, seq, GROUP_WIDTH), lambda bi, qi: (bi, 0, 2)),
            pl.BlockSpec((None, seq, LANES), lambda bi, qi: (bi, 0, 0)),
            vec,
            _resident(selq.shape), _resident(selk.shape),
        ],
        out_specs=pl.BlockSpec((None, tq, GROUP_WIDTH), lambda bi, qi: (bi, qi, 0)),
        out_shape=jax.ShapeDtypeStruct((b, seq, GROUP_WIDTH), BF16),
        scratch_shapes=[pltpu.VMEM((N_HEADS, seq, LANES), BF16), pltpu.VMEM((seq, GROUP_WIDTH), BF16),
                        pltpu.VMEM((N_HEADS, seq, LANES), BF16)],
        compiler_params=_cparams(("parallel", "arbitrary")),
        name="fox",
    )(fox3, fox3, fox3, small3, bias, selq, selk)


def _ssd_kernel(ssd_ref, small_ref, cw_ref, cb_ref, dtb_ref, alog_ref, dskip_ref, gn_ref, o_ref,
                st_ref, *, seq):
    c_len = SSD_CHUNK
    assert c_len == LANES
    n_batch = ssd_ref.shape[0]
    st_ref[...] = jnp.zeros_like(st_ref)

    a_lane = -jnp.exp(alog_ref[...])
    tri = (_iota2((c_len, c_len), 0) >= _iota2((c_len, c_len), 1))
    tri16 = tri.astype(BF16)
    sel16 = (_iota2((8, LANES), 0) + SMALL_DT == _iota2((8, LANES), 1)).astype(BF16)
    lane = _iota2((c_len, LANES), 1)
    lo = lane < HEAD_DIM

    def chunk(n, _):
        _round_robin([chunk_one(n, gb) for gb in range(n_batch)])
        return 0

    def chunk_one(n, gb):
        rows = pl.ds(pl.multiple_of(n * c_len, c_len), c_len)
        halo_rows = pl.ds(pl.multiple_of(jnp.maximum(n * c_len - HALO, 0), HALO), HALO)
        halo = jnp.where(n > 0, ssd_ref[gb, halo_rows, GROUP_WIDTH:].astype(F32), 0.0)
        cur = ssd_ref[gb, rows, GROUP_WIDTH:].astype(F32)
        ext = jnp.concatenate([halo, cur], axis=0)
        conv = cb_ref[...] + cw_ref[SSD_CONV - 1:SSD_CONV, :] * cur
        for shift in range(1, SSD_CONV):
            conv = conv + cw_ref[SSD_CONV - 1 - shift:SSD_CONV - shift, :] * pltpu.roll(ext, shift, 0)[HALO:, :]
        xc = _silu(conv)
        xs = xc[:, 0:GROUP_WIDTH]
        bm = xc[:, GROUP_WIDTH:GROUP_WIDTH + LANES]
        cm = xc[:, GROUP_WIDTH + LANES:]
        dt = _softplus(small_ref[gb, rows, :] + dtb_ref[...])
        a_cs = _masked_sums(tri16, dt * a_lane, 3)
        yield
        a_row = _masked_sums(sel16, a_cs, 3, form="mask_xt")
        cm16 = cm.astype(BF16)
        y_halves = []
        for g in range(2):
            in_group = (lane >= SSD_STATE * g) & (lane < SSD_STATE * (g + 1))
            b_g = jnp.where(in_group, bm, 0.0).astype(BF16)
            scores = _dot_nt(cm16, b_g)
            yield
            heads = (2 * g, 2 * g + 1)
            a_col = [jnp.broadcast_to(_lane_col(a_cs, SMALL_DT + h), (c_len, LANES)) for h in heads]
            dt_col = [_lane_col(dt, SMALL_DT + h) for h in heads]
            a_own = jnp.where(lo, a_col[0], a_col[1])
            a_last = a_own[c_len - 1:c_len, :]
            xdt = xs[:, g * LANES:(g + 1) * LANES] * jnp.where(lo, dt_col[0], dt_col[1])
            xdt16 = xdt.astype(BF16)
            y_intra = []
            for index, h in enumerate(heads):
                seg = a_col[index] - a_row[h:h + 1, :]
                decay = jnp.where(tri, jnp.exp(jnp.where(tri, seg, 0.0)), 0.0)
                y_intra.append(_dot((scores * decay).astype(BF16), xdt16))
            state = st_ref[gb, g]
            y_state = _dot(cm16, state.astype(BF16))
            d_state = _dot_tn(b_g, (xdt * jnp.exp(a_last - a_own)).astype(BF16))
            yield
            y_halves.append(jnp.where(lo, y_intra[0], y_intra[1]) + y_state * jnp.exp(a_own))
            st_ref[gb, g] = state * jnp.exp(a_last) + d_state
        y = jnp.concatenate(y_halves, axis=-1) + xs * dskip_ref[...]
        y = y * _silu(ssd_ref[gb, rows, 0:GROUP_WIDTH].astype(F32))
        out = jnp.concatenate([_rms(y[:, i * LANES:(i + 1) * LANES], gn_ref[:, i * LANES:(i + 1) * LANES])
                               for i in range(2)], axis=-1)
        o_ref[gb, rows, :] = out.astype(o_ref.dtype)

    lax.fori_loop(0, seq // c_len, chunk, 0)


def _batch_spec(n_batch, seq, width):
    return pl.BlockSpec((n_batch, seq, width), lambda bi: (bi, 0, 0))


def _segment_spec(n_batch, seg, width):
    return pl.BlockSpec((n_batch, seg, width), lambda bi, si: (bi, si, 0))


def _zero_at_first_segment(state_ref):
    @pl.when(pl.program_id(1) == 0)
    def _():
        state_ref[...] = jnp.zeros_like(state_ref)


def _ssd(ssd3, small3, cw, cb, dtb, alog, dskip, gn):
    b, seq, width = ssd3.shape
    conv_dim = GROUP_WIDTH + 2 * LANES
    kern = functools.partial(_ssd_kernel, seq=seq)
    nb = min(SSD_BATCH, b)

    def vec(n, rows=1):
        return pl.BlockSpec((rows, n), lambda bi: (0, 0))

    return pl.pallas_call(
        kern,
        grid=(b // nb,),
        in_specs=[
            _batch_spec(nb, seq, width), _batch_spec(nb, seq, LANES),
            vec(conv_dim, SSD_CONV), vec(conv_dim), vec(LANES), vec(LANES), vec(GROUP_WIDTH), vec(GROUP_WIDTH),
        ],
        out_specs=_batch_spec(nb, seq, GROUP_WIDTH),
        out_shape=jax.ShapeDtypeStruct((b, seq, GROUP_WIDTH), BF16),
        scratch_shapes=[pltpu.VMEM((nb, 2, LANES, LANES), F32)],
        compiler_params=_cparams(("parallel",)),
        name="ssd",
    )(ssd3, small3, cw, cb, dtb, alog, dskip, gn)


def _gla_levels(c_len, sub):
    sizes = []
    size = sub
    while size <= c_len:
        sizes.append(size)
        size *= 2
    return sizes


def _gla_consts(c_len, dk, dv, sub):
    dkh = dk // N_HEADS
    dvh = dv // N_HEADS
    sizes = _gla_levels(c_len, sub)
    row = _iota2((c_len, c_len), 0)
    col = _iota2((c_len, c_len), 1)
    sum_masks = []
    for size in sizes:
        same = (row // size) == (col // size)
        sum_masks.append((same & (col <= row)).astype(BF16))
        sum_masks.append(same.astype(BF16))
    arow = _iota2((c_len, N_HEADS * c_len), 0)
    acol = _iota2((c_len, N_HEADS * c_len), 1) % c_len
    keep = [((arow // sub) == (acol // sub)) & (acol <= arow)]
    for level in range(1, len(sizes)):
        size, half = sizes[level], sizes[level - 1]
        keep.append(((arow // size) == (acol // size)) & ((arow % size) >= half) & ((acol % size) < half))
    klane = (_iota2((c_len, dk), 1) // dkh).astype(BF16)
    vlane = (_iota2((c_len, dv), 1) // dvh).astype(BF16)
    diag = (_iota2((dv, dk), 0) // dvh) == (_iota2((dv, dk), 1) // dkh)
    sum_mask = jnp.concatenate(sum_masks, axis=0)
    return dict(sizes=sizes, sum_mask=jnp.concatenate([sum_mask, sum_mask], axis=1), keep=keep,
                klane=klane, vlane=vlane, diag=diag)


def _each(fn, *lists):
    return [fn(*args) for args in zip(*lists)]


def _round_robin(generators):
    live = list(generators)
    while live:
        still = []
        for gen in live:
            try:
                next(gen)
                still.append(gen)
            except StopIteration:
                pass
        live = still


def _gla_chunk(q, k, v, logf, state_t, consts):
    c_len = q[0].shape[0]
    sizes = consts["sizes"]
    n_levels = len(sizes)
    def prefix_sums(x):
        hi = x.astype(BF16)
        lo = (x - hi.astype(F32)).astype(BF16)
        return _dot(consts["sum_mask"], jnp.concatenate([hi, lo], axis=0))

    sums = _each(prefix_sums, logf)
    cs = [[s[(2 * i) * c_len:(2 * i + 1) * c_len] for i in range(n_levels)] for s in sums]
    tot = [[s[(2 * i + 1) * c_len:(2 * i + 2) * c_len] for i in range(n_levels)] for s in sums]

    def stack_heads(x, lane_head):
        x16 = x.astype(BF16)
        return jnp.concatenate([jnp.where(lane_head == h, x16, jnp.zeros_like(x16)) for h in range(N_HEADS)], axis=0)

    q0 = _each(lambda x, c: (x * jnp.exp(c[0])).astype(BF16), q, cs)
    k0 = _each(lambda x, c: stack_heads(x * jnp.exp(-c[0]), consts["klane"]), k, cs)
    att = _each(lambda a, b: jnp.where(consts["keep"][0], _dot_nt(a, b), 0.0), q0, k0)
    for level in range(1, n_levels):
        q_l = q0 if level == 1 else _each(lambda x, c: (x * jnp.exp(c[level - 1])).astype(BF16), q, cs)
        k_l = _each(lambda x, c, t: stack_heads(x * jnp.exp(t[level - 1] - c[level - 1]), consts["klane"]), k, cs, tot)
        att = _each(lambda a, b, prev: jnp.where(consts["keep"][level], _dot_nt(a, b), prev), q_l, k_l, att)
    v_stack = _each(lambda x: stack_heads(x, consts["vlane"]), v)
    o = _each(lambda a, b: _dot(a.astype(BF16), b), att, v_stack)

    q_s = _each(lambda x, c: (x * jnp.exp(c[-1])).astype(BF16), q, cs)
    k_s = _each(lambda x, c, t: (x * jnp.exp(t[-1] - c[-1])).astype(BF16), k, cs, tot)
    o = _each(lambda acc, a, s: acc + _dot_nt(a, s.astype(BF16)), o, q_s, state_t)
    d_state = _each(lambda x, y: jnp.where(consts["diag"], _dot_tn(x.astype(BF16), y), 0.0), v, k_s)
    new_state = _each(lambda s, t, d: s * jnp.exp(t[-1][0:1, :]) + d, state_t, tot, d_state)
    return o, new_state


def _gla_kernel(gla_ref, small_ref, wg_ref, bg_ref, on_ref, o_ref, st_ref):
    c_len = GLA_CHUNK
    dk = GLA_KEY_DIM
    seq = gla_ref.shape[1]
    _zero_at_first_segment(st_ref)
    scale = (dk // N_HEADS) ** -0.5
    consts = _gla_consts(c_len, dk, GROUP_WIDTH, c_len)

    def chunk(n, _):
        rows = pl.ds(pl.multiple_of(n * c_len, c_len), c_len)
        batch = range(gla_ref.shape[0])
        gate = [_dot(small_ref[gb, rows, :].astype(BF16), wg_ref[...]) + bg_ref[...] for gb in batch]
        logf = [_log_sigmoid(x) * (1.0 / GLA_TAU) for x in gate]
        q = [gla_ref[gb, rows, 0:dk].astype(F32) * scale for gb in batch]
        k = [gla_ref[gb, rows, dk:2 * dk].astype(F32) for gb in batch]
        v = [gla_ref[gb, rows, 2 * dk:2 * dk + GROUP_WIDTH].astype(F32) for gb in batch]
        o, new_state = _gla_chunk(q, k, v, logf, [st_ref[gb] for gb in batch], consts)
        for gb in batch:
            st_ref[gb] = new_state[gb]
            r = gla_ref[gb, rows, 2 * dk + GROUP_WIDTH:].astype(F32)
            o_ref[gb, rows, :] = (_rms_heads64_wide(o[gb], on_ref[...]) * _silu(r)).astype(o_ref.dtype)
        return 0

    lax.fori_loop(0, seq // c_len, chunk, 0)


def _gla(gla3, small3, wg, bg, on):
    b, seq, width = gla3.shape
    nb = min(GLA_BATCH, b)
    seg = min(MIX_SEGMENT, seq)
    const = lambda bi, si: (0, 0)
    return pl.pallas_call(
        _gla_kernel,
        grid=(b // nb, seq // seg),
        in_specs=[
            _segment_spec(nb, seg, width), _segment_spec(nb, seg, LANES),
            pl.BlockSpec((LANES, GLA_KEY_DIM), const),
            pl.BlockSpec((1, GLA_KEY_DIM), const),
            pl.BlockSpec((1, GROUP_WIDTH), const),
        ],
        out_specs=_segment_spec(nb, seg, GROUP_WIDTH),
        out_shape=jax.ShapeDtypeStruct((b, seq, GROUP_WIDTH), BF16),
        scratch_shapes=[pltpu.VMEM((nb, GROUP_WIDTH, GLA_KEY_DIM), F32)],
        compiler_params=_cparams(("parallel", "arbitrary")),
        name="gla",
    )(gla3, small3, wg, bg, on)


def _hgrn_kernel(hg_ref, lbl_ref, on_ref, o_ref, st_ref, *, layer):
    c_len = HGRN_CHUNK
    dk = HGRN_FDIM
    seq = hg_ref.shape[1]
    _zero_at_first_segment(st_ref)
    logits = lbl_ref[...]
    e = jnp.exp(logits - jnp.max(logits, axis=0, keepdims=True))
    soft = e / jnp.sum(e, axis=0, keepdims=True)
    lb = jnp.zeros((1, dk), F32)
    for i in range(1, layer + 1):
        lb = lb + soft[i:i + 1, :]
    lb = jnp.clip(lb, 0.0, 1.0 - 1e-6)
    consts = _gla_consts(c_len, dk, GROUP_WIDTH, HGRN_SUB)

    def chunk(n, _):
        rows = pl.ds(pl.multiple_of(n * c_len, c_len), c_len)
        batch = range(hg_ref.shape[0])
        sig = [_sigmoid_pair(hg_ref[gb, rows, dk:2 * dk].astype(F32)) for gb in batch]
        logf = [jnp.log(lb + (1.0 - lb) * s[0]) for s in sig]
        k = [(1.0 - lb) * s[1] for s in sig]
        q = [hg_ref[gb, rows, 0:dk].astype(F32) for gb in batch]
        v = [hg_ref[gb, rows, 2 * dk:2 * dk + GROUP_WIDTH].astype(F32) for gb in batch]
        o, new_state = _gla_chunk(q, k, v, logf, [st_ref[gb] for gb in batch], consts)
        for gb in batch:
            st_ref[gb] = new_state[gb]
            g = hg_ref[gb, rows, 2 * dk + GROUP_WIDTH:].astype(F32)
            o_ref[gb, rows, :] = (_rms_heads64_wide(o[gb], on_ref[...]) * _silu(g)).astype(o_ref.dtype)
        return 0

    lax.fori_loop(0, seq // c_len, chunk, 0)


def _hgrn(hg3, lb_logits, on, layer):
    b, seq, width = hg3.shape
    depth = lb_logits.shape[0]
    kern = functools.partial(_hgrn_kernel, layer=layer)
    nb = min(HGRN_BATCH, b)
    seg = min(MIX_SEGMENT, seq)
    const = lambda bi, si: (0, 0)
    return pl.pallas_call(
        kern,
        grid=(b // nb, seq // seg),
        in_specs=[
            _segment_spec(nb, seg, width),
            pl.BlockSpec((depth, HGRN_FDIM), const),
            pl.BlockSpec((1, GROUP_WIDTH), const),
        ],
        out_specs=_segment_spec(nb, seg, GROUP_WIDTH),
        out_shape=jax.ShapeDtypeStruct((b, seq, GROUP_WIDTH), BF16),
        scratch_shapes=[pltpu.VMEM((nb, GROUP_WIDTH, HGRN_FDIM), F32)],
        compiler_params=_cparams(("parallel", "arbitrary")),
        name="hgrn2",
    )(hg3, lb_logits, on)


def _kv_kernel(mem_ref, mn_ref, w_ref, kn_ref, k_ref, v_ref):
    mem_n = _rms(mem_ref[...], mn_ref[...]).astype(BF16)
    kv = _dot(mem_n, w_ref[...])
    for h in range(XA_HEADS):
        cols = slice(h * XA_HEAD_DIM, (h + 1) * XA_HEAD_DIM)
        k_ref[:, cols] = _rms(kv[:, cols], kn_ref[...]).astype(k_ref.dtype)
    v_ref[...] = kv[:, D_MODEL:].astype(v_ref.dtype)


def _kv_proj(mem2d, mem_norm, w_kv, kn, tm):
    m = mem2d.shape[0]
    depth = w_kv.shape[0]
    out = jax.ShapeDtypeStruct((depth, m, D_MODEL), BF16)
    return pl.pallas_call(
        _kv_kernel,
        grid=(depth, m // tm),
        in_specs=[
            pl.BlockSpec((tm, D_MODEL), lambda l, i: (i, 0)),
            pl.BlockSpec((1, D_MODEL), lambda l, i: (0, 0)),
            pl.BlockSpec((None, D_MODEL, 2 * D_MODEL), lambda l, i: (l, 0, 0)),
            pl.BlockSpec((None, 1, XA_HEAD_DIM), lambda l, i: (l, 0, 0)),
        ],
        out_specs=[pl.BlockSpec((None, tm, D_MODEL), lambda l, i: (l, i, 0))] * 2,
        out_shape=[out, out],
        compiler_params=_cparams(("parallel", "parallel")),
        name="mem_kv",
    )(mem2d, mem_norm, w_kv, kn)


def _mix_xattn_kernel(x_ref, yf_ref, ys_ref, yg_ref, yh_ref, fon_ref, wout_ref, ln_ref, wq_ref, qn_ref,
                      k_ref, v_ref, wo_ref, o_ref):
    y_fox = _rms_heads64_wide(yf_ref[...].astype(F32), fon_ref[...]).astype(BF16)
    mixed = jnp.concatenate([y_fox, ys_ref[...], yg_ref[...], yh_ref[...]], axis=-1)
    x = x_ref[...] + _dot(mixed, wout_ref[...])
    h = _rms(x, ln_ref[...]).astype(BF16)
    q = _dot(h, wq_ref[...])
    scale = XA_HEAD_DIM ** -0.5
    outs = []
    for hd in range(XA_HEADS):
        cols = slice(hd * XA_HEAD_DIM, (hd + 1) * XA_HEAD_DIM)
        qh = (_rms(q[:, cols], qn_ref[...]) * scale).astype(BF16)
        logits = _dot_nt(qh, k_ref[:, cols])
        p = jnp.exp(logits - jnp.max(logits, axis=-1, keepdims=True))
        p = p / jnp.sum(p, axis=-1, keepdims=True)
        outs.append(_dot(p.astype(BF16), v_ref[:, cols]).astype(BF16))
    o = jnp.concatenate(outs, axis=-1)
    o_ref[...] = x + _dot(o, wo_ref[...])


def _mix_xattn(x2d, ys, fox_on, w_out, ln, wq, qn, k3, v3, wo, tm, seq):
    m = x2d.shape[0]
    n_mem = k3.shape[1]
    tiles_per_seq = seq // tm
    row = lambda i: (i, 0)
    const = lambda i: (0, 0)
    mem_spec = pl.BlockSpec((None, n_mem, D_MODEL), lambda i: (i // tiles_per_seq, 0, 0))
    return pl.pallas_call(
        _mix_xattn_kernel,
        grid=(m // tm,),
        in_specs=[
            pl.BlockSpec((tm, D_MODEL), row),
            *[pl.BlockSpec((tm, GROUP_WIDTH), row)] * 4,
            pl.BlockSpec((1, GROUP_WIDTH), const),
            _resident((D_MODEL, D_MODEL)),
            pl.BlockSpec((1, D_MODEL), const),
            _resident((D_MODEL, D_MODEL)),
            pl.BlockSpec((1, XA_HEAD_DIM), const),
            mem_spec, mem_spec,
            _resident((D_MODEL, D_MODEL)),
        ],
        out_specs=pl.BlockSpec((tm, D_MODEL), row),
        out_shape=jax.ShapeDtypeStruct((m, D_MODEL), F32),
        compiler_params=_cparams(("parallel",)),
        name="mix_xattn",
    )(x2d, *ys, fox_on, w_out, ln, wq, qn, k3, v3, wo)


def _ffn_kernel(x_ref, xp_ref, ln_ref, wup_ref, cw_ref, cb_ref, wdown_ref, o_ref, act_ref, *, tiles_per_seq):
    i = pl.program_id(0)
    x = x_ref[...]
    first = (i % tiles_per_seq) == 0
    h = _rms(x, ln_ref[...]).astype(BF16)
    hp = jnp.where(first, 0.0, _rms(xp_ref[...], ln_ref[...])).astype(BF16)
    h_ext = jnp.concatenate([hp, h], axis=0)
    for c in range(D_FF // FF_TILE):
        cols = slice(c * FF_TILE, (c + 1) * FF_TILE)
        vcols = slice(D_FF + c * FF_TILE, D_FF + (c + 1) * FF_TILE)
        gate = _dot(h_ext, wup_ref[:, cols])
        conv = cb_ref[:, cols] + cw_ref[FFN_CONV - 1:FFN_CONV, cols] * gate[HALO:, :]
        for shift in range(1, FFN_CONV):
            conv = conv + (cw_ref[FFN_CONV - 1 - shift:FFN_CONV - shift, cols]
                           * pltpu.roll(gate, shift, 0)[HALO:, :])
        val = _dot(h, wup_ref[:, vcols])
        act_ref[:, cols] = (_silu(conv) * val).astype(BF16)
    o_ref[...] = x + _dot(act_ref[...], wdown_ref[...])


def _ffn(x2d, ln, w_up, cw, cb, w_down, tm, seq):
    m = x2d.shape[0]
    kern = functools.partial(_ffn_kernel, tiles_per_seq=seq // tm)
    const = lambda i: (0, 0)
    halo_blocks = tm // HALO
    return pl.pallas_call(
        kern,
        grid=(m // tm,),
        in_specs=[
            pl.BlockSpec((tm, D_MODEL), lambda i: (i, 0)),
            pl.BlockSpec((HALO, D_MODEL), lambda i: (jnp.maximum(i * halo_blocks - 1, 0), 0)),
            pl.BlockSpec((1, D_MODEL), const),
            _resident((D_MODEL, 2 * D_FF)),
            pl.BlockSpec((FFN_CONV, D_FF), const),
            pl.BlockSpec((1, D_FF), const),
            _resident((D_FF, D_MODEL)),
        ],
        out_specs=pl.BlockSpec((tm, D_MODEL), lambda i: (i, 0)),
        out_shape=jax.ShapeDtypeStruct((m, D_MODEL), F32),
        scratch_shapes=[pltpu.VMEM((tm, D_FF), BF16)],
        compiler_params=_cparams(("parallel",)),
        name="conv_glu_ffn",
    )(x2d, x2d, ln, w_up, cw, cb, w_down)


def _pad_lanes(vec, offset, total=LANES):
    vec = vec.astype(F32).reshape(1, -1)
    return jnp.pad(vec, ((0, 0), (offset, total - offset - vec.shape[1])))


def _arrange_w_in(w):
    fq, fk, fv, ff, sz, sxbc, sdt, gq, gk, gv, ga, gr, hq, hf, hi, hg = jnp.split(
        w, [256, 512, 768, 772, 1028, 1540, 1544, 1672, 1800, 2056, 2072, 2328, 2584, 2840, 3096], axis=1)
    small = jnp.concatenate([ff, sdt, ga], axis=1)
    small = jnp.pad(small, ((0, 0), (0, LANES - small.shape[1])))
    return jnp.concatenate([fq, fk, fv, sz, sxbc, gq, gk, gv, gr, hq, hf, hi, hg, small], axis=1).astype(BF16)


def kernel(x, mem, ln_mix, w_in, w_out, fox_f_bias, fox_qn, fox_kn, fox_on, ssd_conv_w, ssd_conv_b, ssd_dt_bias,
           ssd_a_log, ssd_d, ssd_norm, gla_w_g2, gla_b_g2, gla_norm, hgrn_lb_logits, hgrn_norm, ln_xattn, mem_norm,
           xa_wq, xa_wkv, xa_wo, xa_qn, xa_kn, ln_ffn, ffn_w_up, ffn_conv_w, ffn_conv_b, ffn_w_down):
    b, seq, d = x.shape
    depth = w_in.shape[0]
    n_mem = mem.shape[1]
    m = b * seq
    tm = min(ROW_TILE, seq)
    x2d = x.reshape(m, d)

    k_all, v_all = _kv_proj(mem.reshape(b * n_mem, d), mem_norm.reshape(1, d), xa_wkv.astype(BF16),
                            xa_kn.reshape(depth, 1, XA_HEAD_DIM), min(512, b * n_mem))
    for l in range(depth):
        tile4 = lambda v: jnp.tile(v.astype(F32).reshape(1, -1), (1, GROUP_WIDTH // HEAD_DIM))
        fox, ssd, gla, hgrn, small = _in_proj(x2d, ln_mix[l].reshape(1, d), _arrange_w_in(w_in[l]),
                                              tile4(fox_qn[l]), tile4(fox_kn[l]), tm)
        small3 = small.reshape(b, seq, LANES)
        y_fox = _fox(fox.reshape(b, seq, -1), small3, _pad_lanes(fox_f_bias[l], SMALL_FF))
        y_ssd = _ssd(ssd.reshape(b, seq, -1), small3, ssd_conv_w[l], ssd_conv_b[l].reshape(1, -1),
                     _pad_lanes(ssd_dt_bias[l], SMALL_DT), _pad_lanes(ssd_a_log[l], SMALL_DT),
                     jnp.repeat(ssd_d[l].astype(F32), HEAD_DIM).reshape(1, -1), ssd_norm[l].reshape(1, -1))
        wg = jnp.pad(gla_w_g2[l], ((SMALL_GA, LANES - SMALL_GA - GLA_GATE_RANK), (0, 0))).astype(BF16)
        y_gla = _gla(gla.reshape(b, seq, -1), small3, wg, gla_b_g2[l].reshape(1, -1), tile4(gla_norm[l]))
        y_hgrn = _hgrn(hgrn.reshape(b, seq, -1), hgrn_lb_logits, tile4(hgrn_norm[l]), l)
        ys = [y.reshape(m, GROUP_WIDTH) for y in (y_fox, y_ssd, y_gla, y_hgrn)]
        x2d = _mix_xattn(x2d, ys, tile4(fox_on[l]), w_out[l].astype(BF16), ln_xattn[l].reshape(1, d),
                         xa_wq[l].astype(BF16),
                         xa_qn[l].reshape(1, -1), k_all[l].reshape(b, n_mem, d), v_all[l].reshape(b, n_mem, d),
                         xa_wo[l].astype(BF16), tm, seq)
        x2d = _ffn(x2d, ln_ffn[l].reshape(1, d), ffn_w_up[l].astype(BF16), ffn_conv_w[l],
                   ffn_conv_b[l].reshape(1, -1), ffn_w_down[l].astype(BF16), min(FFN_ROWS, seq), seq)
    return x2d.reshape(b, seq, d)
```

```python
import functools

import jax
import jax.numpy as jnp
import numpy as np
from jax import lax
from jax.experimental import pallas as pl
from jax.experimental.pallas import tpu as pltpu

F32 = jnp.float32
BF16 = jnp.bfloat16

EPS = 1e-6
MASK_VALUE = -1e30
D_MODEL = 1024
GROUP_WIDTH = 256
HEAD_DIM = 64
LANES = 128
N_HEADS = 4
SSD_STATE = 64
SSD_CONV = 4
SSD_CHUNK = 128
GLA_KEY_DIM = 128
GLA_GATE_RANK = 16
GLA_TAU = 16.0
GLA_CHUNK = 64
HGRN_FDIM = 256
HGRN_CHUNK = 64
HGRN_SUB = 16
XA_HEADS = 4
XA_HEAD_DIM = 256
D_FF = 2816
FFN_CONV = 3
FF_TILE = 256
FFN_ROWS = 1024
ROW_TILE = 1024
HALO = 16
SSD_BATCH = 4
GLA_BATCH = 8
HGRN_BATCH = 8
GLA_SEGMENT = 1024
HGRN_SEGMENT = 512

SMALL_FF = 0
SMALL_DT = 4
SMALL_GA = 8

VMEM_LIMIT = 56 * 1024 * 1024


def _cparams(sem):
    return pltpu.CompilerParams(dimension_semantics=sem, vmem_limit_bytes=VMEM_LIMIT)


def _resident(shape):
    return pl.BlockSpec(shape, lambda *_: (0,) * len(shape), pipeline_mode=pl.Buffered(1))


def _dot(a, b, precision=None):
    return jnp.dot(a, b, preferred_element_type=F32, precision=precision)


def _dot_nt(a, b, precision=None):
    return lax.dot_general(a, b, (((1,), (1,)), ((), ())), preferred_element_type=F32, precision=precision)


def _dot_tn(a, b, precision=None):
    return lax.dot_general(a, b, (((0,), (0,)), ((), ())), preferred_element_type=F32, precision=precision)


def _rms(x, w):
    return x * lax.rsqrt(jnp.mean(x * x, axis=-1, keepdims=True) + EPS) * w


def _rms_heads64(x, w):
    lane = lax.broadcasted_iota(jnp.int32, x.shape, 1)
    lo = lane < HEAD_DIM
    sq = x * x
    s_lo = jnp.sum(jnp.where(lo, sq, 0.0), axis=-1, keepdims=True)
    s_hi = jnp.sum(jnp.where(lo, 0.0, sq), axis=-1, keepdims=True)
    ms = jnp.where(lo, s_lo, s_hi) * (1.0 / HEAD_DIM)
    return x * lax.rsqrt(ms + EPS) * w


def _rms_heads64_wide(x, w):
    return jnp.concatenate(
        [_rms_heads64(x[:, i * LANES:(i + 1) * LANES], w[:, i * LANES:(i + 1) * LANES]) for i in range(2)], axis=-1)


def _sigmoid_pair(z):
    e = jnp.exp(-jnp.abs(z))
    big = 1.0 / (1.0 + e)
    small = e * big
    pos = z >= 0
    return jnp.where(pos, big, small), jnp.where(pos, small, big)


def _silu(z):
    return z * (1.0 / (1.0 + jnp.exp(-z)))


def _log_sigmoid(z):
    return jnp.minimum(z, 0.0) - jnp.log1p(jnp.exp(-jnp.abs(z)))


def _softplus(z):
    return jnp.maximum(z, 0.0) + jnp.log1p(jnp.exp(-jnp.abs(z)))


def _iota2(shape, axis):
    return lax.broadcasted_iota(jnp.int32, shape, axis)


def _lane_col(x, lane_index):
    return x[:, lane_index:lane_index + 1]


def _masked_sums(mask16, x, pieces, form="mask_x"):
    total = None
    rest = x
    for index in range(pieces):
        part = rest.astype(BF16)
        if form == "mask_x":
            term = _dot(mask16, part)
        elif form == "mask_xt":
            term = _dot_nt(mask16, part)
        else:
            term = _dot(part, mask16)
        total = term if total is None else total + term
        if index + 1 < pieces:
            rest = rest - part.astype(F32)
    return total


IN_GROUPS = (768, 768, 768, 1024, 128)
D_IN_PAD = sum(IN_GROUPS)


def _hgrn_lower_bound(logits, layer):
    e = jnp.exp(logits - jnp.max(logits, axis=0, keepdims=True))
    soft = e / jnp.sum(e, axis=0, keepdims=True)
    lb = jnp.zeros((1, logits.shape[1]), F32)
    for i in range(1, layer + 1):
        lb = lb + soft[i:i + 1, :]
    return jnp.clip(lb, 0.0, 1.0 - 1e-6)


def _in_proj_kernel(x_ref, ln_ref, w_ref, fqn_ref, fkn_ref, lbl_ref, fox_ref, ssd_ref, gla_ref, hgrn_ref, small_ref,
                    hlogf_ref, *, layer):
    h = _rms(x_ref[...], ln_ref[...]).astype(BF16)
    outs = (fox_ref, ssd_ref, gla_ref, hgrn_ref, small_ref)
    gw = GROUP_WIDTH
    start = 0
    for width, out in zip(IN_GROUPS, outs):
        proj = _dot(h, w_ref[:, start:start + width])
        if out is fox_ref:
            out[:, 0:gw] = (_rms_heads64_wide(proj[:, 0:gw], fqn_ref[...]) * HEAD_DIM ** -0.5).astype(out.dtype)
            out[:, gw:2 * gw] = _rms_heads64_wide(proj[:, gw:2 * gw], fkn_ref[...]).astype(out.dtype)
            out[:, 2 * gw:] = proj[:, 2 * gw:].astype(out.dtype)
        elif out is hgrn_ref:
            lb = _hgrn_lower_bound(lbl_ref[...], layer)
            sig, sig_neg = _sigmoid_pair(proj[:, gw:2 * gw])
            hlogf_ref[...] = jnp.log(lb + (1.0 - lb) * sig)
            out[:, 0:gw] = proj[:, 0:gw].astype(out.dtype)
            out[:, gw:2 * gw] = ((1.0 - lb) * sig_neg).astype(out.dtype)
            out[:, 2 * gw:] = proj[:, 2 * gw:].astype(out.dtype)
        else:
            out[...] = proj.astype(out.dtype)
        start += width


def _in_proj(x2d, ln, w_r, fox_qn, fox_kn, lb_logits, layer, tm):
    m = x2d.shape[0]
    out_widths = IN_GROUPS + (HGRN_FDIM,)
    out_shape = [jax.ShapeDtypeStruct((m, n), BF16) for n in IN_GROUPS[:4]]
    out_shape += [jax.ShapeDtypeStruct((m, IN_GROUPS[4]), F32), jax.ShapeDtypeStruct((m, HGRN_FDIM), F32)]
    vec = pl.BlockSpec((1, GROUP_WIDTH), lambda i: (0, 0))
    return pl.pallas_call(
        functools.partial(_in_proj_kernel, layer=layer),
        grid=(m // tm,),
        in_specs=[
            pl.BlockSpec((tm, D_MODEL), lambda i: (i, 0)),
            pl.BlockSpec((1, D_MODEL), lambda i: (0, 0)),
            _resident((D_MODEL, D_IN_PAD)),
            vec, vec,
            pl.BlockSpec(lb_logits.shape, lambda i: (0, 0)),
        ],
        out_specs=[pl.BlockSpec((tm, n), lambda i: (i, 0)) for n in out_widths],
        out_shape=out_shape,
        compiler_params=_cparams(("parallel",)),
        name="in_proj",
    )(x2d, ln, w_r, fox_qn, fox_kn, lb_logits)


FOX_TILE = 256
FOX_CUM_BLOCK = 256
N_SPLIT = 3


def _split3(c):
    hi = c.astype(BF16).astype(F32)
    r = c - hi
    mid = r.astype(BF16).astype(F32)
    lo = (r - mid).astype(BF16).astype(F32)
    return hi, mid, lo


def _fox_selectors():
    selq = np.zeros(((N_SPLIT + 1) * LANES, GROUP_WIDTH), np.float32)
    selk = np.zeros_like(selq)
    for h in range(N_HEADS):
        base = (h // 2) * LANES + HEAD_DIM * (1 - h % 2)
        for p in range(N_SPLIT):
            selq[p * LANES + SMALL_FF + h, base + p] = 1.0
            selq[N_SPLIT * LANES, base + N_SPLIT + p] = 1.0
            selk[N_SPLIT * LANES, base + p] = 1.0
            selk[p * LANES + SMALL_FF + h, base + N_SPLIT + p] = -1.0
    return jnp.asarray(selq, BF16), jnp.asarray(selk, BF16)


def _fox_kernel(q_ref, k_ref, v_ref, small_ref, bias_ref, selq_ref, selk_ref, o_ref,
                ka_ref, qaug_ref, va_ref, *, seq):
    qi = pl.program_id(1)
    tq = FOX_TILE
    halves = GROUP_WIDTH // LANES

    def own_lanes(lane, hh):
        return (lane >= HEAD_DIM * hh) & (lane < HEAD_DIM * (hh + 1))

    @pl.when(qi == 0)
    def _prepare():
        nblk = seq // FOX_CUM_BLOCK
        tri = (_iota2((FOX_CUM_BLOCK, FOX_CUM_BLOCK), 0) >= _iota2((FOX_CUM_BLOCK, FOX_CUM_BLOCK), 1)).astype(BF16)
        lane = _iota2((FOX_CUM_BLOCK, LANES), 1)
        blocks = [pl.ds(blk * FOX_CUM_BLOCK, FOX_CUM_BLOCK) for blk in range(nblk)]
        local = [_masked_sums(tri, _log_sigmoid(small_ref[rows, :] + bias_ref[...]), 3) for rows in blocks]
        offset = jnp.zeros((1, LANES), F32)
        cum = []
        for c_local in local:
            cum.append(c_local + offset)
            offset = offset + c_local[FOX_CUM_BLOCK - 1:FOX_CUM_BLOCK, :]
        pieces = [jnp.concatenate([*_split3(c), jnp.ones_like(c)], axis=-1).astype(BF16) for c in cum]
        k_aug = [_dot(x, selk_ref[...]).astype(BF16) for x in pieces]
        q_aug = [_dot(x, selq_ref[...]).astype(BF16) for x in pieces]
        for rows, k_aug_blk, q_aug_blk in zip(blocks, k_aug, q_aug):
            qaug_ref[rows, :] = q_aug_blk
            for half in range(halves):
                lanes = slice(half * LANES, (half + 1) * LANES)
                kn = k_ref[rows, lanes]
                v_half = v_ref[rows, lanes]
                for hh in range(2):
                    head = 2 * half + hh
                    own = own_lanes(lane, hh)
                    va_ref[head, rows, :] = jnp.where(own, v_half, jnp.ones_like(v_half))
                    ka_ref[head, rows, :] = jnp.where(own, kn, k_aug_blk[:, lanes])

    qrows = pl.ds(pl.multiple_of(qi * tq, tq), tq)
    lane = _iota2((tq, LANES), 1)
    lo_half = lane < HEAD_DIM
    qas = []
    for half in range(halves):
        lanes = slice(half * LANES, (half + 1) * LANES)
        for hh in range(2):
            qas.append(jnp.where(own_lanes(lane, hh), q_ref[:, lanes], qaug_ref[qrows, lanes]))

    heads = range(N_HEADS)

    causal = _iota2((tq, tq), 0) >= _iota2((tq, tq), 1)

    def attend(n_blocks):
        past = (n_blocks - 1) * tq
        own = slice(past, past + tq)
        s_own = [jnp.where(causal, _dot_nt(qas[head], ka_ref[head, own, :]), MASK_VALUE) for head in heads]
        m = [jnp.max(x, axis=-1, keepdims=True) for x in s_own]
        if past:
            s_past = [_dot_nt(qas[head], ka_ref[head, 0:past, :]) for head in heads]
            m = _each(lambda a, x: jnp.maximum(a, jnp.max(x, axis=-1, keepdims=True)), m, s_past)
        acc = [_dot(jnp.exp(s_own[head] - m[head]).astype(BF16), va_ref[head, own, :]) for head in heads]
        if past:
            p_past = _each(lambda x, a: jnp.exp(x - a).astype(BF16), s_past, m)
            acc = [acc[head] + _dot(p_past[head], va_ref[head, 0:past, :]) for head in heads]
        for half in range(halves):
            acc_lo, acc_hi = acc[2 * half], acc[2 * half + 1]
            o = jnp.where(lo_half, acc_lo / pltpu.roll(acc_lo, HEAD_DIM, 1),
                          acc_hi / pltpu.roll(acc_hi, HEAD_DIM, 1))
            o_ref[:, half * LANES:(half + 1) * LANES] = o.astype(o_ref.dtype)
        return 0

    lax.switch(qi, [functools.partial(attend, n + 1) for n in range(seq // tq)])


def _fox(fox3, small3, bias):
    b, seq, _ = fox3.shape
    tq = FOX_TILE
    kern = functools.partial(_fox_kernel, seq=seq)
    selq, selk = _fox_selectors()
    vec = pl.BlockSpec((1, LANES), lambda bi, qi: (0, 0))
    return pl.pallas_call(
        kern,
        grid=(b, seq // tq),
        in_specs=[
            pl.BlockSpec((None, tq, GROUP_WIDTH), lambda bi, qi: (bi, qi, 0)),
            pl.BlockSpec((None, seq, GROUP_WIDTH), lambda bi, qi: (bi, 0, 1)),
            pl.BlockSpec((None, seq, GROUP_WIDTH), lambda bi, qi: (bi, 0, 2)),
            pl.BlockSpec((None, seq, LANES), lambda bi, qi: (bi, 0, 0)),
            vec,
            _resident(selq.shape), _resident(selk.shape),
        ],
        out_specs=pl.BlockSpec((None, tq, GROUP_WIDTH), lambda bi, qi: (bi, qi, 0)),
        out_shape=jax.ShapeDtypeStruct((b, seq, GROUP_WIDTH), BF16),
        scratch_shapes=[pltpu.VMEM((N_HEADS, seq, LANES), BF16), pltpu.VMEM((seq, GROUP_WIDTH), BF16),
                        pltpu.VMEM((N_HEADS, seq, LANES), BF16)],
        compiler_params=_cparams(("parallel", "arbitrary")),
        name="fox",
    )(fox3, fox3, fox3, small3, bias, selq, selk)


def _ssd_kernel(ssd_ref, small_ref, cw_ref, cb_ref, dtb_ref, alog_ref, dskip_ref, gn_ref, o_ref,
                st_ref, *, seq):
    c_len = SSD_CHUNK
    assert c_len == LANES
    n_batch = ssd_ref.shape[0]
    st_ref[...] = jnp.zeros_like(st_ref)

    a_lane = -jnp.exp(alog_ref[...])
    tri = (_iota2((c_len, c_len), 0) >= _iota2((c_len, c_len), 1))
    tri16 = tri.astype(BF16)
    sel16 = (_iota2((8, LANES), 0) + SMALL_DT == _iota2((8, LANES), 1)).astype(BF16)
    lane = _iota2((c_len, LANES), 1)
    lo = lane < HEAD_DIM

    def chunk(n, _):
        _round_robin([chunk_one(n, gb) for gb in range(n_batch)])
        return 0

    def chunk_one(n, gb):
        rows = pl.ds(pl.multiple_of(n * c_len, c_len), c_len)
        halo_rows = pl.ds(pl.multiple_of(jnp.maximum(n * c_len - HALO, 0), HALO), HALO)
        halo = jnp.where(n > 0, ssd_ref[gb, halo_rows, GROUP_WIDTH:].astype(F32), 0.0)
        cur = ssd_ref[gb, rows, GROUP_WIDTH:].astype(F32)
        ext = jnp.concatenate([halo, cur], axis=0)
        conv = cb_ref[...] + cw_ref[SSD_CONV - 1:SSD_CONV, :] * cur
        for shift in range(1, SSD_CONV):
            conv = conv + cw_ref[SSD_CONV - 1 - shift:SSD_CONV - shift, :] * pltpu.roll(ext, shift, 0)[HALO:, :]
        xc = _silu(conv)
        xs = xc[:, 0:GROUP_WIDTH]
        bm = xc[:, GROUP_WIDTH:GROUP_WIDTH + LANES]
        cm = xc[:, GROUP_WIDTH + LANES:]
        dt = _softplus(small_ref[gb, rows, :] + dtb_ref[...])
        a_cs = _masked_sums(tri16, dt * a_lane, 3)
        yield
        a_row = _masked_sums(sel16, a_cs, 3, form="mask_xt")
        cm16 = cm.astype(BF16)
        y_halves = []
        for g in range(2):
            in_group = (lane >= SSD_STATE * g) & (lane < SSD_STATE * (g + 1))
            b_g = jnp.where(in_group, bm, 0.0).astype(BF16)
            scores = _dot_nt(cm16, b_g)
            yield
            heads = (2 * g, 2 * g + 1)
            a_col = [jnp.broadcast_to(_lane_col(a_cs, SMALL_DT + h), (c_len, LANES)) for h in heads]
            dt_col = [_lane_col(dt, SMALL_DT + h) for h in heads]
            a_own = jnp.where(lo, a_col[0], a_col[1])
            a_last = a_own[c_len - 1:c_len, :]
            xdt = xs[:, g * LANES:(g + 1) * LANES] * jnp.where(lo, dt_col[0], dt_col[1])
            xdt16 = xdt.astype(BF16)
            y_intra = []
            for index, h in enumerate(heads):
                seg = a_col[index] - a_row[h:h + 1, :]
                decay = jnp.where(tri, jnp.exp(jnp.where(tri, seg, 0.0)), 0.0)
                y_intra.append(_dot((scores * decay).astype(BF16), xdt16))
            state = st_ref[gb, g]
            y_state = _dot(cm16, state.astype(BF16))
            d_state = _dot_tn(b_g, (xdt * jnp.exp(a_last - a_own)).astype(BF16))
            yield
            y_halves.append(jnp.where(lo, y_intra[0], y_intra[1]) + y_state * jnp.exp(a_own))
            st_ref[gb, g] = state * jnp.exp(a_last) + d_state
        y = jnp.concatenate(y_halves, axis=-1) + xs * dskip_ref[...]
        y = y * _silu(ssd_ref[gb, rows, 0:GROUP_WIDTH].astype(F32))
        out = jnp.concatenate([_rms(y[:, i * LANES:(i + 1) * LANES], gn_ref[:, i * LANES:(i + 1) * LANES])
                               for i in range(2)], axis=-1)
        o_ref[gb, rows, :] = out.astype(o_ref.dtype)

    lax.fori_loop(0, seq // c_len, chunk, 0)


def _batch_spec(n_batch, seq, width):
    return pl.BlockSpec((n_batch, seq, width), lambda bi: (bi, 0, 0))


def _segment_spec(n_batch, seg, width):
    return pl.BlockSpec((n_batch, seg, width), lambda bi, si: (bi, si, 0))


def _zero_at_first_segment(state_ref):
    @pl.when(pl.program_id(1) == 0)
    def _():
        state_ref[...] = jnp.zeros_like(state_ref)


def _ssd(ssd3, small3, cw, cb, dtb, alog, dskip, gn):
    b, seq, width = ssd3.shape
    conv_dim = GROUP_WIDTH + 2 * LANES
    kern = functools.partial(_ssd_kernel, seq=seq)
    nb = min(SSD_BATCH, b)

    def vec(n, rows=1):
        return pl.BlockSpec((rows, n), lambda bi: (0, 0))

    return pl.pallas_call(
        kern,
        grid=(b // nb,),
        in_specs=[
            _batch_spec(nb, seq, width), _batch_spec(nb, seq, LANES),
            vec(conv_dim, SSD_CONV), vec(conv_dim), vec(LANES), vec(LANES), vec(GROUP_WIDTH), vec(GROUP_WIDTH),
        ],
        out_specs=_batch_spec(nb, seq, GROUP_WIDTH),
        out_shape=jax.ShapeDtypeStruct((b, seq, GROUP_WIDTH), BF16),
        scratch_shapes=[pltpu.VMEM((nb, 2, LANES, LANES), F32)],
        compiler_params=_cparams(("parallel",)),
        name="ssd",
    )(ssd3, small3, cw, cb, dtb, alog, dskip, gn)


def _gla_levels(c_len, sub):
    sizes = []
    size = sub
    while size <= c_len:
        sizes.append(size)
        size *= 2
    return sizes


def _gla_consts(c_len, dk, dv, sub):
    dkh = dk // N_HEADS
    dvh = dv // N_HEADS
    sizes = _gla_levels(c_len, sub)
    row = _iota2((c_len, c_len), 0)
    col = _iota2((c_len, c_len), 1)
    sum_masks = []
    for size in sizes:
        same = (row // size) == (col // size)
        sum_masks.append((same & (col <= row)).astype(BF16))
        sum_masks.append(same.astype(BF16))
    arow = _iota2((c_len, N_HEADS * c_len), 0)
    acol = _iota2((c_len, N_HEADS * c_len), 1) % c_len
    keep = [((arow // sub) == (acol // sub)) & (acol <= arow)]
    for level in range(1, len(sizes)):
        size, half = sizes[level], sizes[level - 1]
        keep.append(((arow // size) == (acol // size)) & ((arow % size) >= half) & ((acol % size) < half))
    klane = (_iota2((c_len, dk), 1) // dkh).astype(BF16)
    vlane = (_iota2((c_len, dv), 1) // dvh).astype(BF16)
    diag = (_iota2((dv, dk), 0) // dvh) == (_iota2((dv, dk), 1) // dkh)
    sum_mask = jnp.concatenate(sum_masks, axis=0)
    return dict(sizes=sizes, sum_mask=jnp.concatenate([sum_mask, sum_mask], axis=1), keep=keep,
                klane=klane, vlane=vlane, diag=diag)


def _each(fn, *lists):
    return [fn(*args) for args in zip(*lists)]


def _round_robin(generators):
    live = list(generators)
    while live:
        still = []
        for gen in live:
            try:
                next(gen)
                still.append(gen)
            except StopIteration:
                pass
        live = still


def _gla_chunk(q, k, v, logf, state_t, consts):
    c_len = q[0].shape[0]
    sizes = consts["sizes"]
    n_levels = len(sizes)
    def prefix_sums(x):
        hi = x.astype(BF16)
        lo = (x - hi.astype(F32)).astype(BF16)
        return _dot(consts["sum_mask"], jnp.concatenate([hi, lo], axis=0))

    sums = _each(prefix_sums, logf)
    cs = [[s[(2 * i) * c_len:(2 * i + 1) * c_len] for i in range(n_levels)] for s in sums]
    tot = [[s[(2 * i + 1) * c_len:(2 * i + 2) * c_len] for i in range(n_levels)] for s in sums]

    def stack_heads(x, lane_head):
        x16 = x.astype(BF16)
        return jnp.concatenate([jnp.where(lane_head == h, x16, jnp.zeros_like(x16)) for h in range(N_HEADS)], axis=0)

    q0 = _each(lambda x, c: (x * jnp.exp(c[0])).astype(BF16), q, cs)
    k0 = _each(lambda x, c: stack_heads(x * jnp.exp(-c[0]), consts["klane"]), k, cs)
    att = _each(lambda a, b: jnp.where(consts["keep"][0], _dot_nt(a, b), 0.0), q0, k0)
    for level in range(1, n_levels):
        q_l = q0 if level == 1 else _each(lambda x, c: (x * jnp.exp(c[level - 1])).astype(BF16), q, cs)
        k_l = _each(lambda x, c, t: stack_heads(x * jnp.exp(t[level - 1] - c[level - 1]), consts["klane"]), k, cs, tot)
        att = _each(lambda a, b, prev: jnp.where(consts["keep"][level], _dot_nt(a, b), prev), q_l, k_l, att)
    v_stack = _each(lambda x: stack_heads(x, consts["vlane"]), v)
    o = _each(lambda a, b: _dot(a.astype(BF16), b), att, v_stack)

    q_s = _each(lambda x, c: (x * jnp.exp(c[-1])).astype(BF16), q, cs)
    k_s = _each(lambda x, c, t: (x * jnp.exp(t[-1] - c[-1])).astype(BF16), k, cs, tot)
    o = _each(lambda acc, a, s: acc + _dot_nt(a, s.astype(BF16)), o, q_s, state_t)
    d_state = _each(lambda x, y: jnp.where(consts["diag"], _dot_tn(x.astype(BF16), y), 0.0), v, k_s)
    new_state = _each(lambda s, t, d: s * jnp.exp(t[-1][0:1, :]) + d, state_t, tot, d_state)
    return o, new_state


def _gla_kernel(gla_ref, small_ref, wg_ref, bg_ref, on_ref, o_ref, st_ref):
    c_len = GLA_CHUNK
    dk = GLA_KEY_DIM
    seq = gla_ref.shape[1]
    _zero_at_first_segment(st_ref)
    scale = (dk // N_HEADS) ** -0.5
    consts = _gla_consts(c_len, dk, GROUP_WIDTH, c_len)

    def chunk(n, _):
        rows = pl.ds(pl.multiple_of(n * c_len, c_len), c_len)
        batch = range(gla_ref.shape[0])
        gate = [_dot(small_ref[gb, rows, :].astype(BF16), wg_ref[...]) + bg_ref[...] for gb in batch]
        logf = [_log_sigmoid(x) * (1.0 / GLA_TAU) for x in gate]
        q = [gla_ref[gb, rows, 0:dk].astype(F32) * scale for gb in batch]
        k = [gla_ref[gb, rows, dk:2 * dk].astype(F32) for gb in batch]
        v = [gla_ref[gb, rows, 2 * dk:2 * dk + GROUP_WIDTH].astype(F32) for gb in batch]
        o, new_state = _gla_chunk(q, k, v, logf, [st_ref[gb] for gb in batch], consts)
        for gb in batch:
            st_ref[gb] = new_state[gb]
            r = gla_ref[gb, rows, 2 * dk + GROUP_WIDTH:].astype(F32)
            o_ref[gb, rows, :] = (_rms_heads64_wide(o[gb], on_ref[...]) * _silu(r)).astype(o_ref.dtype)
        return 0

    lax.fori_loop(0, seq // c_len, chunk, 0)


def _gla(gla3, small3, wg, bg, on):
    b, seq, width = gla3.shape
    nb = min(GLA_BATCH, b)
    seg = min(GLA_SEGMENT, seq)
    const = lambda bi, si: (0, 0)
    return pl.pallas_call(
        _gla_kernel,
        grid=(b // nb, seq // seg),
        in_specs=[
            _segment_spec(nb, seg, width), _segment_spec(nb, seg, LANES),
            pl.BlockSpec((LANES, GLA_KEY_DIM), const),
            pl.BlockSpec((1, GLA_KEY_DIM), const),
            pl.BlockSpec((1, GROUP_WIDTH), const),
        ],
        out_specs=_segment_spec(nb, seg, GROUP_WIDTH),
        out_shape=jax.ShapeDtypeStruct((b, seq, GROUP_WIDTH), BF16),
        scratch_shapes=[pltpu.VMEM((nb, GROUP_WIDTH, GLA_KEY_DIM), F32)],
        compiler_params=_cparams(("parallel", "arbitrary")),
        name="gla",
    )(gla3, small3, wg, bg, on)


def _hgrn_kernel(hg_ref, logf_ref, on_ref, o_ref, st_ref):
    c_len = HGRN_CHUNK
    dk = HGRN_FDIM
    seq = hg_ref.shape[1]
    _zero_at_first_segment(st_ref)
    consts = _gla_consts(c_len, dk, GROUP_WIDTH, HGRN_SUB)

    def chunk(n, _):
        rows = pl.ds(pl.multiple_of(n * c_len, c_len), c_len)
        batch = range(hg_ref.shape[0])
        logf = [logf_ref[gb, rows, :] for gb in batch]
        k = [hg_ref[gb, rows, dk:2 * dk].astype(F32) for gb in batch]
        q = [hg_ref[gb, rows, 0:dk].astype(F32) for gb in batch]
        v = [hg_ref[gb, rows, 2 * dk:2 * dk + GROUP_WIDTH].astype(F32) for gb in batch]
        o, new_state = _gla_chunk(q, k, v, logf, [st_ref[gb] for gb in batch], consts)
        for gb in batch:
            st_ref[gb] = new_state[gb]
            g = hg_ref[gb, rows, 2 * dk + GROUP_WIDTH:].astype(F32)
            o_ref[gb, rows, :] = (_rms_heads64_wide(o[gb], on_ref[...]) * _silu(g)).astype(o_ref.dtype)
        return 0

    lax.fori_loop(0, seq // c_len, chunk, 0)


def _hgrn(hg3, logf3, on):
    b, seq, width = hg3.shape
    nb = min(HGRN_BATCH, b)
    seg = min(HGRN_SEGMENT, seq)
    const = lambda bi, si: (0, 0)
    return pl.pallas_call(
        _hgrn_kernel,
        grid=(b // nb, seq // seg),
        in_specs=[
            _segment_spec(nb, seg, width),
            _segment_spec(nb, seg, HGRN_FDIM),
            pl.BlockSpec((1, GROUP_WIDTH), const),
        ],
        out_specs=_segment_spec(nb, seg, GROUP_WIDTH),
        out_shape=jax.ShapeDtypeStruct((b, seq, GROUP_WIDTH), BF16),
        scratch_shapes=[pltpu.VMEM((nb, GROUP_WIDTH, HGRN_FDIM), F32)],
        compiler_params=_cparams(("parallel", "arbitrary")),
        name="hgrn2",
    )(hg3, logf3, on)


def _kv_kernel(mem_ref, mn_ref, w_ref, kn_ref, k_ref, v_ref):
    mem_n = _rms(mem_ref[...], mn_ref[...]).astype(BF16)
    kv = _dot(mem_n, w_ref[...])
    for h in range(XA_HEADS):
        cols = slice(h * XA_HEAD_DIM, (h + 1) * XA_HEAD_DIM)
        k_ref[:, cols] = _rms(kv[:, cols], kn_ref[...]).astype(k_ref.dtype)
    v_ref[...] = kv[:, D_MODEL:].astype(v_ref.dtype)


def _kv_proj(mem2d, mem_norm, w_kv, kn, tm):
    m = mem2d.shape[0]
    depth = w_kv.shape[0]
    out = jax.ShapeDtypeStruct((depth, m, D_MODEL), BF16)
    return pl.pallas_call(
        _kv_kernel,
        grid=(depth, m // tm),
        in_specs=[
            pl.BlockSpec((tm, D_MODEL), lambda l, i: (i, 0)),
            pl.BlockSpec((1, D_MODEL), lambda l, i: (0, 0)),
            pl.BlockSpec((None, D_MODEL, 2 * D_MODEL), lambda l, i: (l, 0, 0)),
            pl.BlockSpec((None, 1, XA_HEAD_DIM), lambda l, i: (l, 0, 0)),
        ],
        out_specs=[pl.BlockSpec((None, tm, D_MODEL), lambda l, i: (l, i, 0))] * 2,
        out_shape=[out, out],
        compiler_params=_cparams(("parallel", "parallel")),
        name="mem_kv",
    )(mem2d, mem_norm, w_kv, kn)


def _mix_xattn_kernel(x_ref, yf_ref, ys_ref, yg_ref, yh_ref, fon_ref, wout_ref, ln_ref, wq_ref, qn_ref,
                      k_ref, v_ref, wo_ref, o_ref):
    y_fox = _rms_heads64_wide(yf_ref[...].astype(F32), fon_ref[...]).astype(BF16)
    mixed = jnp.concatenate([y_fox, ys_ref[...], yg_ref[...], yh_ref[...]], axis=-1)
    x = x_ref[...] + _dot(mixed, wout_ref[...])
    h = _rms(x, ln_ref[...]).astype(BF16)
    q = _dot(h, wq_ref[...])
    scale = XA_HEAD_DIM ** -0.5
    outs = []
    for hd in range(XA_HEADS):
        cols = slice(hd * XA_HEAD_DIM, (hd + 1) * XA_HEAD_DIM)
        qh = (_rms(q[:, cols], qn_ref[...]) * scale).astype(BF16)
        logits = _dot_nt(qh, k_ref[:, cols])
        p = jnp.exp(logits - jnp.max(logits, axis=-1, keepdims=True))
        p = p / jnp.sum(p, axis=-1, keepdims=True)
        outs.append(_dot(p.astype(BF16), v_ref[:, cols]).astype(BF16))
    o = jnp.concatenate(outs, axis=-1)
    o_ref[...] = x + _dot(o, wo_ref[...])


def _mix_xattn(x2d, ys, fox_on, w_out, ln, wq, qn, k3, v3, wo, tm, seq):
    m = x2d.shape[0]
    n_mem = k3.shape[1]
    tiles_per_seq = seq // tm
    row = lambda i: (i, 0)
    const = lambda i: (0, 0)
    mem_spec = pl.BlockSpec((None, n_mem, D_MODEL), lambda i: (i // tiles_per_seq, 0, 0))
    return pl.pallas_call(
        _mix_xattn_kernel,
        grid=(m // tm,),
        in_specs=[
            pl.BlockSpec((tm, D_MODEL), row),
            *[pl.BlockSpec((tm, GROUP_WIDTH), row)] * 4,
            pl.BlockSpec((1, GROUP_WIDTH), const),
            _resident((D_MODEL, D_MODEL)),
            pl.BlockSpec((1, D_MODEL), const),
            _resident((D_MODEL, D_MODEL)),
            pl.BlockSpec((1, XA_HEAD_DIM), const),
            mem_spec, mem_spec,
            _resident((D_MODEL, D_MODEL)),
        ],
        out_specs=pl.BlockSpec((tm, D_MODEL), row),
        out_shape=jax.ShapeDtypeStruct((m, D_MODEL), F32),
        compiler_params=_cparams(("parallel",)),
        name="mix_xattn",
    )(x2d, *ys, fox_on, w_out, ln, wq, qn, k3, v3, wo)


def _ffn_kernel(x_ref, xp_ref, ln_ref, wup_ref, cw_ref, cb_ref, wdown_ref, o_ref, act_ref, *, tiles_per_seq):
    i = pl.program_id(0)
    x = x_ref[...]
    first = (i % tiles_per_seq) == 0
    h = _rms(x, ln_ref[...]).astype(BF16)
    hp = jnp.where(first, 0.0, _rms(xp_ref[...], ln_ref[...])).astype(BF16)
    h_ext = jnp.concatenate([hp, h], axis=0)
    for c in range(D_FF // FF_TILE):
        cols = slice(c * FF_TILE, (c + 1) * FF_TILE)
        vcols = slice(D_FF + c * FF_TILE, D_FF + (c + 1) * FF_TILE)
        gate = _dot(h_ext, wup_ref[:, cols])
        conv = cb_ref[:, cols] + cw_ref[FFN_CONV - 1:FFN_CONV, cols] * gate[HALO:, :]
        for shift in range(1, FFN_CONV):
            conv = conv + (cw_ref[FFN_CONV - 1 - shift:FFN_CONV - shift, cols]
                           * pltpu.roll(gate, shift, 0)[HALO:, :])
        val = _dot(h, wup_ref[:, vcols])
        act_ref[:, cols] = (_silu(conv) * val).astype(BF16)
    o_ref[...] = x + _dot(act_ref[...], wdown_ref[...])


def _ffn(x2d, ln, w_up, cw, cb, w_down, tm, seq):
    m = x2d.shape[0]
    kern = functools.partial(_ffn_kernel, tiles_per_seq=seq // tm)
    const = lambda i: (0, 0)
    halo_blocks = tm // HALO
    return pl.pallas_call(
        kern,
        grid=(m // tm,),
        in_specs=[
            pl.BlockSpec((tm, D_MODEL), lambda i: (i, 0)),
            pl.BlockSpec((HALO, D_MODEL), lambda i: (jnp.maximum(i * halo_blocks - 1, 0), 0)),
            pl.BlockSpec((1, D_MODEL), const),
            _resident((D_MODEL, 2 * D_FF)),
            pl.BlockSpec((FFN_CONV, D_FF), const),
            pl.BlockSpec((1, D_FF), const),
            _resident((D_FF, D_MODEL)),
        ],
        out_specs=pl.BlockSpec((tm, D_MODEL), lambda i: (i, 0)),
        out_shape=jax.ShapeDtypeStruct((m, D_MODEL), F32),
        scratch_shapes=[pltpu.VMEM((tm, D_FF), BF16)],
        compiler_params=_cparams(("parallel",)),
        name="conv_glu_ffn",
    )(x2d, x2d, ln, w_up, cw, cb, w_down)


def _pad_lanes(vec, offset, total=LANES):
    vec = vec.astype(F32).reshape(1, -1)
    return jnp.pad(vec, ((0, 0), (offset, total - offset - vec.shape[1])))


def _arrange_w_in(w):
    fq, fk, fv, ff, sz, sxbc, sdt, gq, gk, gv, ga, gr, hq, hf, hi, hg = jnp.split(
        w, [256, 512, 768, 772, 1028, 1540, 1544, 1672, 1800, 2056, 2072, 2328, 2584, 2840, 3096], axis=1)
    small = jnp.concatenate([ff, sdt, ga], axis=1)
    small = jnp.pad(small, ((0, 0), (0, LANES - small.shape[1])))
    return jnp.concatenate([fq, fk, fv, sz, sxbc, gq, gk, gv, gr, hq, hf, hi, hg, small], axis=1).astype(BF16)


def kernel(x, mem, ln_mix, w_in, w_out, fox_f_bias, fox_qn, fox_kn, fox_on, ssd_conv_w, ssd_conv_b, ssd_dt_bias,
           ssd_a_log, ssd_d, ssd_norm, gla_w_g2, gla_b_g2, gla_norm, hgrn_lb_logits, hgrn_norm, ln_xattn, mem_norm,
           xa_wq, xa_wkv, xa_wo, xa_qn, xa_kn, ln_ffn, ffn_w_up, ffn_conv_w, ffn_conv_b, ffn_w_down):
    b, seq, d = x.shape
    depth = w_in.shape[0]
    n_mem = mem.shape[1]
    m = b * seq
    tm = min(ROW_TILE, seq)
    x2d = x.reshape(m, d)

    k_all, v_all = _kv_proj(mem.reshape(b * n_mem, d), mem_norm.reshape(1, d), xa_wkv.astype(BF16),
                            xa_kn.reshape(depth, 1, XA_HEAD_DIM), min(512, b * n_mem))
    for l in range(depth):
        tile4 = lambda v: jnp.tile(v.astype(F32).reshape(1, -1), (1, GROUP_WIDTH // HEAD_DIM))
        fox, ssd, gla, hgrn, small, hgrn_logf = _in_proj(x2d, ln_mix[l].reshape(1, d), _arrange_w_in(w_in[l]),
                                                         tile4(fox_qn[l]), tile4(fox_kn[l]), hgrn_lb_logits, l, tm)
        small3 = small.reshape(b, seq, LANES)
        y_fox = _fox(fox.reshape(b, seq, -1), small3, _pad_lanes(fox_f_bias[l], SMALL_FF))
        y_ssd = _ssd(ssd.reshape(b, seq, -1), small3, ssd_conv_w[l], ssd_conv_b[l].reshape(1, -1),
                     _pad_lanes(ssd_dt_bias[l], SMALL_DT), _pad_lanes(ssd_a_log[l], SMALL_DT),
                     jnp.repeat(ssd_d[l].astype(F32), HEAD_DIM).reshape(1, -1), ssd_norm[l].reshape(1, -1))
        wg = jnp.pad(gla_w_g2[l], ((SMALL_GA, LANES - SMALL_GA - GLA_GATE_RANK), (0, 0))).astype(BF16)
        y_gla = _gla(gla.reshape(b, seq, -1), small3, wg, gla_b_g2[l].reshape(1, -1), tile4(gla_norm[l]))
        y_hgrn = _hgrn(hgrn.reshape(b, seq, -1), hgrn_logf.reshape(b, seq, -1), tile4(hgrn_norm[l]))
        ys = [y.reshape(m, GROUP_WIDTH) for y in (y_fox, y_ssd, y_gla, y_hgrn)]
        x2d = _mix_xattn(x2d, ys, tile4(fox_on[l]), w_out[l].astype(BF16), ln_xattn[l].reshape(1, d),
                         xa_wq[l].astype(BF16),
                         xa_qn[l].reshape(1, -1), k_all[l].reshape(b, n_mem, d), v_all[l].reshape(b, n_mem, d),
                         xa_wo[l].astype(BF16), tm, seq)
        x2d = _ffn(x2d, ln_ffn[l].reshape(1, d), ffn_w_up[l].astype(BF16), ffn_conv_w[l],
                   ffn_conv_b[l].reshape(1, -1), ffn_w_down[l].astype(BF16), min(FFN_ROWS, seq), seq)
    return x2d.reshape(b, seq, d)
```

```python
import functools

import jax
import jax.numpy as jnp
import numpy as np
from jax import lax
from jax.experimental import pallas as pl
from jax.experimental.pallas import tpu as pltpu

F32 = jnp.float32
BF16 = jnp.bfloat16

EPS = 1e-6
MASK_VALUE = -1e30
D_MODEL = 1024
GROUP_WIDTH = 256
HEAD_DIM = 64
LANES = 128
N_HEADS = 4
SSD_STATE = 64
SSD_CONV = 4
SSD_CHUNK = 128
GLA_KEY_DIM = 128
GLA_GATE_RANK = 16
GLA_TAU = 16.0
GLA_CHUNK = 64
HGRN_FDIM = 256
HGRN_CHUNK = 64
HGRN_SUB = 16
XA_HEADS = 4
XA_HEAD_DIM = 256
D_FF = 2816
FFN_CONV = 3
FF_TILE = 256
FFN_ROWS = 1024
ROW_TILE = 1024
HALO = 16
SSD_BATCH = 4
GLA_BATCH = 16
HGRN_BATCH = 8
GLA_SEGMENT = 512
HGRN_SEGMENT = 512

SMALL_FF = 0
SMALL_DT = 4
SMALL_GA = 8

VMEM_LIMIT = 56 * 1024 * 1024


def _cparams(sem):
    return pltpu.CompilerParams(dimension_semantics=sem, vmem_limit_bytes=VMEM_LIMIT)


def _resident(shape):
    return pl.BlockSpec(shape, lambda *_: (0,) * len(shape), pipeline_mode=pl.Buffered(1))


def _dot(a, b, precision=None):
    return jnp.dot(a, b, preferred_element_type=F32, precision=precision)


def _dot_nt(a, b, precision=None):
    return lax.dot_general(a, b, (((1,), (1,)), ((), ())), preferred_element_type=F32, precision=precision)


def _dot_tn(a, b, precision=None):
    return lax.dot_general(a, b, (((0,), (0,)), ((), ())), preferred_element_type=F32, precision=precision)


def _rms(x, w):
    return x * lax.rsqrt(jnp.mean(x * x, axis=-1, keepdims=True) + EPS) * w


def _rms_heads64(x, w):
    lane = lax.broadcasted_iota(jnp.int32, x.shape, 1)
    lo = lane < HEAD_DIM
    sq = x * x
    s_lo = jnp.sum(jnp.where(lo, sq, 0.0), axis=-1, keepdims=True)
    s_hi = jnp.sum(jnp.where(lo, 0.0, sq), axis=-1, keepdims=True)
    ms = jnp.where(lo, s_lo, s_hi) * (1.0 / HEAD_DIM)
    return x * lax.rsqrt(ms + EPS) * w


def _rms_heads64_wide(x, w):
    return jnp.concatenate(
        [_rms_heads64(x[:, i * LANES:(i + 1) * LANES], w[:, i * LANES:(i + 1) * LANES]) for i in range(2)], axis=-1)


def _sigmoid_pair(z):
    e = jnp.exp(-jnp.abs(z))
    big = 1.0 / (1.0 + e)
    small = e * big
    pos = z >= 0
    return jnp.where(pos, big, small), jnp.where(pos, small, big)


def _silu(z):
    return z * (1.0 / (1.0 + jnp.exp(-z)))


def _log_sigmoid(z):
    return jnp.minimum(z, 0.0) - jnp.log1p(jnp.exp(-jnp.abs(z)))


def _softplus(z):
    return jnp.maximum(z, 0.0) + jnp.log1p(jnp.exp(-jnp.abs(z)))


def _iota2(shape, axis):
    return lax.broadcasted_iota(jnp.int32, shape, axis)


def _lane_col(x, lane_index):
    return x[:, lane_index:lane_index + 1]


def _masked_sums(mask16, x, pieces, form="mask_x"):
    total = None
    rest = x
    for index in range(pieces):
        part = rest.astype(BF16)
        if form == "mask_x":
            term = _dot(mask16, part)
        elif form == "mask_xt":
            term = _dot_nt(mask16, part)
        else:
            term = _dot(part, mask16)
        total = term if total is None else total + term
        if index + 1 < pieces:
            rest = rest - part.astype(F32)
    return total


IN_GROUPS = (768, 768, 768, 1024, 128)
D_IN_PAD = sum(IN_GROUPS)


def _hgrn_lower_bound(logits, layer):
    e = jnp.exp(logits - jnp.max(logits, axis=0, keepdims=True))
    soft = e / jnp.sum(e, axis=0, keepdims=True)
    lb = jnp.zeros((1, logits.shape[1]), F32)
    for i in range(1, layer + 1):
        lb = lb + soft[i:i + 1, :]
    return jnp.clip(lb, 0.0, 1.0 - 1e-6)


def _in_proj_kernel(x_ref, ln_ref, w_ref, fqn_ref, fkn_ref, lbl_ref, fox_ref, ssd_ref, gla_ref, hgrn_ref, small_ref,
                    hlogf_ref, *, layer):
    h = _rms(x_ref[...], ln_ref[...]).astype(BF16)
    outs = (fox_ref, ssd_ref, gla_ref, hgrn_ref, small_ref)
    gw = GROUP_WIDTH
    start = 0
    for width, out in zip(IN_GROUPS, outs):
        proj = _dot(h, w_ref[:, start:start + width])
        if out is fox_ref:
            out[:, 0:gw] = (_rms_heads64_wide(proj[:, 0:gw], fqn_ref[...]) * HEAD_DIM ** -0.5).astype(out.dtype)
            out[:, gw:2 * gw] = _rms_heads64_wide(proj[:, gw:2 * gw], fkn_ref[...]).astype(out.dtype)
            out[:, 2 * gw:] = proj[:, 2 * gw:].astype(out.dtype)
        elif out is hgrn_ref:
            lb = _hgrn_lower_bound(lbl_ref[...], layer)
            sig, sig_neg = _sigmoid_pair(proj[:, gw:2 * gw])
            hlogf_ref[...] = jnp.log(lb + (1.0 - lb) * sig)
            out[:, 0:gw] = proj[:, 0:gw].astype(out.dtype)
            out[:, gw:2 * gw] = ((1.0 - lb) * sig_neg).astype(out.dtype)
            out[:, 2 * gw:] = proj[:, 2 * gw:].astype(out.dtype)
        else:
            out[...] = proj.astype(out.dtype)
        start += width


def _in_proj(x2d, ln, w_r, fox_qn, fox_kn, lb_logits, layer, tm):
    m = x2d.shape[0]
    out_widths = IN_GROUPS + (HGRN_FDIM,)
    out_shape = [jax.ShapeDtypeStruct((m, n), BF16) for n in IN_GROUPS[:4]]
    out_shape += [jax.ShapeDtypeStruct((m, IN_GROUPS[4]), F32), jax.ShapeDtypeStruct((m, HGRN_FDIM), F32)]
    vec = pl.BlockSpec((1, GROUP_WIDTH), lambda i: (0, 0))
    return pl.pallas_call(
        functools.partial(_in_proj_kernel, layer=layer),
        grid=(m // tm,),
        in_specs=[
            pl.BlockSpec((tm, D_MODEL), lambda i: (i, 0)),
            pl.BlockSpec((1, D_MODEL), lambda i: (0, 0)),
            _resident((D_MODEL, D_IN_PAD)),
            vec, vec,
            pl.BlockSpec(lb_logits.shape, lambda i: (0, 0)),
        ],
        out_specs=[pl.BlockSpec((tm, n), lambda i: (i, 0)) for n in out_widths],
        out_shape=out_shape,
        compiler_params=_cparams(("parallel",)),
        name="in_proj",
    )(x2d, ln, w_r, fox_qn, fox_kn, lb_logits)


FOX_TILE = 512
FOX_CUM_BLOCK = 256
N_SPLIT = 3


def _split3(c):
    hi = c.astype(BF16).astype(F32)
    r = c - hi
    mid = r.astype(BF16).astype(F32)
    lo = (r - mid).astype(BF16).astype(F32)
    return hi, mid, lo


def _fox_selectors():
    selq = np.zeros(((N_SPLIT + 1) * LANES, GROUP_WIDTH), np.float32)
    selk = np.zeros_like(selq)
    for h in range(N_HEADS):
        base = (h // 2) * LANES + HEAD_DIM * (1 - h % 2)
        for p in range(N_SPLIT):
            selq[p * LANES + SMALL_FF + h, base + p] = 1.0
            selq[N_SPLIT * LANES, base + N_SPLIT + p] = 1.0
            selk[N_SPLIT * LANES, base + p] = 1.0
            selk[p * LANES + SMALL_FF + h, base + N_SPLIT + p] = -1.0
    return jnp.asarray(selq, BF16), jnp.asarray(selk, BF16)


def _fox_kernel(q_ref, k_ref, v_ref, small_ref, bias_ref, selq_ref, selk_ref, o_ref,
                ka_ref, qaug_ref, va_ref, *, seq):
    qi = pl.program_id(1)
    tq = FOX_TILE
    halves = GROUP_WIDTH // LANES

    def own_lanes(lane, hh):
        return (lane >= HEAD_DIM * hh) & (lane < HEAD_DIM * (hh + 1))

    @pl.when(qi == 0)
    def _prepare():
        nblk = seq // FOX_CUM_BLOCK
        tri = (_iota2((FOX_CUM_BLOCK, FOX_CUM_BLOCK), 0) >= _iota2((FOX_CUM_BLOCK, FOX_CUM_BLOCK), 1)).astype(BF16)
        lane = _iota2((FOX_CUM_BLOCK, LANES), 1)
        blocks = [pl.ds(blk * FOX_CUM_BLOCK, FOX_CUM_BLOCK) for blk in range(nblk)]
        local = [_masked_sums(tri, _log_sigmoid(small_ref[rows, :] + bias_ref[...]), 3) for rows in blocks]
        offset = jnp.zeros((1, LANES), F32)
        cum = []
        for c_local in local:
            cum.append(c_local + offset)
            offset = offset + c_local[FOX_CUM_BLOCK - 1:FOX_CUM_BLOCK, :]
        pieces = [jnp.concatenate([*_split3(c), jnp.ones_like(c)], axis=-1).astype(BF16) for c in cum]
        k_aug = [_dot(x, selk_ref[...]).astype(BF16) for x in pieces]
        q_aug = [_dot(x, selq_ref[...]).astype(BF16) for x in pieces]
        for rows, k_aug_blk, q_aug_blk in zip(blocks, k_aug, q_aug):
            qaug_ref[rows, :] = q_aug_blk
            for half in range(halves):
                lanes = slice(half * LANES, (half + 1) * LANES)
                kn = k_ref[rows, lanes]
                v_half = v_ref[rows, lanes]
                for hh in range(2):
                    head = 2 * half + hh
                    own = own_lanes(lane, hh)
                    va_ref[head, rows, :] = jnp.where(own, v_half, jnp.ones_like(v_half))
                    ka_ref[head, rows, :] = jnp.where(own, kn, k_aug_blk[:, lanes])

    qrows = pl.ds(pl.multiple_of(qi * tq, tq), tq)
    lane = _iota2((tq, LANES), 1)
    lo_half = lane < HEAD_DIM
    qas = []
    for half in range(halves):
        lanes = slice(half * LANES, (half + 1) * LANES)
        for hh in range(2):
            qas.append(jnp.where(own_lanes(lane, hh), q_ref[:, lanes], qaug_ref[qrows, lanes]))

    heads = range(N_HEADS)

    causal = _iota2((tq, tq), 0) >= _iota2((tq, tq), 1)

    def attend(n_blocks):
        past = (n_blocks - 1) * tq
        own = slice(past, past + tq)
        s_own = [jnp.where(causal, _dot_nt(qas[head], ka_ref[head, own, :]), MASK_VALUE) for head in heads]
        m = [jnp.max(x, axis=-1, keepdims=True) for x in s_own]
        if past:
            s_past = [_dot_nt(qas[head], ka_ref[head, 0:past, :]) for head in heads]
            m = _each(lambda a, x: jnp.maximum(a, jnp.max(x, axis=-1, keepdims=True)), m, s_past)
        acc = [_dot(jnp.exp(s_own[head] - m[head]).astype(BF16), va_ref[head, own, :]) for head in heads]
        if past:
            p_past = _each(lambda x, a: jnp.exp(x - a).astype(BF16), s_past, m)
            acc = [acc[head] + _dot(p_past[head], va_ref[head, 0:past, :]) for head in heads]
        for half in range(halves):
            acc_lo, acc_hi = acc[2 * half], acc[2 * half + 1]
            o = jnp.where(lo_half, acc_lo / pltpu.roll(acc_lo, HEAD_DIM, 1),
                          acc_hi / pltpu.roll(acc_hi, HEAD_DIM, 1))
            o_ref[:, half * LANES:(half + 1) * LANES] = o.astype(o_ref.dtype)
        return 0

    lax.switch(qi, [functools.partial(attend, n + 1) for n in range(seq // tq)])


def _fox(fox3, small3, bias):
    b, seq, _ = fox3.shape
    tq = FOX_TILE
    kern = functools.partial(_fox_kernel, seq=seq)
    selq, selk = _fox_selectors()
    vec = pl.BlockSpec((1, LANES), lambda bi, qi: (0, 0))
    return pl.pallas_call(
        kern,
        grid=(b, seq // tq),
        in_specs=[
            pl.BlockSpec((None, tq, GROUP_WIDTH), lambda bi, qi: (bi, qi, 0)),
            pl.BlockSpec((None, seq, GROUP_WIDTH), lambda bi, qi: (bi, 0, 1)),
            pl.BlockSpec((None, seq, GROUP_WIDTH), lambda bi, qi: (bi, 0, 2)),
            pl.BlockSpec((None, seq, LANES), lambda bi, qi: (bi, 0, 0)),
            vec,
            _resident(selq.shape), _resident(selk.shape),
        ],
        out_specs=pl.BlockSpec((None, tq, GROUP_WIDTH), lambda bi, qi: (bi, qi, 0)),
        out_shape=jax.ShapeDtypeStruct((b, seq, GROUP_WIDTH), BF16),
        scratch_shapes=[pltpu.VMEM((N_HEADS, seq, LANES), BF16), pltpu.VMEM((seq, GROUP_WIDTH), BF16),
                        pltpu.VMEM((N_HEADS, seq, LANES), BF16)],
        compiler_params=_cparams(("parallel", "arbitrary")),
        name="fox",
    )(fox3, fox3, fox3, small3, bias, selq, selk)


def _ssd_kernel(ssd_ref, small_ref, cw_ref, cb_ref, dtb_ref, alog_ref, dskip_ref, gn_ref, o_ref,
                st_ref, *, seq):
    c_len = SSD_CHUNK
    assert c_len == LANES
    n_batch = ssd_ref.shape[0]
    st_ref[...] = jnp.zeros_like(st_ref)

    a_lane = -jnp.exp(alog_ref[...])
    tri = (_iota2((c_len, c_len), 0) >= _iota2((c_len, c_len), 1))
    tri16 = tri.astype(BF16)
    sel16 = (_iota2((8, LANES), 0) + SMALL_DT == _iota2((8, LANES), 1)).astype(BF16)
    lane = _iota2((c_len, LANES), 1)
    lo = lane < HEAD_DIM

    def chunk(n, _):
        _round_robin([chunk_one(n, gb) for gb in range(n_batch)])
        return 0

    def chunk_one(n, gb):
        rows = pl.ds(pl.multiple_of(n * c_len, c_len), c_len)
        halo_rows = pl.ds(pl.multiple_of(jnp.maximum(n * c_len - HALO, 0), HALO), HALO)
        halo = jnp.where(n > 0, ssd_ref[gb, halo_rows, GROUP_WIDTH:].astype(F32), 0.0)
        cur = ssd_ref[gb, rows, GROUP_WIDTH:].astype(F32)
        ext = jnp.concatenate([halo, cur], axis=0)
        conv = cb_ref[...] + cw_ref[SSD_CONV - 1:SSD_CONV, :] * cur
        for shift in range(1, SSD_CONV):
            conv = conv + cw_ref[SSD_CONV - 1 - shift:SSD_CONV - shift, :] * pltpu.roll(ext, shift, 0)[HALO:, :]
        xc = _silu(conv)
        xs = xc[:, 0:GROUP_WIDTH]
        bm = xc[:, GROUP_WIDTH:GROUP_WIDTH + LANES]
        cm = xc[:, GROUP_WIDTH + LANES:]
        dt = _softplus(small_ref[gb, rows, :] + dtb_ref[...])
        a_cs = _masked_sums(tri16, dt * a_lane, 3)
        yield
        a_row = _masked_sums(sel16, a_cs, 3, form="mask_xt")
        cm16 = cm.astype(BF16)
        y_halves = []
        for g in range(2):
            in_group = (lane >= SSD_STATE * g) & (lane < SSD_STATE * (g + 1))
            b_g = jnp.where(in_group, bm, 0.0).astype(BF16)
            scores = _dot_nt(cm16, b_g)
            yield
            heads = (2 * g, 2 * g + 1)
            a_col = [jnp.broadcast_to(_lane_col(a_cs, SMALL_DT + h), (c_len, LANES)) for h in heads]
            dt_col = [_lane_col(dt, SMALL_DT + h) for h in heads]
            a_own = jnp.where(lo, a_col[0], a_col[1])
            a_last = a_own[c_len - 1:c_len, :]
            xdt = xs[:, g * LANES:(g + 1) * LANES] * jnp.where(lo, dt_col[0], dt_col[1])
            xdt16 = xdt.astype(BF16)
            y_intra = []
            for index, h in enumerate(heads):
                seg = a_col[index] - a_row[h:h + 1, :]
                decay = jnp.where(tri, jnp.exp(jnp.where(tri, seg, 0.0)), 0.0)
                y_intra.append(_dot((scores * decay).astype(BF16), xdt16))
            state = st_ref[gb, g]
            y_state = _dot(cm16, state.astype(BF16))
            d_state = _dot_tn(b_g, (xdt * jnp.exp(a_last - a_own)).astype(BF16))
            yield
            y_halves.append(jnp.where(lo, y_intra[0], y_intra[1]) + y_state * jnp.exp(a_own))
            st_ref[gb, g] = state * jnp.exp(a_last) + d_state
        y = jnp.concatenate(y_halves, axis=-1) + xs * dskip_ref[...]
        y = y * _silu(ssd_ref[gb, rows, 0:GROUP_WIDTH].astype(F32))
        out = jnp.concatenate([_rms(y[:, i * LANES:(i + 1) * LANES], gn_ref[:, i * LANES:(i + 1) * LANES])
                               for i in range(2)], axis=-1)
        o_ref[gb, rows, :] = out.astype(o_ref.dtype)

    lax.fori_loop(0, seq // c_len, chunk, 0)


def _batch_spec(n_batch, seq, width):
    return pl.BlockSpec((n_batch, seq, width), lambda bi: (bi, 0, 0))


def _segment_spec(n_batch, seg, width):
    return pl.BlockSpec((n_batch, seg, width), lambda bi, si: (bi, si, 0))


def _zero_at_first_segment(state_ref):
    @pl.when(pl.program_id(1) == 0)
    def _():
        state_ref[...] = jnp.zeros_like(state_ref)


def _ssd(ssd3, small3, cw, cb, dtb, alog, dskip, gn):
    b, seq, width = ssd3.shape
    conv_dim = GROUP_WIDTH + 2 * LANES
    kern = functools.partial(_ssd_kernel, seq=seq)
    nb = min(SSD_BATCH, b)

    def vec(n, rows=1):
        return pl.BlockSpec((rows, n), lambda bi: (0, 0))

    return pl.pallas_call(
        kern,
        grid=(b // nb,),
        in_specs=[
            _batch_spec(nb, seq, width), _batch_spec(nb, seq, LANES),
            vec(conv_dim, SSD_CONV), vec(conv_dim), vec(LANES), vec(LANES), vec(GROUP_WIDTH), vec(GROUP_WIDTH),
        ],
        out_specs=_batch_spec(nb, seq, GROUP_WIDTH),
        out_shape=jax.ShapeDtypeStruct((b, seq, GROUP_WIDTH), BF16),
        scratch_shapes=[pltpu.VMEM((nb, 2, LANES, LANES), F32)],
        compiler_params=_cparams(("parallel",)),
        name="ssd",
    )(ssd3, small3, cw, cb, dtb, alog, dskip, gn)


def _gla_levels(c_len, sub):
    sizes = []
    size = sub
    while size <= c_len:
        sizes.append(size)
        size *= 2
    return sizes


def _gla_consts(c_len, dk, dv, sub):
    dkh = dk // N_HEADS
    dvh = dv // N_HEADS
    sizes = _gla_levels(c_len, sub)
    row = _iota2((c_len, c_len), 0)
    col = _iota2((c_len, c_len), 1)
    sum_masks = []
    for size in sizes:
        same = (row // size) == (col // size)
        sum_masks.append((same & (col <= row)).astype(BF16))
        sum_masks.append(same.astype(BF16))
    arow = _iota2((c_len, N_HEADS * c_len), 0)
    acol = _iota2((c_len, N_HEADS * c_len), 1) % c_len
    keep = [((arow // sub) == (acol // sub)) & (acol <= arow)]
    for level in range(1, len(sizes)):
        size, half = sizes[level], sizes[level - 1]
        keep.append(((arow // size) == (acol // size)) & ((arow % size) >= half) & ((acol % size) < half))
    klane = (_iota2((c_len, dk), 1) // dkh).astype(BF16)
    vlane = (_iota2((c_len, dv), 1) // dvh).astype(BF16)
    diag = (_iota2((dv, dk), 0) // dvh) == (_iota2((dv, dk), 1) // dkh)
    sum_mask = jnp.concatenate(sum_masks, axis=0)
    return dict(sizes=sizes, sum_mask=jnp.concatenate([sum_mask, sum_mask], axis=1), keep=keep,
                klane=klane, vlane=vlane, diag=diag)


def _each(fn, *lists):
    return [fn(*args) for args in zip(*lists)]


def _round_robin(generators):
    live = list(generators)
    while live:
        still = []
        for gen in live:
            try:
                next(gen)
                still.append(gen)
            except StopIteration:
                pass
        live = still


def _gla_chunk(q, k, v, logf, state_t, consts):
    c_len = q[0].shape[0]
    sizes = consts["sizes"]
    n_levels = len(sizes)
    def prefix_sums(x):
        hi = x.astype(BF16)
        lo = (x - hi.astype(F32)).astype(BF16)
        return _dot(consts["sum_mask"], jnp.concatenate([hi, lo], axis=0))

    sums = _each(prefix_sums, logf)
    cs = [[s[(2 * i) * c_len:(2 * i + 1) * c_len] for i in range(n_levels)] for s in sums]
    tot = [[s[(2 * i + 1) * c_len:(2 * i + 2) * c_len] for i in range(n_levels)] for s in sums]

    def stack_heads(x, lane_head):
        x16 = x.astype(BF16)
        return jnp.concatenate([jnp.where(lane_head == h, x16, jnp.zeros_like(x16)) for h in range(N_HEADS)], axis=0)

    q0 = _each(lambda x, c: (x * jnp.exp(c[0])).astype(BF16), q, cs)
    k0 = _each(lambda x, c: stack_heads(x * jnp.exp(-c[0]), consts["klane"]), k, cs)
    att = _each(lambda a, b: jnp.where(consts["keep"][0], _dot_nt(a, b), 0.0), q0, k0)
    for level in range(1, n_levels):
        q_l = q0 if level == 1 else _each(lambda x, c: (x * jnp.exp(c[level - 1])).astype(BF16), q, cs)
        k_l = _each(lambda x, c, t: stack_heads(x * jnp.exp(t[level - 1] - c[level - 1]), consts["klane"]), k, cs, tot)
        att = _each(lambda a, b, prev: jnp.where(consts["keep"][level], _dot_nt(a, b), prev), q_l, k_l, att)
    v_stack = _each(lambda x: stack_heads(x, consts["vlane"]), v)
    o = _each(lambda a, b: _dot(a.astype(BF16), b), att, v_stack)

    q_s = _each(lambda x, c: (x * jnp.exp(c[-1])).astype(BF16), q, cs)
    k_s = _each(lambda x, c, t: (x * jnp.exp(t[-1] - c[-1])).astype(BF16), k, cs, tot)
    o = _each(lambda acc, a, s: acc + _dot_nt(a, s.astype(BF16)), o, q_s, state_t)
    d_state = _each(lambda x, y: jnp.where(consts["diag"], _dot_tn(x.astype(BF16), y), 0.0), v, k_s)
    new_state = _each(lambda s, t, d: s * jnp.exp(t[-1][0:1, :]) + d, state_t, tot, d_state)
    return o, new_state


def _gla_kernel(gla_ref, small_ref, wg_ref, bg_ref, on_ref, o_ref, st_ref):
    c_len = GLA_CHUNK
    dk = GLA_KEY_DIM
    seq = gla_ref.shape[1]
    _zero_at_first_segment(st_ref)
    scale = (dk // N_HEADS) ** -0.5
    consts = _gla_consts(c_len, dk, GROUP_WIDTH, c_len)

    def chunk(n, _):
        rows = pl.ds(pl.multiple_of(n * c_len, c_len), c_len)
        batch = range(gla_ref.shape[0])
        gate = [_dot(small_ref[gb, rows, :].astype(BF16), wg_ref[...]) + bg_ref[...] for gb in batch]
        logf = [_log_sigmoid(x) * (1.0 / GLA_TAU) for x in gate]
        q = [gla_ref[gb, rows, 0:dk].astype(F32) * scale for gb in batch]
        k = [gla_ref[gb, rows, dk:2 * dk].astype(F32) for gb in batch]
        v = [gla_ref[gb, rows, 2 * dk:2 * dk + GROUP_WIDTH].astype(F32) for gb in batch]
        o, new_state = _gla_chunk(q, k, v, logf, [st_ref[gb] for gb in batch], consts)
        for gb in batch:
            st_ref[gb] = new_state[gb]
            r = gla_ref[gb, rows, 2 * dk + GROUP_WIDTH:].astype(F32)
            o_ref[gb, rows, :] = (_rms_heads64_wide(o[gb], on_ref[...]) * _silu(r)).astype(o_ref.dtype)
        return 0

    lax.fori_loop(0, seq // c_len, chunk, 0)


def _gla(gla3, small3, wg, bg, on):
    b, seq, width = gla3.shape
    nb = min(GLA_BATCH, b)
    seg = min(GLA_SEGMENT, seq)
    const = lambda bi, si: (0, 0)
    return pl.pallas_call(
        _gla_kernel,
        grid=(b // nb, seq // seg),
        in_specs=[
            _segment_spec(nb, seg, width), _segment_spec(nb, seg, LANES),
            pl.BlockSpec((LANES, GLA_KEY_DIM), const),
            pl.BlockSpec((1, GLA_KEY_DIM), const),
            pl.BlockSpec((1, GROUP_WIDTH), const),
        ],
        out_specs=_segment_spec(nb, seg, GROUP_WIDTH),
        out_shape=jax.ShapeDtypeStruct((b, seq, GROUP_WIDTH), BF16),
        scratch_shapes=[pltpu.VMEM((nb, GROUP_WIDTH, GLA_KEY_DIM), F32)],
        compiler_params=_cparams(("parallel", "arbitrary")),
        name="gla",
    )(gla3, small3, wg, bg, on)


def _hgrn_kernel(hg_ref, logf_ref, on_ref, o_ref, st_ref):
    c_len = HGRN_CHUNK
    dk = HGRN_FDIM
    seq = hg_ref.shape[1]
    _zero_at_first_segment(st_ref)
    consts = _gla_consts(c_len, dk, GROUP_WIDTH, HGRN_SUB)

    def chunk(n, _):
        rows = pl.ds(pl.multiple_of(n * c_len, c_len), c_len)
        batch = range(hg_ref.shape[0])
        logf = [logf_ref[gb, rows, :] for gb in batch]
        k = [hg_ref[gb, rows, dk:2 * dk].astype(F32) for gb in batch]
        q = [hg_ref[gb, rows, 0:dk].astype(F32) for gb in batch]
        v = [hg_ref[gb, rows, 2 * dk:2 * dk + GROUP_WIDTH].astype(F32) for gb in batch]
        o, new_state = _gla_chunk(q, k, v, logf, [st_ref[gb] for gb in batch], consts)
        for gb in batch:
            st_ref[gb] = new_state[gb]
            g = hg_ref[gb, rows, 2 * dk + GROUP_WIDTH:].astype(F32)
            o_ref[gb, rows, :] = (_rms_heads64_wide(o[gb], on_ref[...]) * _silu(g)).astype(o_ref.dtype)
        return 0

    lax.fori_loop(0, seq // c_len, chunk, 0)


def _hgrn(hg3, logf3, on):
    b, seq, width = hg3.shape
    nb = min(HGRN_BATCH, b)
    seg = min(HGRN_SEGMENT, seq)
    const = lambda bi, si: (0, 0)
    return pl.pallas_call(
        _hgrn_kernel,
        grid=(b // nb, seq // seg),
        in_specs=[
            _segment_spec(nb, seg, width),
            _segment_spec(nb, seg, HGRN_FDIM),
            pl.BlockSpec((1, GROUP_WIDTH), const),
        ],
        out_specs=_segment_spec(nb, seg, GROUP_WIDTH),
        out_shape=jax.ShapeDtypeStruct((b, seq, GROUP_WIDTH), BF16),
        scratch_shapes=[pltpu.VMEM((nb, GROUP_WIDTH, HGRN_FDIM), F32)],
        compiler_params=_cparams(("parallel", "arbitrary")),
        name="hgrn2",
    )(hg3, logf3, on)


def _kv_kernel(mem_ref, mn_ref, w_ref, kn_ref, k_ref, v_ref):
    mem_n = _rms(mem_ref[...], mn_ref[...]).astype(BF16)
    kv = _dot(mem_n, w_ref[...])
    for h in range(XA_HEADS):
        cols = slice(h * XA_HEAD_DIM, (h + 1) * XA_HEAD_DIM)
        k_ref[:, cols] = _rms(kv[:, cols], kn_ref[...]).astype(k_ref.dtype)
    v_ref[...] = kv[:, D_MODEL:].astype(v_ref.dtype)


def _kv_proj(mem2d, mem_norm, w_kv, kn, tm):
    m = mem2d.shape[0]
    depth = w_kv.shape[0]
    out = jax.ShapeDtypeStruct((depth, m, D_MODEL), BF16)
    return pl.pallas_call(
        _kv_kernel,
        grid=(depth, m // tm),
        in_specs=[
            pl.BlockSpec((tm, D_MODEL), lambda l, i: (i, 0)),
            pl.BlockSpec((1, D_MODEL), lambda l, i: (0, 0)),
            pl.BlockSpec((None, D_MODEL, 2 * D_MODEL), lambda l, i: (l, 0, 0)),
            pl.BlockSpec((None, 1, XA_HEAD_DIM), lambda l, i: (l, 0, 0)),
        ],
        out_specs=[pl.BlockSpec((None, tm, D_MODEL), lambda l, i: (l, i, 0))] * 2,
        out_shape=[out, out],
        compiler_params=_cparams(("parallel", "parallel")),
        name="mem_kv",
    )(mem2d, mem_norm, w_kv, kn)


def _mix_xattn_kernel(x_ref, yf_ref, ys_ref, yg_ref, yh_ref, fon_ref, wout_ref, ln_ref, wq_ref, qn_ref,
                      k_ref, v_ref, wo_ref, o_ref):
    y_fox = _rms_heads64_wide(yf_ref[...].astype(F32), fon_ref[...]).astype(BF16)
    mixed = jnp.concatenate([y_fox, ys_ref[...], yg_ref[...], yh_ref[...]], axis=-1)
    x = x_ref[...] + _dot(mixed, wout_ref[...])
    h = _rms(x, ln_ref[...]).astype(BF16)
    q = _dot(h, wq_ref[...])
    scale = XA_HEAD_DIM ** -0.5
    outs = []
    for hd in range(XA_HEADS):
        cols = slice(hd * XA_HEAD_DIM, (hd + 1) * XA_HEAD_DIM)
        qh = (_rms(q[:, cols], qn_ref[...]) * scale).astype(BF16)
        logits = _dot_nt(qh, k_ref[:, cols])
        p = jnp.exp(logits - jnp.max(logits, axis=-1, keepdims=True))
        p = p / jnp.sum(p, axis=-1, keepdims=True)
        outs.append(_dot(p.astype(BF16), v_ref[:, cols]).astype(BF16))
    o = jnp.concatenate(outs, axis=-1)
    o_ref[...] = x + _dot(o, wo_ref[...])


def _mix_xattn(x2d, ys, fox_on, w_out, ln, wq, qn, k3, v3, wo, tm, seq):
    m = x2d.shape[0]
    n_mem = k3.shape[1]
    tiles_per_seq = seq // tm
    row = lambda i: (i, 0)
    const = lambda i: (0, 0)
    mem_spec = pl.BlockSpec((None, n_mem, D_MODEL), lambda i: (i // tiles_per_seq, 0, 0))
    return pl.pallas_call(
        _mix_xattn_kernel,
        grid=(m // tm,),
        in_specs=[
            pl.BlockSpec((tm, D_MODEL), row),
            *[pl.BlockSpec((tm, GROUP_WIDTH), row)] * 4,
            pl.BlockSpec((1, GROUP_WIDTH), const),
            _resident((D_MODEL, D_MODEL)),
            pl.BlockSpec((1, D_MODEL), const),
            _resident((D_MODEL, D_MODEL)),
            pl.BlockSpec((1, XA_HEAD_DIM), const),
            mem_spec, mem_spec,
            _resident((D_MODEL, D_MODEL)),
        ],
        out_specs=pl.BlockSpec((tm, D_MODEL), row),
        out_shape=jax.ShapeDtypeStruct((m, D_MODEL), F32),
        compiler_params=_cparams(("parallel",)),
        name="mix_xattn",
    )(x2d, *ys, fox_on, w_out, ln, wq, qn, k3, v3, wo)


def _ffn_kernel(x_ref, xp_ref, ln_ref, wup_ref, cw_ref, cb_ref, wdown_ref, o_ref, act_ref, *, tiles_per_seq):
    i = pl.program_id(0)
    x = x_ref[...]
    first = (i % tiles_per_seq) == 0
    h = _rms(x, ln_ref[...]).astype(BF16)
    hp = jnp.where(first, 0.0, _rms(xp_ref[...], ln_ref[...])).astype(BF16)
    h_ext = jnp.concatenate([hp, h], axis=0)
    for c in range(D_FF // FF_TILE):
        cols = slice(c * FF_TILE, (c + 1) * FF_TILE)
        vcols = slice(D_FF + c * FF_TILE, D_FF + (c + 1) * FF_TILE)
        gate = _dot(h_ext, wup_ref[:, cols])
        conv = cb_ref[:, cols] + cw_ref[FFN_CONV - 1:FFN_CONV, cols] * gate[HALO:, :]
        for shift in range(1, FFN_CONV):
            conv = conv + (cw_ref[FFN_CONV - 1 - shift:FFN_CONV - shift, cols]
                           * pltpu.roll(gate, shift, 0)[HALO:, :])
        val = _dot(h, wup_ref[:, vcols])
        act_ref[:, cols] = (_silu(conv) * val).astype(BF16)
    o_ref[...] = x + _dot(act_ref[...], wdown_ref[...])


def _ffn(x2d, ln, w_up, cw, cb, w_down, tm, seq):
    m = x2d.shape[0]
    kern = functools.partial(_ffn_kernel, tiles_per_seq=seq // tm)
    const = lambda i: (0, 0)
    halo_blocks = tm // HALO
    return pl.pallas_call(
        kern,
        grid=(m // tm,),
        in_specs=[
            pl.BlockSpec((tm, D_MODEL), lambda i: (i, 0)),
            pl.BlockSpec((HALO, D_MODEL), lambda i: (jnp.maximum(i * halo_blocks - 1, 0), 0)),
            pl.BlockSpec((1, D_MODEL), const),
            _resident((D_MODEL, 2 * D_FF)),
            pl.BlockSpec((FFN_CONV, D_FF), const),
            pl.BlockSpec((1, D_FF), const),
            _resident((D_FF, D_MODEL)),
        ],
        out_specs=pl.BlockSpec((tm, D_MODEL), lambda i: (i, 0)),
        out_shape=jax.ShapeDtypeStruct((m, D_MODEL), F32),
        scratch_shapes=[pltpu.VMEM((tm, D_FF), BF16)],
        compiler_params=_cparams(("parallel",)),
        name="conv_glu_ffn",
    )(x2d, x2d, ln, w_up, cw, cb, w_down)


def _pad_lanes(vec, offset, total=LANES):
    vec = vec.astype(F32).reshape(1, -1)
    return jnp.pad(vec, ((0, 0), (offset, total - offset - vec.shape[1])))


def _arrange_w_in(w):
    fq, fk, fv, ff, sz, sxbc, sdt, gq, gk, gv, ga, gr, hq, hf, hi, hg = jnp.split(
        w, [256, 512, 768, 772, 1028, 1540, 1544, 1672, 1800, 2056, 2072, 2328, 2584, 2840, 3096], axis=1)
    small = jnp.concatenate([ff, sdt, ga], axis=1)
    small = jnp.pad(small, ((0, 0), (0, LANES - small.shape[1])))
    return jnp.concatenate([fq, fk, fv, sz, sxbc, gq, gk, gv, gr, hq, hf, hi, hg, small], axis=1).astype(BF16)


def kernel(x, mem, ln_mix, w_in, w_out, fox_f_bias, fox_qn, fox_kn, fox_on, ssd_conv_w, ssd_conv_b, ssd_dt_bias,
           ssd_a_log, ssd_d, ssd_norm, gla_w_g2, gla_b_g2, gla_norm, hgrn_lb_logits, hgrn_norm, ln_xattn, mem_norm,
           xa_wq, xa_wkv, xa_wo, xa_qn, xa_kn, ln_ffn, ffn_w_up, ffn_conv_w, ffn_conv_b, ffn_w_down):
    b, seq, d = x.shape
    depth = w_in.shape[0]
    n_mem = mem.shape[1]
    m = b * seq
    tm = min(ROW_TILE, seq)
    x2d = x.reshape(m, d)

    k_all, v_all = _kv_proj(mem.reshape(b * n_mem, d), mem_norm.reshape(1, d), xa_wkv.astype(BF16),
                            xa_kn.reshape(depth, 1, XA_HEAD_DIM), min(512, b * n_mem))
    for l in range(depth):
        tile4 = lambda v: jnp.tile(v.astype(F32).reshape(1, -1), (1, GROUP_WIDTH // HEAD_DIM))
        fox, ssd, gla, hgrn, small, hgrn_logf = _in_proj(x2d, ln_mix[l].reshape(1, d), _arrange_w_in(w_in[l]),
                                                         tile4(fox_qn[l]), tile4(fox_kn[l]), hgrn_lb_logits, l, tm)
        small3 = small.reshape(b, seq, LANES)
        y_fox = _fox(fox.reshape(b, seq, -1), small3, _pad_lanes(fox_f_bias[l], SMALL_FF))
        y_ssd = _ssd(ssd.reshape(b, seq, -1), small3, ssd_conv_w[l], ssd_conv_b[l].reshape(1, -1),
                     _pad_lanes(ssd_dt_bias[l], SMALL_DT), _pad_lanes(ssd_a_log[l], SMALL_DT),
                     jnp.repeat(ssd_d[l].astype(F32), HEAD_DIM).reshape(1, -1), ssd_norm[l].reshape(1, -1))
        wg = jnp.pad(gla_w_g2[l], ((SMALL_GA, LANES - SMALL_GA - GLA_GATE_RANK), (0, 0))).astype(BF16)
        y_gla = _gla(gla.reshape(b, seq, -1), small3, wg, gla_b_g2[l].reshape(1, -1), tile4(gla_norm[l]))
        y_hgrn = _hgrn(hgrn.reshape(b, seq, -1), hgrn_logf.reshape(b, seq, -1), tile4(hgrn_norm[l]))
        ys = [y.reshape(m, GROUP_WIDTH) for y in (y_fox, y_ssd, y_gla, y_hgrn)]
        x2d = _mix_xattn(x2d, ys, tile4(fox_on[l]), w_out[l].astype(BF16), ln_xattn[l].reshape(1, d),
                         xa_wq[l].astype(BF16),
                         xa_qn[l].reshape(1, -1), k_all[l].reshape(b, n_mem, d), v_all[l].reshape(b, n_mem, d),
                         xa_wo[l].astype(BF16), tm, seq)
        x2d = _ffn(x2d, ln_ffn[l].reshape(1, d), ffn_w_up[l].astype(BF16), ffn_conv_w[l],
                   ffn_conv_b[l].reshape(1, -1), ffn_w_down[l].astype(BF16), min(FFN_ROWS, seq), seq)
    return x2d.reshape(b, seq, d)
```

```python
import functools

import jax
import jax.numpy as jnp
import numpy as np
from jax import lax
from jax.experimental import pallas as pl
from jax.experimental.pallas import tpu as pltpu

F32 = jnp.float32
BF16 = jnp.bfloat16

EPS = 1e-6
MASK_VALUE = -1e30
D_MODEL = 1024
GROUP_WIDTH = 256
HEAD_DIM = 64
LANES = 128
N_HEADS = 4
SSD_STATE = 64
SSD_CONV = 4
SSD_CHUNK = 128
GLA_KEY_DIM = 128
GLA_GATE_RANK = 16
GLA_TAU = 16.0
GLA_CHUNK = 64
HGRN_FDIM = 256
HGRN_CHUNK = 64
HGRN_SUB = 16
XA_HEADS = 4
XA_HEAD_DIM = 256
D_FF = 2816
FFN_CONV = 3
FF_TILE = 256
FFN_ROWS = 1024
ROW_TILE = 1024
HALO = 16
SSD_BATCH = 4
GLA_BATCH = 16
HGRN_BATCH = 16
GLA_SEGMENT = 512
HGRN_SEGMENT = 256

SMALL_FF = 0
SMALL_DT = 4
SMALL_GA = 8

VMEM_LIMIT = 56 * 1024 * 1024


def _cparams(sem):
    return pltpu.CompilerParams(dimension_semantics=sem, vmem_limit_bytes=VMEM_LIMIT)


def _resident(shape):
    return pl.BlockSpec(shape, lambda *_: (0,) * len(shape), pipeline_mode=pl.Buffered(1))


def _dot(a, b, precision=None):
    return jnp.dot(a, b, preferred_element_type=F32, precision=precision)


def _dot_nt(a, b, precision=None):
    return lax.dot_general(a, b, (((1,), (1,)), ((), ())), preferred_element_type=F32, precision=precision)


def _dot_tn(a, b, precision=None):
    return lax.dot_general(a, b, (((0,), (0,)), ((), ())), preferred_element_type=F32, precision=precision)


def _rms(x, w):
    return x * lax.rsqrt(jnp.mean(x * x, axis=-1, keepdims=True) + EPS) * w


def _rms_heads64(x, w):
    lane = lax.broadcasted_iota(jnp.int32, x.shape, 1)
    lo = lane < HEAD_DIM
    sq = x * x
    s_lo = jnp.sum(jnp.where(lo, sq, 0.0), axis=-1, keepdims=True)
    s_hi = jnp.sum(jnp.where(lo, 0.0, sq), axis=-1, keepdims=True)
    ms = jnp.where(lo, s_lo, s_hi) * (1.0 / HEAD_DIM)
    return x * lax.rsqrt(ms + EPS) * w


def _rms_heads64_wide(x, w):
    return jnp.concatenate(
        [_rms_heads64(x[:, i * LANES:(i + 1) * LANES], w[:, i * LANES:(i + 1) * LANES]) for i in range(2)], axis=-1)


def _sigmoid_pair(z):
    e = jnp.exp(-jnp.abs(z))
    big = 1.0 / (1.0 + e)
    small = e * big
    pos = z >= 0
    return jnp.where(pos, big, small), jnp.where(pos, small, big)


def _silu(z):
    return z * (1.0 / (1.0 + jnp.exp(-z)))


def _log_sigmoid(z):
    return jnp.minimum(z, 0.0) - jnp.log1p(jnp.exp(-jnp.abs(z)))


def _softplus(z):
    return jnp.maximum(z, 0.0) + jnp.log1p(jnp.exp(-jnp.abs(z)))


def _iota2(shape, axis):
    return lax.broadcasted_iota(jnp.int32, shape, axis)


def _lane_col(x, lane_index):
    return x[:, lane_index:lane_index + 1]


def _masked_sums(mask16, x, pieces, form="mask_x"):
    total = None
    rest = x
    for index in range(pieces):
        part = rest.astype(BF16)
        if form == "mask_x":
            term = _dot(mask16, part)
        elif form == "mask_xt":
            term = _dot_nt(mask16, part)
        else:
            term = _dot(part, mask16)
        total = term if total is None else total + term
        if index + 1 < pieces:
            rest = rest - part.astype(F32)
    return total


IN_GROUPS = (768, 768, 768, 1024, 128)
D_IN_PAD = sum(IN_GROUPS)


def _hgrn_lower_bound(logits, layer):
    e = jnp.exp(logits - jnp.max(logits, axis=0, keepdims=True))
    soft = e / jnp.sum(e, axis=0, keepdims=True)
    lb = jnp.zeros((1, logits.shape[1]), F32)
    for i in range(1, layer + 1):
        lb = lb + soft[i:i + 1, :]
    return jnp.clip(lb, 0.0, 1.0 - 1e-6)


def _in_proj_kernel(x_ref, ln_ref, w_ref, fqn_ref, fkn_ref, lbl_ref, fox_ref, ssd_ref, gla_ref, hgrn_ref, small_ref,
                    hlogf_ref, *, layer):
    h = _rms(x_ref[...], ln_ref[...]).astype(BF16)
    outs = (fox_ref, ssd_ref, gla_ref, hgrn_ref, small_ref)
    gw = GROUP_WIDTH
    start = 0
    for width, out in zip(IN_GROUPS, outs):
        proj = _dot(h, w_ref[:, start:start + width])
        if out is fox_ref:
            out[:, 0:gw] = (_rms_heads64_wide(proj[:, 0:gw], fqn_ref[...]) * HEAD_DIM ** -0.5).astype(out.dtype)
            out[:, gw:2 * gw] = _rms_heads64_wide(proj[:, gw:2 * gw], fkn_ref[...]).astype(out.dtype)
            out[:, 2 * gw:] = proj[:, 2 * gw:].astype(out.dtype)
        elif out is hgrn_ref:
            lb = _hgrn_lower_bound(lbl_ref[...], layer)
            sig, sig_neg = _sigmoid_pair(proj[:, gw:2 * gw])
            hlogf_ref[...] = jnp.log(lb + (1.0 - lb) * sig)
            out[:, 0:gw] = proj[:, 0:gw].astype(out.dtype)
            out[:, gw:2 * gw] = ((1.0 - lb) * sig_neg).astype(out.dtype)
            out[:, 2 * gw:] = proj[:, 2 * gw:].astype(out.dtype)
        else:
            out[...] = proj.astype(out.dtype)
        start += width


def _in_proj(x2d, ln, w_r, fox_qn, fox_kn, lb_logits, layer, tm):
    m = x2d.shape[0]
    out_widths = IN_GROUPS + (HGRN_FDIM,)
    out_shape = [jax.ShapeDtypeStruct((m, n), BF16) for n in IN_GROUPS[:4]]
    out_shape += [jax.ShapeDtypeStruct((m, IN_GROUPS[4]), F32), jax.ShapeDtypeStruct((m, HGRN_FDIM), F32)]
    vec = pl.BlockSpec((1, GROUP_WIDTH), lambda i: (0, 0))
    return pl.pallas_call(
        functools.partial(_in_proj_kernel, layer=layer),
        grid=(m // tm,),
        in_specs=[
            pl.BlockSpec((tm, D_MODEL), lambda i: (i, 0)),
            pl.BlockSpec((1, D_MODEL), lambda i: (0, 0)),
            _resident((D_MODEL, D_IN_PAD)),
            vec, vec,
            pl.BlockSpec(lb_logits.shape, lambda i: (0, 0)),
        ],
        out_specs=[pl.BlockSpec((tm, n), lambda i: (i, 0)) for n in out_widths],
        out_shape=out_shape,
        compiler_params=_cparams(("parallel",)),
        name="in_proj",
    )(x2d, ln, w_r, fox_qn, fox_kn, lb_logits)


FOX_TILE = 512
FOX_CUM_BLOCK = 256
N_SPLIT = 3


def _split3(c):
    hi = c.astype(BF16).astype(F32)
    r = c - hi
    mid = r.astype(BF16).astype(F32)
    lo = (r - mid).astype(BF16).astype(F32)
    return hi, mid, lo


def _fox_selectors():
    selq = np.zeros(((N_SPLIT + 1) * LANES, GROUP_WIDTH), np.float32)
    selk = np.zeros_like(selq)
    for h in range(N_HEADS):
        base = (h // 2) * LANES + HEAD_DIM * (1 - h % 2)
        for p in range(N_SPLIT):
            selq[p * LANES + SMALL_FF + h, base + p] = 1.0
            selq[N_SPLIT * LANES, base + N_SPLIT + p] = 1.0
            selk[N_SPLIT * LANES, base + p] = 1.0
            selk[p * LANES + SMALL_FF + h, base + N_SPLIT + p] = -1.0
    return jnp.asarray(selq, BF16), jnp.asarray(selk, BF16)


def _fox_kernel(q_ref, k_ref, v_ref, small_ref, bias_ref, selq_ref, selk_ref, o_ref,
                ka_ref, qaug_ref, va_ref, *, seq):
    qi = pl.program_id(1)
    tq = FOX_TILE
    halves = GROUP_WIDTH // LANES

    def own_lanes(lane, hh):
        return (lane >= HEAD_DIM * hh) & (lane < HEAD_DIM * (hh + 1))

    @pl.when(qi == 0)
    def _prepare():
        nblk = seq // FOX_CUM_BLOCK
        tri = (_iota2((FOX_CUM_BLOCK, FOX_CUM_BLOCK), 0) >= _iota2((FOX_CUM_BLOCK, FOX_CUM_BLOCK), 1)).astype(BF16)
        lane = _iota2((FOX_CUM_BLOCK, LANES), 1)
        blocks = [pl.ds(blk * FOX_CUM_BLOCK, FOX_CUM_BLOCK) for blk in range(nblk)]
        local = [_masked_sums(tri, _log_sigmoid(small_ref[rows, :] + bias_ref[...]), 3) for rows in blocks]
        offset = jnp.zeros((1, LANES), F32)
        cum = []
        for c_local in local:
            cum.append(c_local + offset)
            offset = offset + c_local[FOX_CUM_BLOCK - 1:FOX_CUM_BLOCK, :]
        pieces = [jnp.concatenate([*_split3(c), jnp.ones_like(c)], axis=-1).astype(BF16) for c in cum]
        k_aug = [_dot(x, selk_ref[...]).astype(BF16) for x in pieces]
        q_aug = [_dot(x, selq_ref[...]).astype(BF16) for x in pieces]
        for rows, k_aug_blk, q_aug_blk in zip(blocks, k_aug, q_aug):
            qaug_ref[rows, :] = q_aug_blk
            for half in range(halves):
                lanes = slice(half * LANES, (half + 1) * LANES)
                kn = k_ref[rows, lanes]
                v_half = v_ref[rows, lanes]
                for hh in range(2):
                    head = 2 * half + hh
                    own = own_lanes(lane, hh)
                    va_ref[head, rows, :] = jnp.where(own, v_half, jnp.ones_like(v_half))
                    ka_ref[head, rows, :] = jnp.where(own, kn, k_aug_blk[:, lanes])

    qrows = pl.ds(pl.multiple_of(qi * tq, tq), tq)
    lane = _iota2((tq, LANES), 1)
    lo_half = lane < HEAD_DIM
    qas = []
    for half in range(halves):
        lanes = slice(half * LANES, (half + 1) * LANES)
        for hh in range(2):
            qas.append(jnp.where(own_lanes(lane, hh), q_ref[:, lanes], qaug_ref[qrows, lanes]))

    heads = range(N_HEADS)

    causal = _iota2((tq, tq), 0) >= _iota2((tq, tq), 1)

    def attend(n_blocks):
        past = (n_blocks - 1) * tq
        own = slice(past, past + tq)
        s_own = [jnp.where(causal, _dot_nt(qas[head], ka_ref[head, own, :]), MASK_VALUE) for head in heads]
        m = [jnp.max(x, axis=-1, keepdims=True) for x in s_own]
        if past:
            s_past = [_dot_nt(qas[head], ka_ref[head, 0:past, :]) for head in heads]
            m = _each(lambda a, x: jnp.maximum(a, jnp.max(x, axis=-1, keepdims=True)), m, s_past)
        acc = [_dot(jnp.exp(s_own[head] - m[head]).astype(BF16), va_ref[head, own, :]) for head in heads]
        if past:
            p_past = _each(lambda x, a: jnp.exp(x - a).astype(BF16), s_past, m)
            acc = [acc[head] + _dot(p_past[head], va_ref[head, 0:past, :]) for head in heads]
        for half in range(halves):
            acc_lo, acc_hi = acc[2 * half], acc[2 * half + 1]
            o = jnp.where(lo_half, acc_lo / pltpu.roll(acc_lo, HEAD_DIM, 1),
                          acc_hi / pltpu.roll(acc_hi, HEAD_DIM, 1))
            o_ref[:, half * LANES:(half + 1) * LANES] = o.astype(o_ref.dtype)
        return 0

    lax.switch(qi, [functools.partial(attend, n + 1) for n in range(seq // tq)])


def _fox(fox3, small3, bias):
    b, seq, _ = fox3.shape
    tq = FOX_TILE
    kern = functools.partial(_fox_kernel, seq=seq)
    selq, selk = _fox_selectors()
    vec = pl.BlockSpec((1, LANES), lambda bi, qi: (0, 0))
    return pl.pallas_call(
        kern,
        grid=(b, seq // tq),
        in_specs=[
            pl.BlockSpec((None, tq, GROUP_WIDTH), lambda bi, qi: (bi, qi, 0)),
            pl.BlockSpec((None, seq, GROUP_WIDTH), lambda bi, qi: (bi, 0, 1)),
            pl.BlockSpec((None, seq, GROUP_WIDTH), lambda bi, qi: (bi, 0, 2)),
            pl.BlockSpec((None, seq, LANES), lambda bi, qi: (bi, 0, 0)),
            vec,
            _resident(selq.shape), _resident(selk.shape),
        ],
        out_specs=pl.BlockSpec((None, tq, GROUP_WIDTH), lambda bi, qi: (bi, qi, 0)),
        out_shape=jax.ShapeDtypeStruct((b, seq, GROUP_WIDTH), BF16),
        scratch_shapes=[pltpu.VMEM((N_HEADS, seq, LANES), BF16), pltpu.VMEM((seq, GROUP_WIDTH), BF16),
                        pltpu.VMEM((N_HEADS, seq, LANES), BF16)],
        compiler_params=_cparams(("parallel", "arbitrary")),
        name="fox",
    )(fox3, fox3, fox3, small3, bias, selq, selk)


def _ssd_kernel(ssd_ref, small_ref, cw_ref, cb_ref, dtb_ref, alog_ref, dskip_ref, gn_ref, o_ref,
                st_ref, *, seq):
    c_len = SSD_CHUNK
    assert c_len == LANES
    n_batch = ssd_ref.shape[0]
    st_ref[...] = jnp.zeros_like(st_ref)

    a_lane = -jnp.exp(alog_ref[...])
    tri = (_iota2((c_len, c_len), 0) >= _iota2((c_len, c_len), 1))
    tri16 = tri.astype(BF16)
    sel16 = (_iota2((8, LANES), 0) + SMALL_DT == _iota2((8, LANES), 1)).astype(BF16)
    lane = _iota2((c_len, LANES), 1)
    lo = lane < HEAD_DIM

    def chunk(n, _):
        _round_robin([chunk_one(n, gb) for gb in range(n_batch)])
        return 0

    def chunk_one(n, gb):
        rows = pl.ds(pl.multiple_of(n * c_len, c_len), c_len)
        halo_rows = pl.ds(pl.multiple_of(jnp.maximum(n * c_len - HALO, 0), HALO), HALO)
        halo = jnp.where(n > 0, ssd_ref[gb, halo_rows, GROUP_WIDTH:].astype(F32), 0.0)
        cur = ssd_ref[gb, rows, GROUP_WIDTH:].astype(F32)
        ext = jnp.concatenate([halo, cur], axis=0)
        conv = cb_ref[...] + cw_ref[SSD_CONV - 1:SSD_CONV, :] * cur
        for shift in range(1, SSD_CONV):
            conv = conv + cw_ref[SSD_CONV - 1 - shift:SSD_CONV - shift, :] * pltpu.roll(ext, shift, 0)[HALO:, :]
        xc = _silu(conv)
        xs = xc[:, 0:GROUP_WIDTH]
        bm = xc[:, GROUP_WIDTH:GROUP_WIDTH + LANES]
        cm = xc[:, GROUP_WIDTH + LANES:]
        dt = _softplus(small_ref[gb, rows, :] + dtb_ref[...])
        a_cs = _masked_sums(tri16, dt * a_lane, 3)
        yield
        a_row = _masked_sums(sel16, a_cs, 3, form="mask_xt")
        cm16 = cm.astype(BF16)
        y_halves = []
        for g in range(2):
            in_group = (lane >= SSD_STATE * g) & (lane < SSD_STATE * (g + 1))
            b_g = jnp.where(in_group, bm, 0.0).astype(BF16)
            scores = _dot_nt(cm16, b_g)
            yield
            heads = (2 * g, 2 * g + 1)
            a_col = [jnp.broadcast_to(_lane_col(a_cs, SMALL_DT + h), (c_len, LANES)) for h in heads]
            dt_col = [_lane_col(dt, SMALL_DT + h) for h in heads]
            a_own = jnp.where(lo, a_col[0], a_col[1])
            a_last = a_own[c_len - 1:c_len, :]
            xdt = xs[:, g * LANES:(g + 1) * LANES] * jnp.where(lo, dt_col[0], dt_col[1])
            xdt16 = xdt.astype(BF16)
            y_intra = []
            for index, h in enumerate(heads):
                seg = a_col[index] - a_row[h:h + 1, :]
                decay = jnp.where(tri, jnp.exp(jnp.where(tri, seg, 0.0)), 0.0)
                y_intra.append(_dot((scores * decay).astype(BF16), xdt16))
            state = st_ref[gb, g]
            y_state = _dot(cm16, state.astype(BF16))
            d_state = _dot_tn(b_g, (xdt * jnp.exp(a_last - a_own)).astype(BF16))
            yield
            y_halves.append(jnp.where(lo, y_intra[0], y_intra[1]) + y_state * jnp.exp(a_own))
            st_ref[gb, g] = state * jnp.exp(a_last) + d_state
        y = jnp.concatenate(y_halves, axis=-1) + xs * dskip_ref[...]
        y = y * _silu(ssd_ref[gb, rows, 0:GROUP_WIDTH].astype(F32))
        out = jnp.concatenate([_rms(y[:, i * LANES:(i + 1) * LANES], gn_ref[:, i * LANES:(i + 1) * LANES])
                               for i in range(2)], axis=-1)
        o_ref[gb, rows, :] = out.astype(o_ref.dtype)

    lax.fori_loop(0, seq // c_len, chunk, 0)


def _batch_spec(n_batch, seq, width):
    return pl.BlockSpec((n_batch, seq, width), lambda bi: (bi, 0, 0))


def _segment_spec(n_batch, seg, width):
    return pl.BlockSpec((n_batch, seg, width), lambda bi, si: (bi, si, 0))


def _zero_at_first_segment(state_ref):
    @pl.when(pl.program_id(1) == 0)
    def _():
        state_ref[...] = jnp.zeros_like(state_ref)


def _ssd(ssd3, small3, cw, cb, dtb, alog, dskip, gn):
    b, seq, width = ssd3.shape
    conv_dim = GROUP_WIDTH + 2 * LANES
    kern = functools.partial(_ssd_kernel, seq=seq)
    nb = min(SSD_BATCH, b)

    def vec(n, rows=1):
        return pl.BlockSpec((rows, n), lambda bi: (0, 0))

    return pl.pallas_call(
        kern,
        grid=(b // nb,),
        in_specs=[
            _batch_spec(nb, seq, width), _batch_spec(nb, seq, LANES),
            vec(conv_dim, SSD_CONV), vec(conv_dim), vec(LANES), vec(LANES), vec(GROUP_WIDTH), vec(GROUP_WIDTH),
        ],
        out_specs=_batch_spec(nb, seq, GROUP_WIDTH),
        out_shape=jax.ShapeDtypeStruct((b, seq, GROUP_WIDTH), BF16),
        scratch_shapes=[pltpu.VMEM((nb, 2, LANES, LANES), F32)],
        compiler_params=_cparams(("parallel",)),
        name="ssd",
    )(ssd3, small3, cw, cb, dtb, alog, dskip, gn)


def _gla_levels(c_len, sub):
    sizes = []
    size = sub
    while size <= c_len:
        sizes.append(size)
        size *= 2
    return sizes


def _gla_consts(c_len, dk, dv, sub):
    dkh = dk // N_HEADS
    dvh = dv // N_HEADS
    sizes = _gla_levels(c_len, sub)
    row = _iota2((c_len, c_len), 0)
    col = _iota2((c_len, c_len), 1)
    sum_masks = []
    for size in sizes:
        same = (row // size) == (col // size)
        sum_masks.append((same & (col <= row)).astype(BF16))
        sum_masks.append(same.astype(BF16))
    arow = _iota2((c_len, N_HEADS * c_len), 0)
    acol = _iota2((c_len, N_HEADS * c_len), 1) % c_len
    keep = [((arow // sub) == (acol // sub)) & (acol <= arow)]
    for level in range(1, len(sizes)):
        size, half = sizes[level], sizes[level - 1]
        keep.append(((arow // size) == (acol // size)) & ((arow % size) >= half) & ((acol % size) < half))
    klane = (_iota2((c_len, dk), 1) // dkh).astype(BF16)
    vlane = (_iota2((c_len, dv), 1) // dvh).astype(BF16)
    diag = (_iota2((dv, dk), 0) // dvh) == (_iota2((dv, dk), 1) // dkh)
    sum_mask = jnp.concatenate(sum_masks, axis=0)
    return dict(sizes=sizes, sum_mask=jnp.concatenate([sum_mask, sum_mask], axis=1), keep=keep,
                klane=klane, vlane=vlane, diag=diag)


def _each(fn, *lists):
    return [fn(*args) for args in zip(*lists)]


def _round_robin(generators):
    live = list(generators)
    while live:
        still = []
        for gen in live:
            try:
                next(gen)
                still.append(gen)
            except StopIteration:
                pass
        live = still


def _gla_chunk(q, k, v, logf, state_t, consts):
    c_len = q[0].shape[0]
    sizes = consts["sizes"]
    n_levels = len(sizes)
    def prefix_sums(x):
        hi = x.astype(BF16)
        lo = (x - hi.astype(F32)).astype(BF16)
        return _dot(consts["sum_mask"], jnp.concatenate([hi, lo], axis=0))

    sums = _each(prefix_sums, logf)
    cs = [[s[(2 * i) * c_len:(2 * i + 1) * c_len] for i in range(n_levels)] for s in sums]
    tot = [[s[(2 * i + 1) * c_len:(2 * i + 2) * c_len] for i in range(n_levels)] for s in sums]

    def stack_heads(x, lane_head):
        x16 = x.astype(BF16)
        return jnp.concatenate([jnp.where(lane_head == h, x16, jnp.zeros_like(x16)) for h in range(N_HEADS)], axis=0)

    q0 = _each(lambda x, c: (x * jnp.exp(c[0])).astype(BF16), q, cs)
    k0 = _each(lambda x, c: stack_heads(x * jnp.exp(-c[0]), consts["klane"]), k, cs)
    att = _each(lambda a, b: jnp.where(consts["keep"][0], _dot_nt(a, b), 0.0), q0, k0)
    for level in range(1, n_levels):
        q_l = q0 if level == 1 else _each(lambda x, c: (x * jnp.exp(c[level - 1])).astype(BF16), q, cs)
        k_l = _each(lambda x, c, t: stack_heads(x * jnp.exp(t[level - 1] - c[level - 1]), consts["klane"]), k, cs, tot)
        att = _each(lambda a, b, prev: jnp.where(consts["keep"][level], _dot_nt(a, b), prev), q_l, k_l, att)
    v_stack = _each(lambda x: stack_heads(x, consts["vlane"]), v)
    o = _each(lambda a, b: _dot(a.astype(BF16), b), att, v_stack)

    q_s = _each(lambda x, c: (x * jnp.exp(c[-1])).astype(BF16), q, cs)
    k_s = _each(lambda x, c, t: (x * jnp.exp(t[-1] - c[-1])).astype(BF16), k, cs, tot)
    o = _each(lambda acc, a, s: acc + _dot_nt(a, s.astype(BF16)), o, q_s, state_t)
    d_state = _each(lambda x, y: jnp.where(consts["diag"], _dot_tn(x.astype(BF16), y), 0.0), v, k_s)
    new_state = _each(lambda s, t, d: s * jnp.exp(t[-1][0:1, :]) + d, state_t, tot, d_state)
    return o, new_state


def _gla_kernel(gla_ref, small_ref, wg_ref, bg_ref, on_ref, o_ref, st_ref):
    c_len = GLA_CHUNK
    dk = GLA_KEY_DIM
    seq = gla_ref.shape[1]
    _zero_at_first_segment(st_ref)
    scale = (dk // N_HEADS) ** -0.5
    consts = _gla_consts(c_len, dk, GROUP_WIDTH, c_len)

    def chunk(n, _):
        rows = pl.ds(pl.multiple_of(n * c_len, c_len), c_len)
        batch = range(gla_ref.shape[0])
        gate = [_dot(small_ref[gb, rows, :].astype(BF16), wg_ref[...]) + bg_ref[...] for gb in batch]
        logf = [_log_sigmoid(x) * (1.0 / GLA_TAU) for x in gate]
        q = [gla_ref[gb, rows, 0:dk].astype(F32) * scale for gb in batch]
        k = [gla_ref[gb, rows, dk:2 * dk].astype(F32) for gb in batch]
        v = [gla_ref[gb, rows, 2 * dk:2 * dk + GROUP_WIDTH].astype(F32) for gb in batch]
        o, new_state = _gla_chunk(q, k, v, logf, [st_ref[gb] for gb in batch], consts)
        for gb in batch:
            st_ref[gb] = new_state[gb]
            r = gla_ref[gb, rows, 2 * dk + GROUP_WIDTH:].astype(F32)
            o_ref[gb, rows, :] = (_rms_heads64_wide(o[gb], on_ref[...]) * _silu(r)).astype(o_ref.dtype)
        return 0

    lax.fori_loop(0, seq // c_len, chunk, 0)


def _gla(gla3, small3, wg, bg, on):
    b, seq, width = gla3.shape
    nb = min(GLA_BATCH, b)
    seg = min(GLA_SEGMENT, seq)
    const = lambda bi, si: (0, 0)
    return pl.pallas_call(
        _gla_kernel,
        grid=(b // nb, seq // seg),
        in_specs=[
            _segment_spec(nb, seg, width), _segment_spec(nb, seg, LANES),
            pl.BlockSpec((LANES, GLA_KEY_DIM), const),
            pl.BlockSpec((1, GLA_KEY_DIM), const),
            pl.BlockSpec((1, GROUP_WIDTH), const),
        ],
        out_specs=_segment_spec(nb, seg, GROUP_WIDTH),
        out_shape=jax.ShapeDtypeStruct((b, seq, GROUP_WIDTH), BF16),
        scratch_shapes=[pltpu.VMEM((nb, GROUP_WIDTH, GLA_KEY_DIM), F32)],
        compiler_params=_cparams(("parallel", "arbitrary")),
        name="gla",
    )(gla3, small3, wg, bg, on)


def _hgrn_kernel(hg_ref, logf_ref, on_ref, o_ref, st_ref):
    c_len = HGRN_CHUNK
    dk = HGRN_FDIM
    seq = hg_ref.shape[1]
    _zero_at_first_segment(st_ref)
    consts = _gla_consts(c_len, dk, GROUP_WIDTH, HGRN_SUB)

    def chunk(n, _):
        rows = pl.ds(pl.multiple_of(n * c_len, c_len), c_len)
        batch = range(hg_ref.shape[0])
        logf = [logf_ref[gb, rows, :] for gb in batch]
        k = [hg_ref[gb, rows, dk:2 * dk].astype(F32) for gb in batch]
        q = [hg_ref[gb, rows, 0:dk].astype(F32) for gb in batch]
        v = [hg_ref[gb, rows, 2 * dk:2 * dk + GROUP_WIDTH].astype(F32) for gb in batch]
        o, new_state = _gla_chunk(q, k, v, logf, [st_ref[gb] for gb in batch], consts)
        for gb in batch:
            st_ref[gb] = new_state[gb]
            g = hg_ref[gb, rows, 2 * dk + GROUP_WIDTH:].astype(F32)
            o_ref[gb, rows, :] = (_rms_heads64_wide(o[gb], on_ref[...]) * _silu(g)).astype(o_ref.dtype)
        return 0

    lax.fori_loop(0, seq // c_len, chunk, 0)


def _hgrn(hg3, logf3, on):
    b, seq, width = hg3.shape
    nb = min(HGRN_BATCH, b)
    seg = min(HGRN_SEGMENT, seq)
    const = lambda bi, si: (0, 0)
    return pl.pallas_call(
        _hgrn_kernel,
        grid=(b // nb, seq // seg),
        in_specs=[
            _segment_spec(nb, seg, width),
            _segment_spec(nb, seg, HGRN_FDIM),
            pl.BlockSpec((1, GROUP_WIDTH), const),
        ],
        out_specs=_segment_spec(nb, seg, GROUP_WIDTH),
        out_shape=jax.ShapeDtypeStruct((b, seq, GROUP_WIDTH), BF16),
        scratch_shapes=[pltpu.VMEM((nb, GROUP_WIDTH, HGRN_FDIM), F32)],
        compiler_params=_cparams(("parallel", "arbitrary")),
        name="hgrn2",
    )(hg3, logf3, on)


def _kv_kernel(mem_ref, mn_ref, w_ref, kn_ref, k_ref, v_ref):
    mem_n = _rms(mem_ref[...], mn_ref[...]).astype(BF16)
    kv = _dot(mem_n, w_ref[...])
    for h in range(XA_HEADS):
        cols = slice(h * XA_HEAD_DIM, (h + 1) * XA_HEAD_DIM)
        k_ref[:, cols] = _rms(kv[:, cols], kn_ref[...]).astype(k_ref.dtype)
    v_ref[...] = kv[:, D_MODEL:].astype(v_ref.dtype)


def _kv_proj(mem2d, mem_norm, w_kv, kn, tm):
    m = mem2d.shape[0]
    depth = w_kv.shape[0]
    out = jax.ShapeDtypeStruct((depth, m, D_MODEL), BF16)
    return pl.pallas_call(
        _kv_kernel,
        grid=(depth, m // tm),
        in_specs=[
            pl.BlockSpec((tm, D_MODEL), lambda l, i: (i, 0)),
            pl.BlockSpec((1, D_MODEL), lambda l, i: (0, 0)),
            pl.BlockSpec((None, D_MODEL, 2 * D_MODEL), lambda l, i: (l, 0, 0)),
            pl.BlockSpec((None, 1, XA_HEAD_DIM), lambda l, i: (l, 0, 0)),
        ],
        out_specs=[pl.BlockSpec((None, tm, D_MODEL), lambda l, i: (l, i, 0))] * 2,
        out_shape=[out, out],
        compiler_params=_cparams(("parallel", "parallel")),
        name="mem_kv",
    )(mem2d, mem_norm, w_kv, kn)


def _mix_xattn_kernel(x_ref, yf_ref, ys_ref, yg_ref, yh_ref, fon_ref, wout_ref, ln_ref, wq_ref, qn_ref,
                      k_ref, v_ref, wo_ref, o_ref):
    y_fox = _rms_heads64_wide(yf_ref[...].astype(F32), fon_ref[...]).astype(BF16)
    mixed = jnp.concatenate([y_fox, ys_ref[...], yg_ref[...], yh_ref[...]], axis=-1)
    x = x_ref[...] + _dot(mixed, wout_ref[...])
    h = _rms(x, ln_ref[...]).astype(BF16)
    q = _dot(h, wq_ref[...])
    scale = XA_HEAD_DIM ** -0.5
    outs = []
    for hd in range(XA_HEADS):
        cols = slice(hd * XA_HEAD_DIM, (hd + 1) * XA_HEAD_DIM)
        qh = (_rms(q[:, cols], qn_ref[...]) * scale).astype(BF16)
        logits = _dot_nt(qh, k_ref[:, cols])
        p = jnp.exp(logits - jnp.max(logits, axis=-1, keepdims=True))
        p = p / jnp.sum(p, axis=-1, keepdims=True)
        outs.append(_dot(p.astype(BF16), v_ref[:, cols]).astype(BF16))
    o = jnp.concatenate(outs, axis=-1)
    o_ref[...] = x + _dot(o, wo_ref[...])


def _mix_xattn(x2d, ys, fox_on, w_out, ln, wq, qn, k3, v3, wo, tm, seq):
    m = x2d.shape[0]
    n_mem = k3.shape[1]
    tiles_per_seq = seq // tm
    row = lambda i: (i, 0)
    const = lambda i: (0, 0)
    mem_spec = pl.BlockSpec((None, n_mem, D_MODEL), lambda i: (i // tiles_per_seq, 0, 0))
    return pl.pallas_call(
        _mix_xattn_kernel,
        grid=(m // tm,),
        in_specs=[
            pl.BlockSpec((tm, D_MODEL), row),
            *[pl.BlockSpec((tm, GROUP_WIDTH), row)] * 4,
            pl.BlockSpec((1, GROUP_WIDTH), const),
            _resident((D_MODEL, D_MODEL)),
            pl.BlockSpec((1, D_MODEL), const),
            _resident((D_MODEL, D_MODEL)),
            pl.BlockSpec((1, XA_HEAD_DIM), const),
            mem_spec, mem_spec,
            _resident((D_MODEL, D_MODEL)),
        ],
        out_specs=pl.BlockSpec((tm, D_MODEL), row),
        out_shape=jax.ShapeDtypeStruct((m, D_MODEL), F32),
        compiler_params=_cparams(("parallel",)),
        name="mix_xattn",
    )(x2d, *ys, fox_on, w_out, ln, wq, qn, k3, v3, wo)


def _ffn_kernel(x_ref, xp_ref, ln_ref, wup_ref, cw_ref, cb_ref, wdown_ref, o_ref, act_ref, *, tiles_per_seq):
    i = pl.program_id(0)
    x = x_ref[...]
    first = (i % tiles_per_seq) == 0
    h = _rms(x, ln_ref[...]).astype(BF16)
    hp = jnp.where(first, 0.0, _rms(xp_ref[...], ln_ref[...])).astype(BF16)
    h_ext = jnp.concatenate([hp, h], axis=0)
    for c in range(D_FF // FF_TILE):
        cols = slice(c * FF_TILE, (c + 1) * FF_TILE)
        vcols = slice(D_FF + c * FF_TILE, D_FF + (c + 1) * FF_TILE)
        gate = _dot(h_ext, wup_ref[:, cols])
        conv = cb_ref[:, cols] + cw_ref[FFN_CONV - 1:FFN_CONV, cols] * gate[HALO:, :]
        for shift in range(1, FFN_CONV):
            conv = conv + (cw_ref[FFN_CONV - 1 - shift:FFN_CONV - shift, cols]
                           * pltpu.roll(gate, shift, 0)[HALO:, :])
        val = _dot(h, wup_ref[:, vcols])
        act_ref[:, cols] = (_silu(conv) * val).astype(BF16)
    o_ref[...] = x + _dot(act_ref[...], wdown_ref[...])


def _ffn(x2d, ln, w_up, cw, cb, w_down, tm, seq):
    m = x2d.shape[0]
    kern = functools.partial(_ffn_kernel, tiles_per_seq=seq // tm)
    const = lambda i: (0, 0)
    halo_blocks = tm // HALO
    return pl.pallas_call(
        kern,
        grid=(m // tm,),
        in_specs=[
            pl.BlockSpec((tm, D_MODEL), lambda i: (i, 0)),
            pl.BlockSpec((HALO, D_MODEL), lambda i: (jnp.maximum(i * halo_blocks - 1, 0), 0)),
            pl.BlockSpec((1, D_MODEL), const),
            _resident((D_MODEL, 2 * D_FF)),
            pl.BlockSpec((FFN_CONV, D_FF), const),
            pl.BlockSpec((1, D_FF), const),
            _resident((D_FF, D_MODEL)),
        ],
        out_specs=pl.BlockSpec((tm, D_MODEL), lambda i: (i, 0)),
        out_shape=jax.ShapeDtypeStruct((m, D_MODEL), F32),
        scratch_shapes=[pltpu.VMEM((tm, D_FF), BF16)],
        compiler_params=_cparams(("parallel",)),
        name="conv_glu_ffn",
    )(x2d, x2d, ln, w_up, cw, cb, w_down)


def _pad_lanes(vec, offset, total=LANES):
    vec = vec.astype(F32).reshape(1, -1)
    return jnp.pad(vec, ((0, 0), (offset, total - offset - vec.shape[1])))


def _arrange_w_in(w):
    fq, fk, fv, ff, sz, sxbc, sdt, gq, gk, gv, ga, gr, hq, hf, hi, hg = jnp.split(
        w, [256, 512, 768, 772, 1028, 1540, 1544, 1672, 1800, 2056, 2072, 2328, 2584, 2840, 3096], axis=1)
    small = jnp.concatenate([ff, sdt, ga], axis=1)
    small = jnp.pad(small, ((0, 0), (0, LANES - small.shape[1])))
    return jnp.concatenate([fq, fk, fv, sz, sxbc, gq, gk, gv, gr, hq, hf, hi, hg, small], axis=1).astype(BF16)


def kernel(x, mem, ln_mix, w_in, w_out, fox_f_bias, fox_qn, fox_kn, fox_on, ssd_conv_w, ssd_conv_b, ssd_dt_bias,
           ssd_a_log, ssd_d, ssd_norm, gla_w_g2, gla_b_g2, gla_norm, hgrn_lb_logits, hgrn_norm, ln_xattn, mem_norm,
           xa_wq, xa_wkv, xa_wo, xa_qn, xa_kn, ln_ffn, ffn_w_up, ffn_conv_w, ffn_conv_b, ffn_w_down):
    b, seq, d = x.shape
    depth = w_in.shape[0]
    n_mem = mem.shape[1]
    m = b * seq
    tm = min(ROW_TILE, seq)
    x2d = x.reshape(m, d)

    k_all, v_all = _kv_proj(mem.reshape(b * n_mem, d), mem_norm.reshape(1, d), xa_wkv.astype(BF16),
                            xa_kn.reshape(depth, 1, XA_HEAD_DIM), min(512, b * n_mem))
    for l in range(depth):
        tile4 = lambda v: jnp.tile(v.astype(F32).reshape(1, -1), (1, GROUP_WIDTH // HEAD_DIM))
        fox, ssd, gla, hgrn, small, hgrn_logf = _in_proj(x2d, ln_mix[l].reshape(1, d), _arrange_w_in(w_in[l]),
                                                         tile4(fox_qn[l]), tile4(fox_kn[l]), hgrn_lb_logits, l, tm)
        small3 = small.reshape(b, seq, LANES)
        y_fox = _fox(fox.reshape(b, seq, -1), small3, _pad_lanes(fox_f_bias[l], SMALL_FF))
        y_ssd = _ssd(ssd.reshape(b, seq, -1), small3, ssd_conv_w[l], ssd_conv_b[l].reshape(1, -1),
                     _pad_lanes(ssd_dt_bias[l], SMALL_DT), _pad_lanes(ssd_a_log[l], SMALL_DT),
                     jnp.repeat(ssd_d[l].astype(F32), HEAD_DIM).reshape(1, -1), ssd_norm[l].reshape(1, -1))
        wg = jnp.pad(gla_w_g2[l], ((SMALL_GA, LANES - SMALL_GA - GLA_GATE_RANK), (0, 0))).astype(BF16)
        y_gla = _gla(gla.reshape(b, seq, -1), small3, wg, gla_b_g2[l].reshape(1, -1), tile4(gla_norm[l]))
        y_hgrn = _hgrn(hgrn.reshape(b, seq, -1), hgrn_logf.reshape(b, seq, -1), tile4(hgrn_norm[l]))
        ys = [y.reshape(m, GROUP_WIDTH) for y in (y_fox, y_ssd, y_gla, y_hgrn)]
        x2d = _mix_xattn(x2d, ys, tile4(fox_on[l]), w_out[l].astype(BF16), ln_xattn[l].reshape(1, d),
                         xa_wq[l].astype(BF16),
                         xa_qn[l].reshape(1, -1), k_all[l].reshape(b, n_mem, d), v_all[l].reshape(b, n_mem, d),
                         xa_wo[l].astype(BF16), tm, seq)
        x2d = _ffn(x2d, ln_ffn[l].reshape(1, d), ffn_w_up[l].astype(BF16), ffn_conv_w[l],
                   ffn_conv_b[l].reshape(1, -1), ffn_w_down[l].astype(BF16), min(FFN_ROWS, seq), seq)
    return x2d.reshape(b, seq, d)
```
